```python
import jax, jax.numpy as jnp
from jax import lax
import numpy as np

D_MODEL = 1024
BATCH = 8
SEQ = 4096
DEPTH = 2

HEAD_DIM = 64
SB_HEADS = 4
DSA_HEADS = 4
NSA_HEADS = 4
BRANCH_WIDTH = 4 * HEAD_DIM
IDX_HEADS = 4
IDX_DIM = 64
DSA_TOPK = 256
CMP_LEN = 32
CMP_STRIDE = 16
CMP_HIDDEN = 128
SEL_LEN = 64
SEL_TOPN = 16
WINDOW = 512
N_MEM = 256
MEM_HEADS = 4
MEM_HEAD_DIM = 64
D_FF = ((8 * D_MODEL + 3 * 256 - 1) // (3 * 256)) * 256
Q_BLOCK = 128
ROPE_THETA = 10000.0
RMS_EPS = 1e-6
FORCED_SCORE = 1e4

IN_SPLITS = (
    ('q_a', SB_HEADS * HEAD_DIM), ('k_a', SB_HEADS * HEAD_DIM), ('v_a', SB_HEADS * HEAD_DIM),
    ('q_b', DSA_HEADS * HEAD_DIM), ('k_b', HEAD_DIM), ('v_b', HEAD_DIM),
    ('q_idx', IDX_HEADS * IDX_DIM), ('k_idx', IDX_DIM), ('w_idx', IDX_HEADS),
    ('q_c', NSA_HEADS * HEAD_DIM), ('k_cmp', HEAD_DIM), ('v_cmp', HEAD_DIM),
    ('k_sel', HEAD_DIM), ('v_sel', HEAD_DIM), ('k_win', HEAD_DIM), ('v_win', HEAD_DIM),
    ('g_c', NSA_HEADS * 3),
    ('g_merge', 3 * D_MODEL),
)
D_IN = sum(w for _, w in IN_SPLITS)

kernel_name = 'hybrid_sb_dsa_nsa_gated_block'


def _rmsnorm(x, g):
    xf = x.astype(jnp.float32)
    y = xf * lax.rsqrt(jnp.mean(xf * xf, axis=-1, keepdims=True) + RMS_EPS)
    return (y * g.astype(jnp.float32)).astype(x.dtype)


def _rope(x):
    S, hd = x.shape[1], x.shape[-1]
    inv = ROPE_THETA ** (-jnp.arange(0, hd, 2, dtype=jnp.float32) / hd)
    ang = jnp.arange(S, dtype=jnp.float32)[:, None] * inv[None, :]
    cos = jnp.cos(ang)[None, :, None, :]
    sin = jnp.sin(ang)[None, :, None, :]
    xf = x.astype(jnp.float32)
    x1, x2 = xf[..., : hd // 2], xf[..., hd // 2:]
    return jnp.concatenate([x1 * cos - x2 * sin, x2 * cos + x1 * sin], axis=-1).astype(x.dtype)


def _masked_softmax(s, mask):
    s = jnp.where(mask, s.astype(jnp.float32), -jnp.inf)
    m = jnp.max(s, axis=-1, keepdims=True)
    m = jnp.where(jnp.isfinite(m), m, 0.0)
    e = jnp.where(mask, jnp.exp(s - m), 0.0)
    den = jnp.sum(e, axis=-1, keepdims=True)
    return e / jnp.maximum(den, 1e-30)


def _sweep(block_fn, S):
    out = lax.map(block_fn, jnp.arange(S // Q_BLOCK))
    out = jnp.moveaxis(out, 0, 1)
    return out.reshape((out.shape[0], S) + out.shape[3:])


def _split_columns(p):
    parts, off = {}, 0
    for name, w in IN_SPLITS:
        parts[name] = p[..., off:off + w]
        off += w
    return parts


def _stick_breaking(q, k, v):
    B, S, H, hd = q.shape
    scale = hd ** -0.5
    kpos = jnp.arange(S)

    def block(i):
        q0 = i * Q_BLOCK
        qpos = q0 + jnp.arange(Q_BLOCK)
        qb = lax.dynamic_slice_in_dim(q, q0, Q_BLOCK, axis=1)
        z = jnp.einsum('bqhd,bkhd->bhqk', qb, k).astype(jnp.float32) * scale
        strict = kpos[None, :] < qpos[:, None]
        log_1m = jnp.where(strict, jax.nn.log_sigmoid(-z), 0.0)
        suffix = lax.cumsum(log_1m, axis=3, reverse=True) - log_1m
        w = jnp.where(strict, jnp.exp(jax.nn.log_sigmoid(z) + suffix), 0.0)
        return jnp.einsum('bhqk,bkhd->bqhd', w.astype(v.dtype), v)

    return _sweep(block, S)


def _dsa_attention(q, k, v, q_idx, k_idx, w_idx):
    B, S, H, hd = q.shape
    topk = min(DSA_TOPK, S // 4)
    kpos = jnp.arange(S)
    bidx = jnp.arange(B)[:, None, None]
    w_scale = (IDX_HEADS * IDX_DIM) ** -0.5
    scale = hd ** -0.5

    def block(i):
        q0 = i * Q_BLOCK
        qpos = q0 + jnp.arange(Q_BLOCK)
        qi = lax.dynamic_slice_in_dim(q_idx, q0, Q_BLOCK, axis=1)
        wi = lax.dynamic_slice_in_dim(w_idx, q0, Q_BLOCK, axis=1).astype(jnp.float32) * w_scale
        rel = jax.nn.relu(jnp.einsum('bqhd,bkd->bqhk', qi, k_idx).astype(jnp.float32))
        score = jnp.einsum('bqh,bqhk->bqk', wi, rel)
        score = jnp.where(kpos[None, None, :] <= qpos[None, :, None], score, -jnp.inf)
        _, sel = lax.top_k(score, topk)
        kg = k[bidx, sel]
        vg = v[bidx, sel]
        qb = lax.dynamic_slice_in_dim(q, q0, Q_BLOCK, axis=1)
        s = jnp.einsum('bqhd,bqkd->bhqk', qb, kg).astype(jnp.float32) * scale
        valid = (sel <= qpos[None, :, None])[:, None]
        p = _masked_softmax(s, valid)
        return jnp.einsum('bhqk,bqkd->bqhd', p.astype(v.dtype), vg)

    return _sweep(block, S)


def _compress(tok, pos_emb, w1, w2):
    B, S, hd = tok.shape
    n_cmp = (S - CMP_LEN) // CMP_STRIDE + 1
    idx = np.arange(n_cmp)[:, None] * CMP_STRIDE + np.arange(CMP_LEN)[None, :]
    blocks = tok[:, idx] + pos_emb
    hid = jax.nn.gelu(blocks.reshape(B, n_cmp, CMP_LEN * hd) @ w1)
    return hid @ w2


def _nsa_attention(q, k_cmp, v_cmp, k_sel, v_sel, k_win, v_win, gates,
                   cmp_pos_k, cmp_w1_k, cmp_w2_k, cmp_pos_v, cmp_w1_v, cmp_w2_v):
    B, S, H, hd = q.shape
    scale = hd ** -0.5
    kc = _compress(k_cmp, cmp_pos_k, cmp_w1_k, cmp_w2_k)
    vc = _compress(v_cmp, cmp_pos_v, cmp_w1_v, cmp_w2_v)
    n_cmp = kc.shape[1]
    cmp_end = jnp.arange(n_cmp) * CMP_STRIDE + CMP_LEN - 1
    n_sel = S // SEL_LEN
    topn = min(SEL_TOPN, n_sel)
    cs = np.arange(n_cmp) * CMP_STRIDE
    ss = np.arange(n_sel) * SEL_LEN
    overlap = jnp.asarray(((cs[:, None] < ss[None, :] + SEL_LEN) &
                           (cs[:, None] + CMP_LEN > ss[None, :])).astype(np.float32))
    k_blocks = k_sel.reshape(B, n_sel, SEL_LEN, hd)
    v_blocks = v_sel.reshape(B, n_sel, SEL_LEN, hd)
    k_pad = jnp.pad(k_win, ((0, 0), (WINDOW, 0), (0, 0)))
    v_pad = jnp.pad(v_win, ((0, 0), (WINDOW, 0), (0, 0)))
    sel_ids = jnp.arange(n_sel)
    offs = jnp.arange(SEL_LEN)
    band_offs = jnp.arange(WINDOW + Q_BLOCK)
    bidx = jnp.arange(B)[:, None, None]

    def block(i):
        q0 = i * Q_BLOCK
        qpos = q0 + jnp.arange(Q_BLOCK)
        qb = lax.dynamic_slice_in_dim(q, q0, Q_BLOCK, axis=1)
        gb = lax.dynamic_slice_in_dim(gates, q0, Q_BLOCK, axis=1)
        s_c = jnp.einsum('bqhd,bnd->bhqn', qb, kc).astype(jnp.float32) * scale
        p_c = _masked_softmax(s_c, (cmp_end[None, :] <= qpos[:, None])[None, None])
        o_cmp = jnp.einsum('bhqn,bnd->bqhd', p_c.astype(vc.dtype), vc)
        imp = jnp.einsum('bhqn,nj->bqj', p_c, overlap)
        cur = qpos // SEL_LEN
        forced = ((sel_ids[None, :] == 0) | (sel_ids[None, :] == cur[:, None]) |
                  (sel_ids[None, :] == cur[:, None] - 1))
        admissible = sel_ids[None, :] * SEL_LEN <= qpos[:, None]
        imp = jnp.where(forced[None], FORCED_SCORE, imp)
        imp = jnp.where(admissible[None], imp, -jnp.inf)
        _, sel = lax.top_k(imp, topn)
        kg = k_blocks[bidx, sel].reshape(B, Q_BLOCK, topn * SEL_LEN, hd)
        vg = v_blocks[bidx, sel].reshape(B, Q_BLOCK, topn * SEL_LEN, hd)
        tok = (sel[..., None] * SEL_LEN + offs).reshape(B, Q_BLOCK, topn * SEL_LEN)
        s_s = jnp.einsum('bqhd,bqkd->bhqk', qb, kg).astype(jnp.float32) * scale
        p_s = _masked_softmax(s_s, (tok <= qpos[None, :, None])[:, None])
        o_sel = jnp.einsum('bhqk,bqkd->bqhd', p_s.astype(vg.dtype), vg)
        kw = lax.dynamic_slice_in_dim(k_pad, q0, WINDOW + Q_BLOCK, axis=1)
        vw = lax.dynamic_slice_in_dim(v_pad, q0, WINDOW + Q_BLOCK, axis=1)
        wpos = q0 - WINDOW + band_offs
        diff = qpos[:, None] - wpos[None, :]
        wmask = (diff >= 0) & (diff < WINDOW) & (wpos[None, :] >= 0)
        s_w = jnp.einsum('bqhd,bkd->bhqk', qb, kw).astype(jnp.float32) * scale
        p_w = _masked_softmax(s_w, wmask[None, None])
        o_win = jnp.einsum('bhqk,bkd->bqhd', p_w.astype(vw.dtype), vw)
        return gb[..., 0:1] * o_cmp + gb[..., 1:2] * o_sel + gb[..., 2:3] * o_win

    return _sweep(block, S)


def _hybrid_mixer(u, w_in, cmp_pos_k, cmp_w1_k, cmp_w2_k, cmp_pos_v, cmp_w1_v, cmp_w2_v, w_up, w_out):
    B, S, _ = u.shape
    p = _split_columns(u @ w_in)

    def heads(t, h, d=HEAD_DIM):
        return t.reshape(B, S, h, d)

    def rope1(t):
        return _rope(t[:, :, None, :])[:, :, 0]

    o_a = _stick_breaking(heads(p['q_a'], SB_HEADS), heads(p['k_a'], SB_HEADS), heads(p['v_a'], SB_HEADS))
    o_b = _dsa_attention(_rope(heads(p['q_b'], DSA_HEADS)), rope1(p['k_b']), p['v_b'],
                         _rope(heads(p['q_idx'], IDX_HEADS, IDX_DIM)), rope1(p['k_idx']), p['w_idx'])
    gates_c = jax.nn.sigmoid(p['g_c'].reshape(B, S, NSA_HEADS, 3))
    o_c = _nsa_attention(_rope(heads(p['q_c'], NSA_HEADS)), rope1(p['k_cmp']), p['v_cmp'],
                         rope1(p['k_sel']), p['v_sel'], rope1(p['k_win']), p['v_win'], gates_c,
                         cmp_pos_k, cmp_w1_k, cmp_w2_k, cmp_pos_v, cmp_w1_v, cmp_w2_v)
    g = jax.nn.sigmoid(p['g_merge'].reshape(B, S, 3, D_MODEL))
    merged = (g[:, :, 0] * (o_a.reshape(B, S, BRANCH_WIDTH) @ w_up[0]) +
              g[:, :, 1] * (o_b.reshape(B, S, BRANCH_WIDTH) @ w_up[1]) +
              g[:, :, 2] * (o_c.reshape(B, S, BRANCH_WIDTH) @ w_up[2]))
    return merged @ w_out


def _memory_attention(h, mem, g_q, g_kv, w_q, w_kv, w_o):
    B, S, _ = h.shape
    M = mem.shape[1]
    q = (_rmsnorm(h, g_q) @ w_q).reshape(B, S, MEM_HEADS, MEM_HEAD_DIM)
    kv = (_rmsnorm(mem, g_kv) @ w_kv).reshape(B, M, 2, MEM_HEADS, MEM_HEAD_DIM)
    s = jnp.einsum('bshd,bmhd->bhsm', q, kv[:, :, 0]).astype(jnp.float32) * MEM_HEAD_DIM ** -0.5
    pr = jax.nn.softmax(s, axis=-1)
    o = jnp.einsum('bhsm,bmhd->bshd', pr.astype(h.dtype), kv[:, :, 1])
    return o.reshape(B, S, MEM_HEADS * MEM_HEAD_DIM) @ w_o


def _swiglu(u, w_in, w_out):
    a = u @ w_in
    return (jax.nn.silu(a[..., :D_FF]) * a[..., D_FF:]) @ w_out


def setup_inputs(seed: int = 0) -> dict:
    key = jax.random.key(seed)
    ks = jax.random.split(key, 21)

    def nrm(k, shape, fan_in):
        return jax.random.normal(k, shape, jnp.float32) * fan_in ** -0.5

    def gain(k, shape):
        return 1.0 + 0.05 * jax.random.normal(k, shape, jnp.float32)

    L = DEPTH
    mem_w = MEM_HEADS * MEM_HEAD_DIM
    return {
        'x': jax.random.normal(ks[0], (BATCH, SEQ, D_MODEL), jnp.float32),
        'mem': jax.random.normal(ks[1], (BATCH, N_MEM, D_MODEL), jnp.float32),
        'norm_mix': gain(ks[2], (L, D_MODEL)),
        'w_in': nrm(ks[3], (L, D_MODEL, D_IN), D_MODEL),
        'cmp_pos_k': 0.1 * jax.random.normal(ks[4], (L, CMP_LEN, HEAD_DIM), jnp.float32),
        'cmp_w1_k': nrm(ks[5], (L, CMP_LEN * HEAD_DIM, CMP_HIDDEN), CMP_LEN * HEAD_DIM),
        'cmp_w2_k': nrm(ks[6], (L, CMP_HIDDEN, HEAD_DIM), CMP_HIDDEN),
        'cmp_pos_v': 0.1 * jax.random.normal(ks[7], (L, CMP_LEN, HEAD_DIM), jnp.float32),
        'cmp_w1_v': nrm(ks[8], (L, CMP_LEN * HEAD_DIM, CMP_HIDDEN), CMP_LEN * HEAD_DIM),
        'cmp_w2_v': nrm(ks[9], (L, CMP_HIDDEN, HEAD_DIM), CMP_HIDDEN),
        'w_up': nrm(ks[10], (L, 3, BRANCH_WIDTH, D_MODEL), BRANCH_WIDTH),
        'w_out': nrm(ks[11], (L, D_MODEL, D_MODEL), D_MODEL),
        'norm_mem_q': gain(ks[12], (L, D_MODEL)),
        'norm_mem_kv': gain(ks[13], (L, D_MODEL)),
        'w_mem_q': nrm(ks[14], (L, D_MODEL, mem_w), D_MODEL),
        'w_mem_kv': nrm(ks[15], (L, D_MODEL, 2 * mem_w), D_MODEL),
        'w_mem_o': nrm(ks[16], (L, mem_w, D_MODEL), mem_w),
        'norm_ffn': gain(ks[17], (L, D_MODEL)),
        'w_ffn_in': nrm(ks[18], (L, D_MODEL, 2 * D_FF), D_MODEL),
        'w_ffn_out': nrm(ks[19], (L, D_FF, D_MODEL), D_FF),
        'norm_final': gain(ks[20], (D_MODEL,)),
    }


def reference(x, mem, norm_mix, w_in, cmp_pos_k, cmp_w1_k, cmp_w2_k, cmp_pos_v, cmp_w1_v, cmp_w2_v,
              w_up, w_out, norm_mem_q, norm_mem_kv, w_mem_q, w_mem_kv, w_mem_o,
              norm_ffn, w_ffn_in, w_ffn_out, norm_final):
    h = x
    for l in range(DEPTH):
        h = h + _hybrid_mixer(_rmsnorm(h, norm_mix[l]), w_in[l],
                              cmp_pos_k[l], cmp_w1_k[l], cmp_w2_k[l],
                              cmp_pos_v[l], cmp_w1_v[l], cmp_w2_v[l], w_up[l], w_out[l])
        h = h + _memory_attention(h, mem, norm_mem_q[l], norm_mem_kv[l], w_mem_q[l], w_mem_kv[l], w_mem_o[l])
        h = h + _swiglu(_rmsnorm(h, norm_ffn[l]), w_ffn_in[l], w_ffn_out[l])
    return _rmsnorm(h, norm_final)
```

```python
import functools

import jax
import jax.numpy as jnp
import numpy as np
from jax import lax
from jax.experimental import pallas as pl
from jax.experimental.pallas import tpu as pltpu

F32 = jnp.float32
MXU_DTYPE = jnp.bfloat16

D_MODEL = 1024
HEAD_DIM = 64
N_HEADS = 4
BRANCH_WIDTH = N_HEADS * HEAD_DIM
IDX_HEADS = 4
DSA_TOPK = 256
CMP_LEN = 32
CMP_STRIDE = 16
CMP_HIDDEN = 128
SEL_LEN = 64
SEL_TOPN = 16
WINDOW = 512
D_FF = 2816
ROPE_THETA = 10000.0
RMS_EPS = 1e-6
FORCED_SCORE = 1e4
Q_SCALE = HEAD_DIM ** -0.5
W_IDX_SCALE = (IDX_HEADS * HEAD_DIM) ** -0.5
NEG = -1e30

LANES = 128
SUBLANES = 8
VMEM_LIMIT = 56 * 1024 * 1024
TM = 512
TQ = 256

_OFF = {}
_o = 0
for _n, _w in (('q_a', 256), ('k_a', 256), ('v_a', 256), ('q_b', 256), ('k_b', 64), ('v_b', 64),
               ('q_idx', 256), ('k_idx', 64), ('w_idx', 4), ('q_c', 256), ('k_cmp', 64),
               ('v_cmp', 64), ('k_sel', 64), ('v_sel', 64), ('k_win', 64), ('v_win', 64),
               ('g_c', 12), ('g_merge', 3 * D_MODEL)):
    _OFF[_n] = (_o, _o + _w)
    _o += _w
D_IN = _o


def _mm(a, b):
    return jnp.dot(a.astype(MXU_DTYPE), b.astype(MXU_DTYPE), preferred_element_type=F32)


def _mm_nt(a, b):
    return lax.dot_general(a.astype(MXU_DTYPE), b.astype(MXU_DTYPE),
                           (((1,), (1,)), ((), ())), preferred_element_type=F32)


def _params(*sem):
    return pltpu.CompilerParams(dimension_semantics=sem, vmem_limit_bytes=VMEM_LIMIT)


def _const_spec(shape):
    nd = len(shape)
    return pl.BlockSpec(shape, lambda *_: (0,) * nd, pipeline_mode=pl.Buffered(1))


def _rmsnorm(x, g):
    return x * lax.rsqrt(jnp.mean(x * x, axis=-1, keepdims=True) + RMS_EPS) * g


def _rope_lanes(x, cos, sin):
    lane = lax.broadcasted_iota(jnp.int32, (1, LANES), 1)
    first_half = (lane % HEAD_DIM) < (HEAD_DIM // 2)
    out = []
    for g in range(x.shape[1] // LANES):
        xg = x[:, g * LANES:(g + 1) * LANES]
        rot = jnp.where(first_half, pltpu.roll(xg, LANES - HEAD_DIM // 2, 1),
                        pltpu.roll(xg, HEAD_DIM // 2, 1))
        out.append(xg * cos + rot * sin)
    return out


def _inproj_kernel(h_ref, g_ref, cos_ref, sin_ref, wa_ref, wr_ref, wv_ref, ws_ref, wc_ref,
                   a_ref, r_ref, v_ref, s_ref, c_ref):
    xn = _rmsnorm(h_ref[...], g_ref[...]).astype(MXU_DTYPE)
    cos, sin = cos_ref[...], sin_ref[...]
    lane = lax.broadcasted_iota(jnp.int32, (1, LANES), 1)

    ya = _mm(xn, wa_ref[...])
    a_ref[:, 0:BRANCH_WIDTH] = (ya[:, 0:BRANCH_WIDTH] * Q_SCALE).astype(a_ref.dtype)
    a_ref[:, BRANCH_WIDTH:] = ya[:, BRANCH_WIDTH:].astype(a_ref.dtype)

    yr = _rope_lanes(_mm(xn, wr_ref[...]), cos, sin)
    for g, og in enumerate(yr):
        if g in (0, 1, 4, 5):
            og = og * Q_SCALE
        r_ref[:, g * LANES:(g + 1) * LANES] = og.astype(r_ref.dtype)

    v_ref[...] = _mm(xn, wv_ref[...]).astype(v_ref.dtype)

    ys = _mm(xn, ws_ref[...])
    s_ref[...] = jnp.where(lane < IDX_HEADS, ys * W_IDX_SCALE, jax.nn.sigmoid(ys))

    yc = _mm(xn, wc_ref[...])
    c_ref[...] = jnp.where(lane < HEAD_DIM, _rope_lanes(yc, cos, sin)[0], yc)


def _in_projection(h, g, w_in, cos_t, sin_t, S):
    T = h.shape[0]
    cols = lambda *names: jnp.concatenate([w_in[:, _OFF[n][0]:_OFF[n][1]] for n in names], axis=1)
    pad = lambda w, n: jnp.pad(w, ((0, 0), (0, n - w.shape[1])))
    wa = cols('q_a', 'k_a', 'v_a').astype(MXU_DTYPE)
    wr = cols('q_b', 'q_idx', 'q_c', 'k_b', 'k_idx', 'k_sel', 'k_win').astype(MXU_DTYPE)
    wv = pad(cols('v_b', 'v_sel', 'v_win'), 256).astype(MXU_DTYPE)
    ws = pad(cols('w_idx', 'g_c'), LANES).astype(MXU_DTYPE)
    wc = cols('k_cmp', 'v_cmp').astype(MXU_DTYPE)
    nb = S // TM
    row = lambda w: pl.BlockSpec((TM, w), lambda i: (i, 0))
    pos = pl.BlockSpec((TM, LANES), lambda i: (i % nb, 0))
    return pl.pallas_call(
        _inproj_kernel,
        grid=(T // TM,),
        in_specs=[row(D_MODEL), _const_spec((1, D_MODEL)), pos, pos,
                  _const_spec(wa.shape), _const_spec(wr.shape), _const_spec(wv.shape),
                  _const_spec(ws.shape), _const_spec(wc.shape)],
        out_specs=[row(768), row(1024), row(256), row(LANES), row(LANES)],
        out_shape=[jax.ShapeDtypeStruct((T, 768), MXU_DTYPE),
                   jax.ShapeDtypeStruct((T, 1024), MXU_DTYPE),
                   jax.ShapeDtypeStruct((T, 256), MXU_DTYPE),
                   jax.ShapeDtypeStruct((T, LANES), F32),
                   jax.ShapeDtypeStruct((T, LANES), F32)],
        compiler_params=_params("parallel"),
        name="in_projection",
    )(h, g.reshape(1, D_MODEL), cos_t, sin_t, wa, wr, wv, ws, wc)


def _sb_kernel(q_ref, k_ref, v_ref, tri_ref, o_ref):
    i = pl.program_id(1)
    row = lax.broadcasted_iota(jnp.int32, (TQ, TQ), 0)
    col = lax.broadcasted_iota(jnp.int32, (TQ, TQ), 1)
    strict = col < row
    tri = tri_ref[...]

    for h in range(N_HEADS):
        q = q_ref[0, h]

        def chunk(j, c, acc, diag):
            ks = pl.ds(pl.multiple_of(j * TQ, TQ), TQ)
            z = _mm_nt(q, k_ref[0, h, ks, :])
            sp = jnp.maximum(z, 0.0) + jnp.log(1.0 + jnp.exp(-jnp.abs(z)))
            if diag:
                sp = jnp.where(strict, sp, 0.0)
            hi = sp.astype(MXU_DTYPE)
            lo = (sp - hi.astype(F32)).astype(MXU_DTYPE)
            cum = _mm(hi, tri) + _mm(lo, tri)
            w = jnp.exp(z - cum - c)
            if diag:
                w = jnp.where(strict, w, 0.0)
            acc = acc + _mm(w, v_ref[0, h, ks, :])
            return c + cum[:, 0:1], acc

        c, acc = chunk(i, jnp.zeros((TQ, 1), F32), jnp.zeros((TQ, HEAD_DIM), F32), True)
        c, acc = lax.fori_loop(0, i, lambda t, ca: chunk(i - 1 - t, ca[0], ca[1], False), (c, acc))
        o_ref[0, :, h * HEAD_DIM:(h + 1) * HEAD_DIM] = acc


def _stick_breaking(q, k, v):
    B, H, S, _ = q.shape
    tri = jnp.asarray(np.tril(np.ones((TQ, TQ), np.float32)), MXU_DTYPE)
    kv_spec = pl.BlockSpec((1, H, S, HEAD_DIM), lambda b, i: (b, 0, 0, 0))
    return pl.pallas_call(
        _sb_kernel,
        grid=(B, S // TQ),
        in_specs=[pl.BlockSpec((1, H, TQ, HEAD_DIM), lambda b, i: (b, 0, i, 0)), kv_spec, kv_spec,
                  _const_spec((TQ, TQ))],
        out_specs=pl.BlockSpec((1, TQ, BRANCH_WIDTH), lambda b, i: (b, i, 0)),
        out_shape=jax.ShapeDtypeStruct((B, S, BRANCH_WIDTH), F32),
        compiler_params=_params("parallel", "arbitrary"),
        name="stick_breaking",
    )(q, k, v, tri)


def _colsum8(x):
    return x.reshape(x.shape[0] // SUBLANES, SUBLANES, x.shape[1]).sum(axis=0)


def _softmax_step(s, vT, m, l, acc):
    m_new = jnp.maximum(m, jnp.max(s, axis=0, keepdims=True))
    alpha = jnp.exp(m - m_new)
    p = jnp.exp(s - m_new)
    l = alpha * l + jnp.sum(p, axis=0, keepdims=True)
    acc = alpha * acc + _mm(vT, p)
    return m_new, l, acc


def _softmax_init():
    return tuple((jnp.full((1, TQ), NEG, F32), jnp.zeros((1, TQ), F32), jnp.zeros((HEAD_DIM, TQ), F32))
                 for _ in range(N_HEADS))


def _dsa_kernel(qi_ref, ki_ref, wg_ref, q_ref, k_ref, vT_ref, o_ref, sc_ref, *, topk, idx_bits):
    i = pl.program_id(1)
    nchunks = i + 1
    qpos = i * TQ + lax.broadcasted_iota(jnp.int32, (1, TQ), 1)
    rowi = lax.broadcasted_iota(jnp.int32, (TQ, TQ), 0)
    int_min = jnp.int32(-2 ** 31)

    def chunk_slice(j):
        return pl.ds(pl.multiple_of(j * TQ, TQ), TQ)

    qi = qi_ref[0]
    wT = wg_ref[0]

    def idx_chunk(j, _):
        kx = ki_ref[0, chunk_slice(j), :]
        acc = jnp.zeros((TQ, TQ), F32)
        for hh in range(IDX_HEADS):
            r = _mm_nt(kx, qi[:, hh * HEAD_DIM:(hh + 1) * HEAD_DIM])
            acc = acc + wT[hh:hh + 1, :] * jnp.maximum(r, 0.0)
        sc_ref[chunk_slice(j), :] = jnp.where(j * TQ + rowi <= qpos, acc, -jnp.inf)
        return 0

    lax.fori_loop(0, nchunks, idx_chunk, 0)

    def count(pred):
        def body(j, acc):
            return acc + _colsum8(jnp.where(pred(sc_ref[chunk_slice(j), :], j), 1.0, 0.0))
        return jnp.sum(lax.fori_loop(0, nchunks, body, jnp.zeros((SUBLANES, TQ), F32)),
                       axis=0, keepdims=True)

    def search():
        def key_to_float(u):
            s = u ^ int_min
            return lax.bitcast_convert_type(jnp.where(s < 0, s ^ jnp.int32(0x7FFFFFFF), s), F32)

        def bit_body(t, res):
            cand = res | jnp.left_shift(jnp.int32(1), 31 - t)
            thr = key_to_float(cand)
            return jnp.where(count(lambda blk, j: blk >= thr) >= topk, cand, res)

        thr = key_to_float(lax.fori_loop(0, 32, bit_body, jnp.zeros((1, TQ), jnp.int32)))
        n_gt = count(lambda blk, j: blk > thr)
        n_ge = count(lambda blk, j: blk >= thr)
        need = topk - n_gt
        has_tie = jnp.max(n_ge - topk) > 0.0

        def tie_limit():
            def bit_body(t, res):
                cand = res | jnp.left_shift(jnp.int32(1), idx_bits - 1 - t)
                n = count(lambda blk, j: (blk == thr) & (j * TQ + rowi < cand))
                return jnp.where(n <= need, cand, res)
            return lax.fori_loop(0, idx_bits, bit_body, jnp.zeros((1, TQ), jnp.int32))

        limit = lax.cond(has_tie, tie_limit, lambda: jnp.full((1, TQ), 2 ** 30, jnp.int32))
        return thr, limit

    thr, limit = lax.cond(i * TQ + TQ > topk, search,
                          lambda: (jnp.full((1, TQ), -jnp.inf, F32), jnp.zeros((1, TQ), jnp.int32)))
    take_all = qpos < topk
    thr = jnp.where(take_all, -jnp.inf, thr)
    limit = jnp.where(take_all, 0, limit)

    q = q_ref[0]

    def att_chunk(j, carry):
        blk = sc_ref[chunk_slice(j), :]
        sel = (blk > thr) | ((blk == thr) & (j * TQ + rowi < limit))
        bias = jnp.where(sel, 0.0, NEG)
        k = k_ref[0, chunk_slice(j), :]
        vT = vT_ref[0, j]
        return tuple(_softmax_step(_mm_nt(k, q[:, h * HEAD_DIM:(h + 1) * HEAD_DIM]) + bias, vT, *carry[h])
                     for h in range(N_HEADS))

    state = lax.fori_loop(0, nchunks, att_chunk, _softmax_init())
    oT = jnp.concatenate([acc / l for (_, l, acc) in state], axis=0)
    o_ref[0] = oT.T


def _dsa_attention(r, wg, kidx, kb, vbT, B, S):
    topk = min(DSA_TOPK, S // 4)
    assert topk <= TQ
    k_spec = pl.BlockSpec((1, S, HEAD_DIM), lambda b, i: (b, 0, 0))
    return pl.pallas_call(
        functools.partial(_dsa_kernel, topk=topk, idx_bits=S.bit_length()),
        grid=(B, S // TQ),
        in_specs=[pl.BlockSpec((1, TQ, BRANCH_WIDTH), lambda b, i: (b, i, 1)), k_spec,
                  pl.BlockSpec((1, 16, TQ), lambda b, i: (b, 0, i)),
                  pl.BlockSpec((1, TQ, BRANCH_WIDTH), lambda b, i: (b, i, 0)), k_spec,
                  pl.BlockSpec((1, S // TQ, HEAD_DIM, TQ), lambda b, i: (b, 0, 0, 0))],
        out_specs=pl.BlockSpec((1, TQ, BRANCH_WIDTH), lambda b, i: (b, i, 0)),
        out_shape=jax.ShapeDtypeStruct((B, S, BRANCH_WIDTH), F32),
        scratch_shapes=[pltpu.VMEM((S, TQ), F32)],
        compiler_params=_params("parallel", "arbitrary"),
        name="dsa_attention",
    )(r, kidx, wg, r, kb, vbT)


def _compress_kernel(c_ref, pos_ref, w1_ref, w2_ref, o_ref):
    half = CMP_STRIDE * HEAD_DIM
    x = c_ref[0]
    top = _mm(x + pos_ref[0:1, :], w1_ref[0:half, :])
    bot = _mm(x + pos_ref[1:2, :], w1_ref[half:, :])
    n = x.shape[0]
    hid = jax.nn.gelu(top + pltpu.roll(bot, n - 1, 0), approximate=True)
    out = _mm(hid, w2_ref[...])
    rowi = lax.broadcasted_iota(jnp.int32, out.shape, 0)
    o_ref[0] = jnp.where(rowi < n - 1, out, 0.0).astype(o_ref.dtype)


def _compress(tok, pos_emb, w1, w2):
    B, S, _ = tok.shape
    assert CMP_LEN == 2 * CMP_STRIDE
    nch = S // CMP_STRIDE
    chunks = tok.reshape(B, nch, CMP_STRIDE * HEAD_DIM)
    pos = pos_emb.reshape(2, CMP_STRIDE * HEAD_DIM)
    return pl.pallas_call(
        _compress_kernel,
        grid=(B,),
        in_specs=[pl.BlockSpec((1, nch, CMP_STRIDE * HEAD_DIM), lambda b: (b, 0, 0)),
                  _const_spec(pos.shape), _const_spec(w1.shape), _const_spec(w2.shape)],
        out_specs=pl.BlockSpec((1, nch, HEAD_DIM), lambda b: (b, 0, 0)),
        out_shape=jax.ShapeDtypeStruct((B, nch, HEAD_DIM), MXU_DTYPE),
        compiler_params=_params("parallel"),
        name="nsa_compress",
    )(chunks, pos, w1.astype(MXU_DTYPE), w2.astype(MXU_DTYPE))


def _nsa_kernel(q_ref, wg_ref, kc_ref, vcT_ref, ovT_ref, ks_ref, vsT_ref, kw_ref, vwT_ref,
                o_ref, selb_ref, *, n_cmp, n_sel, topn):
    i = pl.program_id(1)
    qpos = i * TQ + lax.broadcasted_iota(jnp.int32, (1, TQ), 1)
    rowi = lax.broadcasted_iota(jnp.int32, (TQ, TQ), 0)
    q = q_ref[0]
    heads = [q[:, h * HEAD_DIM:(h + 1) * HEAD_DIM] for h in range(N_HEADS)]
    gates = wg_ref[0]

    def chunk_slice(j):
        return pl.ds(pl.multiple_of(j * TQ, TQ), TQ)

    ncp = kc_ref.shape[1]
    cidx = lax.broadcasted_iota(jnp.int32, (ncp, TQ), 0)
    cmask = (cidx * CMP_STRIDE + (CMP_LEN - 1) <= qpos) & (cidx < n_cmp)
    kc, vcT, ovT = kc_ref[0], vcT_ref[0], ovT_ref[...]
    o_cmp = []
    imp = jnp.zeros((n_sel, TQ), F32)
    for h in range(N_HEADS):
        s = jnp.where(cmask, _mm_nt(kc, heads[h]), -jnp.inf)
        m = jnp.max(s, axis=0, keepdims=True)
        m = jnp.where(m > -jnp.inf, m, 0.0)
        e = jnp.where(cmask, jnp.exp(s - m), 0.0)
        p = e / jnp.maximum(jnp.sum(e, axis=0, keepdims=True), 1e-30)
        o_cmp.append(_mm(vcT, p))
        p_hi = p.astype(MXU_DTYPE)
        imp = imp + _mm(ovT, p_hi) + _mm(ovT, p - p_hi.astype(F32))

    bidx = lax.broadcasted_iota(jnp.int32, (n_sel, TQ), 0)
    cur = qpos // SEL_LEN
    forced = (bidx == 0) | (bidx == cur) | (bidx == cur - 1)
    work = jnp.where(bidx * SEL_LEN <= qpos, jnp.where(forced, FORCED_SCORE, imp), -jnp.inf)
    sel = jnp.zeros((n_sel, TQ), F32)
    for _ in range(topn):
        m = jnp.max(work, axis=0, keepdims=True)
        first = jnp.min(jnp.where((work == m) & (m > -jnp.inf), bidx, n_sel), axis=0, keepdims=True)
        pick = bidx == first
        sel = jnp.where(pick, 1.0, sel)
        work = jnp.where(pick, -jnp.inf, work)
    selb_ref[...] = jnp.where(sel > 0.0, 0.0, NEG)

    per_chunk = TQ // SEL_LEN

    def sel_chunk(j, carry):
        rows = [jnp.broadcast_to(selb_ref[pl.ds(j * per_chunk + bb, 1), :], (SEL_LEN, TQ))
                for bb in range(per_chunk)]
        bias = jnp.where(j * TQ + rowi <= qpos, jnp.concatenate(rows, axis=0), NEG)
        k = ks_ref[0, chunk_slice(j), :]
        vT = vsT_ref[0, j]
        return tuple(_softmax_step(_mm_nt(k, heads[h]) + bias, vT, *carry[h]) for h in range(N_HEADS))

    st_sel = lax.fori_loop(0, i + 1, sel_chunk, _softmax_init())

    def win_chunk(j, carry):
        diff = qpos - (j * TQ + rowi)
        bias = jnp.where((diff >= 0) & (diff < WINDOW), 0.0, NEG)
        k = kw_ref[0, chunk_slice(j), :]
        vT = vwT_ref[0, j]
        return tuple(_softmax_step(_mm_nt(k, heads[h]) + bias, vT, *carry[h]) for h in range(N_HEADS))

    st_win = lax.fori_loop(jnp.maximum(i - WINDOW // TQ, 0), i + 1, win_chunk, _softmax_init())

    outs = []
    for h in range(N_HEADS):
        g = lambda br: gates[IDX_HEADS + 3 * h + br:IDX_HEADS + 3 * h + br + 1, :]
        outs.append(g(0) * o_cmp[h] + g(1) * (st_sel[h][2] / st_sel[h][1])
                    + g(2) * (st_win[h][2] / st_win[h][1]))
    o_ref[0] = jnp.concatenate(outs, axis=0).T


def _nsa_attention(r, wg, kc, vcT, ksel, vselT, kwin, vwinT, B, S):
    n_cmp = (S - CMP_LEN) // CMP_STRIDE + 1
    n_sel = S // SEL_LEN
    topn = min(SEL_TOPN, n_sel)
    ncp = kc.shape[1]
    cs = np.arange(ncp) * CMP_STRIDE
    ss = np.arange(n_sel) * SEL_LEN
    overlapT = ((cs[None, :] < ss[:, None] + SEL_LEN) & (cs[None, :] + CMP_LEN > ss[:, None])
                & (np.arange(ncp)[None, :] < n_cmp)).astype(np.float32)
    k_spec = pl.BlockSpec((1, S, HEAD_DIM), lambda b, i: (b, 0, 0))
    vT_spec = pl.BlockSpec((1, S // TQ, HEAD_DIM, TQ), lambda b, i: (b, 0, 0, 0))
    return pl.pallas_call(
        functools.partial(_nsa_kernel, n_cmp=n_cmp, n_sel=n_sel, topn=topn),
        grid=(B, S // TQ),
        in_specs=[pl.BlockSpec((1, TQ, BRANCH_WIDTH), lambda b, i: (b, i, 2)),
                  pl.BlockSpec((1, 16, TQ), lambda b, i: (b, 0, i)),
                  pl.BlockSpec((1, ncp, HEAD_DIM), lambda b, i: (b, 0, 0)),
                  pl.BlockSpec((1, HEAD_DIM, ncp), lambda b, i: (b, 0, 0)),
                  _const_spec((n_sel, ncp)), k_spec, vT_spec, k_spec, vT_spec],
        out_specs=pl.BlockSpec((1, TQ, BRANCH_WIDTH), lambda b, i: (b, i, 0)),
        out_shape=jax.ShapeDtypeStruct((B, S, BRANCH_WIDTH), F32),
        scratch_shapes=[pltpu.VMEM((n_sel, TQ), F32)],
        compiler_params=_params("parallel", "arbitrary"),
        name="nsa_attention",
    )(r, wg, kc, vcT, jnp.asarray(overlapT, MXU_DTYPE), ksel, vselT, kwin, vwinT)


def _merge_kernel(h_ref, g_ref, oa_ref, ob_ref, oc_ref, wg_ref, wup_ref, wout_ref, o_ref):
    x = h_ref[...]
    xn = _rmsnorm(x, g_ref[...]).astype(MXU_DTYPE)
    merged = jnp.zeros((TM, D_MODEL), F32)
    for j, br_ref in enumerate((oa_ref, ob_ref, oc_ref)):
        gate = jax.nn.sigmoid(_mm(xn, wg_ref[:, j * D_MODEL:(j + 1) * D_MODEL]))
        merged = merged + gate * _mm(br_ref[...], wup_ref[j])
    o_ref[...] = x + _mm(merged, wout_ref[...])


def _merge(h, g, o_a, o_b, o_c, w_gm, w_up, w_out):
    T = h.shape[0]
    row = lambda w: pl.BlockSpec((TM, w), lambda i: (i, 0))
    return pl.pallas_call(
        _merge_kernel,
        grid=(T // TM,),
        in_specs=[row(D_MODEL), _const_spec((1, D_MODEL)), row(BRANCH_WIDTH), row(BRANCH_WIDTH),
                  row(BRANCH_WIDTH), _const_spec(w_gm.shape), _const_spec(w_up.shape),
                  _const_spec(w_out.shape)],
        out_specs=row(D_MODEL),
        out_shape=jax.ShapeDtypeStruct((T, D_MODEL), F32),
        compiler_params=_params("parallel"),
        name="merge",
    )(h, g.reshape(1, D_MODEL), o_a, o_b, o_c, w_gm.astype(MXU_DTYPE), w_up.astype(MXU_DTYPE),
      w_out.astype(MXU_DTYPE))


def _norm_matmul_kernel(x_ref, g_ref, w_ref, o_ref):
    o_ref[...] = _mm(_rmsnorm(x_ref[...], g_ref[...]), w_ref[...]).astype(o_ref.dtype)


def _norm_matmul(x, g, w, tm):
    T, n = x.shape[0], w.shape[1]
    return pl.pallas_call(
        _norm_matmul_kernel,
        grid=(T // tm,),
        in_specs=[pl.BlockSpec((tm, D_MODEL), lambda i: (i, 0)), _const_spec((1, D_MODEL)),
                  _const_spec(w.shape)],
        out_specs=pl.BlockSpec((tm, n), lambda i: (i, 0)),
        out_shape=jax.ShapeDtypeStruct((T, n), MXU_DTYPE),
        compiler_params=_params("parallel"),
        name="norm_matmul",
    )(x, g.reshape(1, D_MODEL), w.astype(MXU_DTYPE))


def _mem_attn_kernel(h_ref, g_ref, wq_ref, kv_ref, wo_ref, o_ref):
    x = h_ref[...]
    q = (_mm(_rmsnorm(x, g_ref[...]), wq_ref[...]) * Q_SCALE).astype(MXU_DTYPE)
    kv = kv_ref[0]
    out = x
    for h in range(N_HEADS):
        lo, hi = h * HEAD_DIM, (h + 1) * HEAD_DIM
        s = _mm_nt(q[:, lo:hi], kv[:, lo:hi])
        e = jnp.exp(s - jnp.max(s, axis=-1, keepdims=True))
        p = e / jnp.sum(e, axis=-1, keepdims=True)
        o_h = _mm(p, kv[:, BRANCH_WIDTH + lo:BRANCH_WIDTH + hi])
        out = out + _mm(o_h, wo_ref[lo:hi, :])
    o_ref[...] = out


def _memory_attention(h, kv, g_q, w_q, w_o, S):
    T = h.shape[0]
    M = kv.shape[1]
    row = pl.BlockSpec((TM, D_MODEL), lambda i: (i, 0))
    nb = S // TM
    return pl.pallas_call(
        _mem_attn_kernel,
        grid=(T // TM,),
        in_specs=[row, _const_spec((1, D_MODEL)), _const_spec(w_q.shape),
                  pl.BlockSpec((1, M, 2 * BRANCH_WIDTH), lambda i: (i // nb, 0, 0)),
                  _const_spec(w_o.shape)],
        out_specs=row,
        out_shape=jax.ShapeDtypeStruct((T, D_MODEL), F32),
        compiler_params=_params("parallel"),
        name="memory_attention",
    )(h, g_q.reshape(1, D_MODEL), w_q.astype(MXU_DTYPE), kv, w_o.astype(MXU_DTYPE))


FF_CHUNK = 256


def _swiglu_kernel(h_ref, g_ref, win_ref, wout_ref, gf_ref, o_ref, *, final_norm):
    x = h_ref[...]
    xn = _rmsnorm(x, g_ref[...]).astype(MXU_DTYPE)
    out = x
    for c in range(D_FF // FF_CHUNK):
        lo, hi = c * FF_CHUNK, (c + 1) * FF_CHUNK
        a = _mm(xn, win_ref[:, lo:hi])
        b = _mm(xn, win_ref[:, D_FF + lo:D_FF + hi])
        out = out + _mm(a * jax.nn.sigmoid(a) * b, wout_ref[lo:hi, :])
    if final_norm:
        out = _rmsnorm(out, gf_ref[...])
    o_ref[...] = out


def _swiglu(h, g, w_in, w_out, g_final, final_norm):
    T = h.shape[0]
    row = pl.BlockSpec((TM, D_MODEL), lambda i: (i, 0))
    return pl.pallas_call(
        functools.partial(_swiglu_kernel, final_norm=final_norm),
        grid=(T // TM,),
        in_specs=[row, _const_spec((1, D_MODEL)), _const_spec(w_in.shape), _const_spec(w_out.shape),
                  _const_spec((1, D_MODEL))],
        out_specs=row,
        out_shape=jax.ShapeDtypeStruct((T, D_MODEL), F32),
        compiler_params=_params("parallel"),
        name="swiglu",
    )(h, g.reshape(1, D_MODEL), w_in.astype(MXU_DTYPE), w_out.astype(MXU_DTYPE),
      g_final.reshape(1, D_MODEL))


def _rope_tables(S):
    inv = ROPE_THETA ** (-jnp.arange(0, HEAD_DIM, 2, dtype=F32) / HEAD_DIM)
    ang = jnp.arange(S, dtype=F32)[:, None] * inv[None, :]
    cos, sin = jnp.cos(ang), jnp.sin(ang)
    return (jnp.concatenate([cos, cos, cos, cos], axis=1),
            jnp.concatenate([-sin, sin, -sin, sin], axis=1))


def _key_major(v, B, S):
    return v.reshape(B, S // TQ, TQ, HEAD_DIM).transpose(0, 1, 3, 2)


def _hybrid_mixer(h, B, S, g, w_in, cmp_k, cmp_v, w_up, w_out, cos_t, sin_t):
    T = B * S
    a, r, v, s, c = _in_projection(h, g, w_in, cos_t, sin_t, S)

    heads = lambda t: t.reshape(B, S, N_HEADS, HEAD_DIM).transpose(0, 2, 1, 3)
    o_a = _stick_breaking(heads(a[:, 0:256]), heads(a[:, 256:512]), heads(a[:, 512:768]))

    r3 = r.reshape(B, S, 1024)
    wg = s[:, :16].reshape(B, S, 16).transpose(0, 2, 1)
    tok = lambda lo: r[:, lo:lo + HEAD_DIM].reshape(B, S, HEAD_DIM)
    o_b = _dsa_attention(r3, wg, tok(832), tok(768), _key_major(v[:, 0:64], B, S), B, S)

    kc = _compress(c[:, 0:64].reshape(B, S, HEAD_DIM), *cmp_k)
    vc = _compress(c[:, 64:128].reshape(B, S, HEAD_DIM), *cmp_v)
    o_c = _nsa_attention(r3, wg, kc, vc.transpose(0, 2, 1), tok(896), _key_major(v[:, 64:128], B, S),
                         tok(960), _key_major(v[:, 128:192], B, S), B, S)

    w_gm = w_in[:, _OFF['g_merge'][0]:]
    return _merge(h, g, o_a.reshape(T, 256), o_b.reshape(T, 256), o_c.reshape(T, 256), w_gm, w_up, w_out)


def kernel(x, mem, norm_mix, w_in, cmp_pos_k, cmp_w1_k, cmp_w2_k, cmp_pos_v, cmp_w1_v, cmp_w2_v,
           w_up, w_out, norm_mem_q, norm_mem_kv, w_mem_q, w_mem_kv, w_mem_o,
           norm_ffn, w_ffn_in, w_ffn_out, norm_final):
    B, S, D = x.shape
    M = mem.shape[1]
    depth = w_in.shape[0]
    assert D == D_MODEL and S % TM == 0 and S % TQ == 0 and WINDOW % TQ == 0
    cos_t, sin_t = _rope_tables(S)
    h = x.reshape(B * S, D)
    mem2 = mem.reshape(B * M, D)
    for l in range(depth):
        h = _hybrid_mixer(h, B, S, norm_mix[l], w_in[l],
                          (cmp_pos_k[l], cmp_w1_k[l], cmp_w2_k[l]),
                          (cmp_pos_v[l], cmp_w1_v[l], cmp_w2_v[l]), w_up[l], w_out[l], cos_t, sin_t)
        kv = _norm_matmul(mem2, norm_mem_kv[l], w_mem_kv[l], min(TM, B * M))
        h = _memory_attention(h, kv.reshape(B, M, 2 * BRANCH_WIDTH), norm_mem_q[l], w_mem_q[l],
                              w_mem_o[l], S)
        h = _swiglu(h, norm_ffn[l], w_ffn_in[l], w_ffn_out[l], norm_final, l == depth - 1)
    return h.reshape(B, S, D)
```

```python
import functools

import jax
import jax.numpy as jnp
import numpy as np
from jax import lax
from jax.experimental import pallas as pl
from jax.experimental.pallas import tpu as pltpu

F32 = jnp.float32
MXU_DTYPE = jnp.bfloat16

D_MODEL = 1024
HEAD_DIM = 64
N_HEADS = 4
BRANCH_WIDTH = N_HEADS * HEAD_DIM
IDX_HEADS = 4
DSA_TOPK = 256
CMP_LEN = 32
CMP_STRIDE = 16
CMP_HIDDEN = 128
SEL_LEN = 64
SEL_TOPN = 16
WINDOW = 512
D_FF = 2816
ROPE_THETA = 10000.0
RMS_EPS = 1e-6
FORCED_SCORE = 1e4
Q_SCALE = HEAD_DIM ** -0.5
W_IDX_SCALE = (IDX_HEADS * HEAD_DIM) ** -0.5
NEG = -1e30

LANES = 128
SUBLANES = 8
VMEM_LIMIT = 56 * 1024 * 1024
TM = 512
TQ = 256

_OFF = {}
_o = 0
for _n, _w in (('q_a', 256), ('k_a', 256), ('v_a', 256), ('q_b', 256), ('k_b', 64), ('v_b', 64),
               ('q_idx', 256), ('k_idx', 64), ('w_idx', 4), ('q_c', 256), ('k_cmp', 64),
               ('v_cmp', 64), ('k_sel', 64), ('v_sel', 64), ('k_win', 64), ('v_win', 64),
               ('g_c', 12), ('g_merge', 3 * D_MODEL)):
    _OFF[_n] = (_o, _o + _w)
    _o += _w
D_IN = _o


def _mm(a, b):
    return jnp.dot(a.astype(MXU_DTYPE), b.astype(MXU_DTYPE), preferred_element_type=F32)


def _mm_nt(a, b):
    return lax.dot_general(a.astype(MXU_DTYPE), b.astype(MXU_DTYPE),
                           (((1,), (1,)), ((), ())), preferred_element_type=F32)


def _params(*sem):
    return pltpu.CompilerParams(dimension_semantics=sem, vmem_limit_bytes=VMEM_LIMIT)


def _const_spec(shape):
    nd = len(shape)
    return pl.BlockSpec(shape, lambda *_: (0,) * nd, pipeline_mode=pl.Buffered(1))


def _rmsnorm(x, g):
    return x * lax.rsqrt(jnp.mean(x * x, axis=-1, keepdims=True) + RMS_EPS) * g


def _rope_lanes(x, cos, sin):
    lane = lax.broadcasted_iota(jnp.int32, (1, LANES), 1)
    first_half = (lane % HEAD_DIM) < (HEAD_DIM // 2)
    out = []
    for g in range(x.shape[1] // LANES):
        xg = x[:, g * LANES:(g + 1) * LANES]
        rot = jnp.where(first_half, pltpu.roll(xg, LANES - HEAD_DIM // 2, 1),
                        pltpu.roll(xg, HEAD_DIM // 2, 1))
        out.append(xg * cos + rot * sin)
    return out


def _inproj_kernel(h_ref, g_ref, cos_ref, sin_ref, wa_ref, wr_ref, wv_ref, ws_ref, wc_ref,
                   a_ref, r_ref, v_ref, s_ref, c_ref):
    xn = _rmsnorm(h_ref[...], g_ref[...]).astype(MXU_DTYPE)
    cos, sin = cos_ref[...], sin_ref[...]
    lane = lax.broadcasted_iota(jnp.int32, (1, LANES), 1)

    ya = _mm(xn, wa_ref[...])
    a_ref[:, 0:BRANCH_WIDTH] = (ya[:, 0:BRANCH_WIDTH] * Q_SCALE).astype(a_ref.dtype)
    a_ref[:, BRANCH_WIDTH:] = ya[:, BRANCH_WIDTH:].astype(a_ref.dtype)

    yr = _rope_lanes(_mm(xn, wr_ref[...]), cos, sin)
    for g, og in enumerate(yr):
        if g in (0, 1, 4, 5):
            og = og * Q_SCALE
        r_ref[:, g * LANES:(g + 1) * LANES] = og.astype(r_ref.dtype)

    v_ref[...] = _mm(xn, wv_ref[...]).astype(v_ref.dtype)

    ys = _mm(xn, ws_ref[...])
    s_ref[...] = jnp.where(lane < IDX_HEADS, ys * W_IDX_SCALE, jax.nn.sigmoid(ys))

    yc = _mm(xn, wc_ref[...])
    c_ref[...] = jnp.where(lane < HEAD_DIM, _rope_lanes(yc, cos, sin)[0], yc)


def _in_projection(h, g, w_in, cos_t, sin_t, S):
    T = h.shape[0]
    cols = lambda *names: jnp.concatenate([w_in[:, _OFF[n][0]:_OFF[n][1]] for n in names], axis=1)
    pad = lambda w, n: jnp.pad(w, ((0, 0), (0, n - w.shape[1])))
    wa = cols('q_a', 'k_a', 'v_a').astype(MXU_DTYPE)
    wr = cols('q_b', 'q_idx', 'q_c', 'k_b', 'k_idx', 'k_sel', 'k_win').astype(MXU_DTYPE)
    wv = pad(cols('v_b', 'v_sel', 'v_win'), 256).astype(MXU_DTYPE)
    ws = pad(cols('w_idx', 'g_c'), LANES).astype(MXU_DTYPE)
    wc = cols('k_cmp', 'v_cmp').astype(MXU_DTYPE)
    nb = S // TM
    row = lambda w: pl.BlockSpec((TM, w), lambda i: (i, 0))
    pos = pl.BlockSpec((TM, LANES), lambda i: (i % nb, 0))
    return pl.pallas_call(
        _inproj_kernel,
        grid=(T // TM,),
        in_specs=[row(D_MODEL), _const_spec((1, D_MODEL)), pos, pos,
                  _const_spec(wa.shape), _const_spec(wr.shape), _const_spec(wv.shape),
                  _const_spec(ws.shape), _const_spec(wc.shape)],
        out_specs=[row(768), row(1024), row(256), row(LANES), row(LANES)],
        out_shape=[jax.ShapeDtypeStruct((T, 768), MXU_DTYPE),
                   jax.ShapeDtypeStruct((T, 1024), MXU_DTYPE),
                   jax.ShapeDtypeStruct((T, 256), MXU_DTYPE),
                   jax.ShapeDtypeStruct((T, LANES), F32),
                   jax.ShapeDtypeStruct((T, LANES), F32)],
        compiler_params=_params("parallel"),
        name="in_projection",
    )(h, g.reshape(1, D_MODEL), cos_t, sin_t, wa, wr, wv, ws, wc)


def _chunk_slice(j):
    return pl.ds(pl.multiple_of(j * TQ, TQ), TQ)


def _sb_kernel(q_ref, k_ref, vT_ref, tri_ref, o_ref, acc_ref):
    i = pl.program_id(1)
    row = lax.broadcasted_iota(jnp.int32, (TQ, TQ), 0)
    col = lax.broadcasted_iota(jnp.int32, (TQ, TQ), 1)
    strict = row < col
    tri = tri_ref[...]
    qs = [q_ref[0, h] for h in range(N_HEADS)]
    acc_ref[...] = jnp.zeros(acc_ref.shape, F32)

    def chunk(j, cs, diag):
        heads = range(N_HEADS)
        zs = [_mm_nt(k_ref[0, h, _chunk_slice(j), :], qs[h]) for h in heads]
        cums = []
        for z in zs:
            sp = jnp.maximum(z, 0.0) + jnp.log(1.0 + jnp.exp(-jnp.abs(z)))
            if diag:
                sp = jnp.where(strict, sp, 0.0)
            hi = sp.astype(MXU_DTYPE)
            lo = (sp - hi.astype(F32)).astype(MXU_DTYPE)
            cums.append(_mm(tri, hi) + _mm(tri, lo))
        ws = []
        for h in heads:
            w = jnp.exp(zs[h] - cums[h] - cs[h])
            ws.append(jnp.where(strict, w, 0.0) if diag else w)
        for h in heads:
            acc_ref[h] += _mm(vT_ref[0, h, j], ws[h])
        return tuple(cs[h] + cums[h][0:1, :] for h in heads)

    cs = chunk(i, tuple(jnp.zeros((1, TQ), F32) for _ in range(N_HEADS)), True)
    lax.fori_loop(0, i, lambda t, cs: chunk(i - 1 - t, cs, False), cs)
    o_ref[0] = acc_ref[...].reshape(BRANCH_WIDTH, TQ).T


def _stick_breaking(q, k, vT):
    B, H, S, _ = q.shape
    tri = jnp.asarray(np.triu(np.ones((TQ, TQ), np.float32)), MXU_DTYPE)
    return pl.pallas_call(
        _sb_kernel,
        grid=(B, S // TQ),
        in_specs=[pl.BlockSpec((1, H, TQ, HEAD_DIM), lambda b, i: (b, 0, i, 0)),
                  pl.BlockSpec((1, H, S, HEAD_DIM), lambda b, i: (b, 0, 0, 0)),
                  pl.BlockSpec((1, H, S // TQ, HEAD_DIM, TQ), lambda b, i: (b, 0, 0, 0, 0)),
                  _const_spec((TQ, TQ))],
        out_specs=pl.BlockSpec((1, TQ, BRANCH_WIDTH), lambda b, i: (b, i, 0)),
        out_shape=jax.ShapeDtypeStruct((B, S, BRANCH_WIDTH), F32),
        scratch_shapes=[pltpu.VMEM((H, HEAD_DIM, TQ), F32)],
        compiler_params=_params("parallel", "arbitrary"),
        name="stick_breaking",
    )(q, k, vT, tri)


def _colsum8(x):
    return x.reshape(x.shape[0] // SUBLANES, SUBLANES, x.shape[1]).sum(axis=0)


V_ROWS = HEAD_DIM + SUBLANES
COUNT_ROWS = 64


def _softmax_step(k, qs, bias, vT, ms, acc_ref):
    ss = [_mm_nt(k, q) + bias for q in qs]
    m_new = [jnp.maximum(m, jnp.max(s, axis=0, keepdims=True)) for m, s in zip(ms, ss)]
    ps = [jnp.exp(s - m) for s, m in zip(ss, m_new)]
    for h in range(len(qs)):
        acc_ref[h] = jnp.exp(ms[h] - m_new[h]) * acc_ref[h] + _mm(vT, ps[h])
    return tuple(m_new)


def _softmax_init(acc_ref):
    acc_ref[...] = jnp.zeros(acc_ref.shape, F32)
    return tuple(jnp.full((1, TQ), NEG, F32) for _ in range(N_HEADS))


def _softmax_result(acc_ref, h):
    return acc_ref[h, 0:HEAD_DIM, :] / acc_ref[h, HEAD_DIM:HEAD_DIM + 1, :]


def _dsa_kernel(qi_ref, ki_ref, wg_ref, q_ref, k_ref, vT_ref, o_ref, sc_ref, acc_ref, *, topk, idx_bits):
    i = pl.program_id(1)
    nchunks = i + 1
    npairs = (i + 2) // 2
    qpos = i * TQ + lax.broadcasted_iota(jnp.int32, (1, TQ), 1)
    rowi = lax.broadcasted_iota(jnp.int32, (TQ, TQ), 0)
    rowi2 = lax.broadcasted_iota(jnp.int32, (2 * TQ, TQ), 0)
    rowc = lax.broadcasted_iota(jnp.int32, (COUNT_ROWS, TQ), 0)
    int_min = jnp.int32(-2 ** 31)

    def pair_slice(j):
        return pl.ds(pl.multiple_of(j * 2 * TQ, 2 * TQ), 2 * TQ)

    qis = [qi_ref[0, hh] for hh in range(IDX_HEADS)]
    wT = wg_ref[0]

    def idx_chunk(j, _):
        kx = ki_ref[0, _chunk_slice(j), :]
        rs = [_mm_nt(kx, qh) for qh in qis]
        acc = jnp.zeros((TQ, TQ), F32)
        for hh in range(IDX_HEADS):
            acc = acc + wT[hh:hh + 1, :] * jnp.maximum(rs[hh], 0.0)
        sc_ref[_chunk_slice(j), :] = jnp.where(j * TQ + rowi <= qpos, acc, -jnp.inf)
        return 0

    lax.fori_loop(0, nchunks, idx_chunk, 0)

    @pl.when(nchunks % 2 == 1)
    def _():
        sc_ref[_chunk_slice(nchunks), :] = jnp.full((TQ, TQ), -jnp.inf, F32)

    def count(pred):
        def body(j, acc):
            for r in range(2 * TQ // COUNT_ROWS):
                base = pl.multiple_of(j * 2 * TQ + r * COUNT_ROWS, COUNT_ROWS)
                blk = sc_ref[pl.ds(base, COUNT_ROWS), :]
                acc = acc + _colsum8(jnp.where(pred(blk, base + rowc), 1.0, 0.0))
            return acc
        return jnp.sum(lax.fori_loop(0, npairs, body, jnp.zeros((SUBLANES, TQ), F32)),
                       axis=0, keepdims=True)

    take_all = qpos < topk

    def search():
        def key_to_float(u):
            s = u ^ int_min
            return lax.bitcast_convert_type(jnp.where(s < 0, s ^ jnp.int32(0x7FFFFFFF), s), F32)

        def bit_body(t, carry):
            res, n_res = carry
            cand = res | jnp.left_shift(jnp.int32(1), 31 - t)
            thr = key_to_float(cand)
            n = count(lambda blk, kpos: blk >= thr)
            ok = n >= topk
            return jnp.where(ok, cand, res), jnp.where(ok, n, n_res)

        res, n_ge = lax.fori_loop(0, 32, bit_body,
                                  (jnp.zeros((1, TQ), jnp.int32), jnp.zeros((1, TQ), F32)))
        thr = key_to_float(res)

        @pl.when(jnp.max(jnp.where(take_all, 0.0, n_ge)) > topk)
        def _():
            need = topk - count(lambda blk, kpos: blk > thr)

            def bit_body(t, lim):
                cand = lim | jnp.left_shift(jnp.int32(1), idx_bits - 1 - t)
                n = count(lambda blk, kpos: (blk == thr) & (kpos < cand))
                return jnp.where(n <= need, cand, lim)

            limit = lax.fori_loop(0, idx_bits, bit_body, jnp.zeros((1, TQ), jnp.int32))

            def drop(j, _):
                blk = sc_ref[pair_slice(j), :]
                sc_ref[pair_slice(j), :] = jnp.where((blk == thr) & (j * 2 * TQ + rowi2 >= limit),
                                                     -jnp.inf, blk)
                return 0

            lax.fori_loop(0, npairs, drop, 0)

        return thr

    lowest = float(np.finfo(np.float32).min)
    thr = lax.cond(i * TQ + TQ > topk, search, lambda: jnp.full((1, TQ), lowest, F32))
    thr = jnp.where(take_all, lowest, thr)

    qs = [q_ref[0, h] for h in range(N_HEADS)]

    def att_chunk(j, ms):
        bias = jnp.where(sc_ref[_chunk_slice(j), :] >= thr, 0.0, NEG)
        return _softmax_step(k_ref[0, _chunk_slice(j), :], qs, bias, vT_ref[0, j], ms, acc_ref)

    lax.fori_loop(0, nchunks, att_chunk, _softmax_init(acc_ref))
    o_ref[0] = jnp.concatenate([_softmax_result(acc_ref, h) for h in range(N_HEADS)], axis=0).T


def _dsa_attention(qidx, kidx, wg, qb, kb, vbT, B, S):
    topk = min(DSA_TOPK, S // 4)
    assert topk <= TQ
    nc = S // TQ
    q_spec = pl.BlockSpec((1, N_HEADS, TQ, HEAD_DIM), lambda b, i: (b, 0, i, 0))
    k_spec = pl.BlockSpec((1, S, HEAD_DIM), lambda b, i: (b, 0, 0))
    return pl.pallas_call(
        functools.partial(_dsa_kernel, topk=topk, idx_bits=S.bit_length()),
        grid=(B, nc),
        in_specs=[q_spec, k_spec, pl.BlockSpec((1, 16, TQ), lambda b, i: (b, 0, i)), q_spec, k_spec,
                  pl.BlockSpec((1, nc, V_ROWS, TQ), lambda b, i: (b, 0, 0, 0))],
        out_specs=pl.BlockSpec((1, TQ, BRANCH_WIDTH), lambda b, i: (b, i, 0)),
        out_shape=jax.ShapeDtypeStruct((B, S, BRANCH_WIDTH), F32),
        scratch_shapes=[pltpu.VMEM(((nc + nc % 2) * TQ, TQ), F32),
                        pltpu.VMEM((N_HEADS, V_ROWS, TQ), F32)],
        compiler_params=_params("parallel", "arbitrary"),
        name="dsa_attention",
    )(qidx, kidx, wg, qb, kb, vbT)


def _compress_kernel(c_ref, pos_ref, w1_ref, w2_ref, o_ref):
    half = CMP_STRIDE * HEAD_DIM
    x = c_ref[0]
    top = _mm(x + pos_ref[0:1, :], w1_ref[0:half, :])
    bot = _mm(x + pos_ref[1:2, :], w1_ref[half:, :])
    n = x.shape[0]
    hid = jax.nn.gelu(top + pltpu.roll(bot, n - 1, 0), approximate=True)
    out = _mm(hid, w2_ref[...])
    rowi = lax.broadcasted_iota(jnp.int32, out.shape, 0)
    o_ref[0] = jnp.where(rowi < n - 1, out, 0.0).astype(o_ref.dtype)


def _compress(tok, pos_emb, w1, w2):
    B, S, _ = tok.shape
    assert CMP_LEN == 2 * CMP_STRIDE
    nch = S // CMP_STRIDE
    chunks = tok.reshape(B, nch, CMP_STRIDE * HEAD_DIM)
    pos = pos_emb.reshape(2, CMP_STRIDE * HEAD_DIM)
    return pl.pallas_call(
        _compress_kernel,
        grid=(B,),
        in_specs=[pl.BlockSpec((1, nch, CMP_STRIDE * HEAD_DIM), lambda b: (b, 0, 0)),
                  _const_spec(pos.shape), _const_spec(w1.shape), _const_spec(w2.shape)],
        out_specs=pl.BlockSpec((1, nch, HEAD_DIM), lambda b: (b, 0, 0)),
        out_shape=jax.ShapeDtypeStruct((B, nch, HEAD_DIM), MXU_DTYPE),
        compiler_params=_params("parallel"),
        name="nsa_compress",
    )(chunks, pos, w1.astype(MXU_DTYPE), w2.astype(MXU_DTYPE))


def _nsa_kernel(q_ref, wg_ref, kc_ref, vcT_ref, ovT_ref, ks_ref, vsT_ref, kw_ref, vwT_ref,
                o_ref, selb_ref, acc_s_ref, acc_w_ref, *, n_cmp, n_sel, topn):
    i = pl.program_id(1)
    qpos = i * TQ + lax.broadcasted_iota(jnp.int32, (1, TQ), 1)
    rowi = lax.broadcasted_iota(jnp.int32, (TQ, TQ), 0)
    heads = [q_ref[0, h] for h in range(N_HEADS)]
    gates = wg_ref[0]

    ncp = kc_ref.shape[1]
    cidx = lax.broadcasted_iota(jnp.int32, (ncp, TQ), 0)
    cmask = (cidx * CMP_STRIDE + (CMP_LEN - 1) <= qpos) & (cidx < n_cmp)
    kc, vcT, ovT = kc_ref[0], vcT_ref[0], ovT_ref[...]
    ss = [jnp.where(cmask, _mm_nt(kc, q), -jnp.inf) for q in heads]
    ps = []
    for s in ss:
        m = jnp.max(s, axis=0, keepdims=True)
        m = jnp.where(m > -jnp.inf, m, 0.0)
        e = jnp.where(cmask, jnp.exp(s - m), 0.0)
        ps.append(e / jnp.maximum(jnp.sum(e, axis=0, keepdims=True), 1e-30))
    o_cmp = [_mm(vcT, p) for p in ps]
    imp = jnp.zeros((n_sel, TQ), F32)
    for p in ps:
        p_hi = p.astype(MXU_DTYPE)
        imp = imp + _mm(ovT, p_hi) + _mm(ovT, p - p_hi.astype(F32))

    bidx = lax.broadcasted_iota(jnp.int32, (n_sel, TQ), 0)
    cur = qpos // SEL_LEN
    forced = (bidx == 0) | (bidx == cur) | (bidx == cur - 1)
    work = jnp.where(bidx * SEL_LEN <= qpos, jnp.where(forced, FORCED_SCORE, imp), -jnp.inf)
    sel = jnp.zeros((n_sel, TQ), F32)
    for _ in range(topn):
        m = jnp.max(work, axis=0, keepdims=True)
        first = jnp.min(jnp.where((work == m) & (m > -jnp.inf), bidx, n_sel), axis=0, keepdims=True)
        pick = bidx == first
        sel = jnp.where(pick, 1.0, sel)
        work = jnp.where(pick, -jnp.inf, work)
    selb_ref[...] = jnp.where(sel > 0.0, 0.0, NEG)

    per_chunk = TQ // SEL_LEN

    def sel_chunk(j, ms, diag):
        rows = [jnp.broadcast_to(selb_ref[pl.ds(j * per_chunk + bb, 1), :], (SEL_LEN, TQ))
                for bb in range(per_chunk)]
        bias = jnp.concatenate(rows, axis=0)
        if diag:
            bias = jnp.where(rowi <= qpos - i * TQ, bias, NEG)
        return _softmax_step(ks_ref[0, _chunk_slice(j), :], heads, bias, vsT_ref[0, j], ms, acc_s_ref)

    ms = lax.fori_loop(0, i, lambda j, ms: sel_chunk(j, ms, False), _softmax_init(acc_s_ref))
    sel_chunk(i, ms, True)

    def win_chunk(j, ms):
        diff = qpos - (j * TQ + rowi)
        bias = jnp.where((diff >= 0) & (diff < WINDOW), 0.0, NEG)
        return _softmax_step(kw_ref[0, _chunk_slice(j), :], heads, bias, vwT_ref[0, j], ms, acc_w_ref)

    lax.fori_loop(jnp.maximum(i - WINDOW // TQ, 0), i + 1, win_chunk, _softmax_init(acc_w_ref))

    outs = []
    for h in range(N_HEADS):
        g = lambda br: gates[IDX_HEADS + 3 * h + br:IDX_HEADS + 3 * h + br + 1, :]
        outs.append(g(0) * o_cmp[h] + g(1) * _softmax_result(acc_s_ref, h)
                    + g(2) * _softmax_result(acc_w_ref, h))
    o_ref[0] = jnp.concatenate(outs, axis=0).T


def _nsa_attention(qc, wg, kc, vcT, ksel, vselT, kwin, vwinT, B, S):
    n_cmp = (S - CMP_LEN) // CMP_STRIDE + 1
    n_sel = S // SEL_LEN
    topn = min(SEL_TOPN, n_sel)
    ncp = kc.shape[1]
    cs = np.arange(ncp) * CMP_STRIDE
    ss = np.arange(n_sel) * SEL_LEN
    overlapT = ((cs[None, :] < ss[:, None] + SEL_LEN) & (cs[None, :] + CMP_LEN > ss[:, None])
                & (np.arange(ncp)[None, :] < n_cmp)).astype(np.float32)
    k_spec = pl.BlockSpec((1, S, HEAD_DIM), lambda b, i: (b, 0, 0))
    vT_spec = pl.BlockSpec((1, S // TQ, V_ROWS, TQ), lambda b, i: (b, 0, 0, 0))
    return pl.pallas_call(
        functools.partial(_nsa_kernel, n_cmp=n_cmp, n_sel=n_sel, topn=topn),
        grid=(B, S // TQ),
        in_specs=[pl.BlockSpec((1, N_HEADS, TQ, HEAD_DIM), lambda b, i: (b, 0, i, 0)),
                  pl.BlockSpec((1, 16, TQ), lambda b, i: (b, 0, i)),
                  pl.BlockSpec((1, ncp, HEAD_DIM), lambda b, i: (b, 0, 0)),
                  pl.BlockSpec((1, HEAD_DIM, ncp), lambda b, i: (b, 0, 0)),
                  _const_spec((n_sel, ncp)), k_spec, vT_spec, k_spec, vT_spec],
        out_specs=pl.BlockSpec((1, TQ, BRANCH_WIDTH), lambda b, i: (b, i, 0)),
        out_shape=jax.ShapeDtypeStruct((B, S, BRANCH_WIDTH), F32),
        scratch_shapes=[pltpu.VMEM((n_sel, TQ), F32), pltpu.VMEM((N_HEADS, V_ROWS, TQ), F32),
                        pltpu.VMEM((N_HEADS, V_ROWS, TQ), F32)],
        compiler_params=_params("parallel", "arbitrary"),
        name="nsa_attention",
    )(qc, wg, kc, vcT, jnp.asarray(overlapT, MXU_DTYPE), ksel, vselT, kwin, vwinT)


def _merge_kernel(h_ref, g_ref, oa_ref, ob_ref, oc_ref, wg_ref, wup_ref, wout_ref, o_ref):
    x = h_ref[...]
    xn = _rmsnorm(x, g_ref[...]).astype(MXU_DTYPE)
    merged = jnp.zeros((TM, D_MODEL), F32)
    for j, br_ref in enumerate((oa_ref, ob_ref, oc_ref)):
        gate = jax.nn.sigmoid(_mm(xn, wg_ref[:, j * D_MODEL:(j + 1) * D_MODEL]))
        merged = merged + gate * _mm(br_ref[...], wup_ref[j])
    o_ref[...] = x + _mm(merged, wout_ref[...])


def _merge(h, g, o_a, o_b, o_c, w_gm, w_up, w_out):
    T = h.shape[0]
    row = lambda w: pl.BlockSpec((TM, w), lambda i: (i, 0))
    return pl.pallas_call(
        _merge_kernel,
        grid=(T // TM,),
        in_specs=[row(D_MODEL), _const_spec((1, D_MODEL)), row(BRANCH_WIDTH), row(BRANCH_WIDTH),
                  row(BRANCH_WIDTH), _const_spec(w_gm.shape), _const_spec(w_up.shape),
                  _const_spec(w_out.shape)],
        out_specs=row(D_MODEL),
        out_shape=jax.ShapeDtypeStruct((T, D_MODEL), F32),
        compiler_params=_params("parallel"),
        name="merge",
    )(h, g.reshape(1, D_MODEL), o_a, o_b, o_c, w_gm.astype(MXU_DTYPE), w_up.astype(MXU_DTYPE),
      w_out.astype(MXU_DTYPE))


def _norm_matmul_kernel(x_ref, g_ref, w_ref, o_ref):
    o_ref[...] = _mm(_rmsnorm(x_ref[...], g_ref[...]), w_ref[...]).astype(o_ref.dtype)


def _norm_matmul(x, g, w, tm):
    T, n = x.shape[0], w.shape[1]
    return pl.pallas_call(
        _norm_matmul_kernel,
        grid=(T // tm,),
        in_specs=[pl.BlockSpec((tm, D_MODEL), lambda i: (i, 0)), _const_spec((1, D_MODEL)),
                  _const_spec(w.shape)],
        out_specs=pl.BlockSpec((tm, n), lambda i: (i, 0)),
        out_shape=jax.ShapeDtypeStruct((T, n), MXU_DTYPE),
        compiler_params=_params("parallel"),
        name="norm_matmul",
    )(x, g.reshape(1, D_MODEL), w.astype(MXU_DTYPE))


def _mem_attn_kernel(h_ref, g_ref, wq_ref, kv_ref, wo_ref, o_ref):
    x = h_ref[...]
    q = (_mm(_rmsnorm(x, g_ref[...]), wq_ref[...]) * Q_SCALE).astype(MXU_DTYPE)
    kv = kv_ref[0]
    out = x
    for h in range(N_HEADS):
        lo, hi = h * HEAD_DIM, (h + 1) * HEAD_DIM
        s = _mm_nt(q[:, lo:hi], kv[:, lo:hi])
        e = jnp.exp(s - jnp.max(s, axis=-1, keepdims=True))
        p = e / jnp.sum(e, axis=-1, keepdims=True)
        o_h = _mm(p, kv[:, BRANCH_WIDTH + lo:BRANCH_WIDTH + hi])
        out = out + _mm(o_h, wo_ref[lo:hi, :])
    o_ref[...] = out


def _memory_attention(h, kv, g_q, w_q, w_o, S):
    T = h.shape[0]
    M = kv.shape[1]
    row = pl.BlockSpec((TM, D_MODEL), lambda i: (i, 0))
    nb = S // TM
    return pl.pallas_call(
        _mem_attn_kernel,
        grid=(T // TM,),
        in_specs=[row, _const_spec((1, D_MODEL)), _const_spec(w_q.shape),
                  pl.BlockSpec((1, M, 2 * BRANCH_WIDTH), lambda i: (i // nb, 0, 0)),
                  _const_spec(w_o.shape)],
        out_specs=row,
        out_shape=jax.ShapeDtypeStruct((T, D_MODEL), F32),
        compiler_params=_params("parallel"),
        name="memory_attention",
    )(h, g_q.reshape(1, D_MODEL), w_q.astype(MXU_DTYPE), kv, w_o.astype(MXU_DTYPE))


FF_CHUNK = 256


def _swiglu_kernel(h_ref, g_ref, win_ref, wout_ref, gf_ref, o_ref, *, final_norm):
    x = h_ref[...]
    xn = _rmsnorm(x, g_ref[...]).astype(MXU_DTYPE)
    out = x
    for c in range(D_FF // FF_CHUNK):
        lo, hi = c * FF_CHUNK, (c + 1) * FF_CHUNK
        a = _mm(xn, win_ref[:, lo:hi])
        b = _mm(xn, win_ref[:, D_FF + lo:D_FF + hi])
        out = out + _mm(a * jax.nn.sigmoid(a) * b, wout_ref[lo:hi, :])
    if final_norm:
        out = _rmsnorm(out, gf_ref[...])
    o_ref[...] = out


def _swiglu(h, g, w_in, w_out, g_final, final_norm):
    T = h.shape[0]
    row = pl.BlockSpec((TM, D_MODEL), lambda i: (i, 0))
    return pl.pallas_call(
        functools.partial(_swiglu_kernel, final_norm=final_norm),
        grid=(T // TM,),
        in_specs=[row, _const_spec((1, D_MODEL)), _const_spec(w_in.shape), _const_spec(w_out.shape),
                  _const_spec((1, D_MODEL))],
        out_specs=row,
        out_shape=jax.ShapeDtypeStruct((T, D_MODEL), F32),
        compiler_params=_params("parallel"),
        name="swiglu",
    )(h, g.reshape(1, D_MODEL), w_in.astype(MXU_DTYPE), w_out.astype(MXU_DTYPE),
      g_final.reshape(1, D_MODEL))


def _rope_tables(S):
    inv = ROPE_THETA ** (-jnp.arange(0, HEAD_DIM, 2, dtype=F32) / HEAD_DIM)
    ang = jnp.arange(S, dtype=F32)[:, None] * inv[None, :]
    cos, sin = jnp.cos(ang), jnp.sin(ang)
    return (jnp.concatenate([cos, cos, cos, cos], axis=1),
            jnp.concatenate([-sin, sin, -sin, sin], axis=1))


def _key_major(v, B, S):
    vT = v.reshape(B, S // TQ, TQ, HEAD_DIM).transpose(0, 1, 3, 2)
    extra = jnp.zeros((B, S // TQ, SUBLANES, TQ), v.dtype).at[:, :, 0, :].set(1)
    return jnp.concatenate([vT, extra], axis=2)


def _hybrid_mixer(h, B, S, g, w_in, cmp_k, cmp_v, w_up, w_out, cos_t, sin_t):
    T = B * S
    a, r, v, s, c = _in_projection(h, g, w_in, cos_t, sin_t, S)

    heads = lambda t: t.reshape(B, S, N_HEADS, HEAD_DIM).transpose(0, 2, 1, 3)
    vaT = a[:, 512:768].reshape(B, S // TQ, TQ, N_HEADS, HEAD_DIM).transpose(0, 3, 1, 4, 2)
    o_a = _stick_breaking(heads(a[:, 0:256]), heads(a[:, 256:512]), vaT)

    wg = s[:, :16].reshape(B, S, 16).transpose(0, 2, 1)
    tok = lambda lo: r[:, lo:lo + HEAD_DIM].reshape(B, S, HEAD_DIM)
    o_b = _dsa_attention(heads(r[:, 256:512]), tok(832), wg, heads(r[:, 0:256]), tok(768),
                         _key_major(v[:, 0:64], B, S), B, S)

    kc = _compress(c[:, 0:64].reshape(B, S, HEAD_DIM), *cmp_k)
    vc = _compress(c[:, 64:128].reshape(B, S, HEAD_DIM), *cmp_v)
    o_c = _nsa_attention(heads(r[:, 512:768]), wg, kc, vc.transpose(0, 2, 1), tok(896),
                         _key_major(v[:, 64:128], B, S), tok(960), _key_major(v[:, 128:192], B, S), B, S)

    w_gm = w_in[:, _OFF['g_merge'][0]:]
    return _merge(h, g, o_a.reshape(T, 256), o_b.reshape(T, 256), o_c.reshape(T, 256), w_gm, w_up, w_out)


def kernel(x, mem, norm_mix, w_in, cmp_pos_k, cmp_w1_k, cmp_w2_k, cmp_pos_v, cmp_w1_v, cmp_w2_v,
           w_up, w_out, norm_mem_q, norm_mem_kv, w_mem_q, w_mem_kv, w_mem_o,
           norm_ffn, w_ffn_in, w_ffn_out, norm_final):
    B, S, D = x.shape
    M = mem.shape[1]
    depth = w_in.shape[0]
    assert D == D_MODEL and S % TM == 0 and S % TQ == 0 and WINDOW % TQ == 0
    cos_t, sin_t = _rope_tables(S)
    h = x.reshape(B * S, D)
    mem2 = mem.reshape(B * M, D)
    for l in range(depth):
        h = _hybrid_mixer(h, B, S, norm_mix[l], w_in[l],
                          (cmp_pos_k[l], cmp_w1_k[l], cmp_w2_k[l]),
                          (cmp_pos_v[l], cmp_w1_v[l], cmp_w2_v[l]), w_up[l], w_out[l], cos_t, sin_t)
        kv = _norm_matmul(mem2, norm_mem_kv[l], w_mem_kv[l], min(TM, B * M))
        h = _memory_attention(h, kv.reshape(B, M, 2 * BRANCH_WIDTH), norm_mem_q[l], w_mem_q[l],
                              w_mem_o[l], S)
        h = _swiglu(h, norm_ffn[l], w_ffn_in[l], w_ffn_out[l], norm_final, l == depth - 1)
    return h.reshape(B, S, D)
```

```python
import functools

import jax
import jax.numpy as jnp
import numpy as np
from jax import lax
from jax.experimental import pallas as pl
from jax.experimental.pallas import tpu as pltpu

F32 = jnp.float32
MXU_DTYPE = jnp.bfloat16

D_MODEL = 1024
HEAD_DIM = 64
N_HEADS = 4
BRANCH_WIDTH = N_HEADS * HEAD_DIM
IDX_HEADS = 4
DSA_TOPK = 256
CMP_LEN = 32
CMP_STRIDE = 16
CMP_HIDDEN = 128
SEL_LEN = 64
SEL_TOPN = 16
WINDOW = 512
D_FF = 2816
ROPE_THETA = 10000.0
RMS_EPS = 1e-6
FORCED_SCORE = 1e4
Q_SCALE = HEAD_DIM ** -0.5
W_IDX_SCALE = (IDX_HEADS * HEAD_DIM) ** -0.5
NEG = -1e30

LANES = 128
SUBLANES = 8
VMEM_LIMIT = 56 * 1024 * 1024
TM = 512
TQ = 256

_OFF = {}
_o = 0
for _n, _w in (('q_a', 256), ('k_a', 256), ('v_a', 256), ('q_b', 256), ('k_b', 64), ('v_b', 64),
               ('q_idx', 256), ('k_idx', 64), ('w_idx', 4), ('q_c', 256), ('k_cmp', 64),
               ('v_cmp', 64), ('k_sel', 64), ('v_sel', 64), ('k_win', 64), ('v_win', 64),
               ('g_c', 12), ('g_merge', 3 * D_MODEL)):
    _OFF[_n] = (_o, _o + _w)
    _o += _w
D_IN = _o


def _mm(a, b):
    return jnp.dot(a.astype(MXU_DTYPE), b.astype(MXU_DTYPE), preferred_element_type=F32)


def _mm_nt(a, b):
    return lax.dot_general(a.astype(MXU_DTYPE), b.astype(MXU_DTYPE),
                           (((1,), (1,)), ((), ())), preferred_element_type=F32)


def _params(*sem):
    return pltpu.CompilerParams(dimension_semantics=sem, vmem_limit_bytes=VMEM_LIMIT)


def _const_spec(shape):
    nd = len(shape)
    return pl.BlockSpec(shape, lambda *_: (0,) * nd, pipeline_mode=pl.Buffered(1))


def _rmsnorm(x, g):
    return x * lax.rsqrt(jnp.mean(x * x, axis=-1, keepdims=True) + RMS_EPS) * g


def _rope_lanes(x, cos, sin):
    lane = lax.broadcasted_iota(jnp.int32, (1, LANES), 1)
    first_half = (lane % HEAD_DIM) < (HEAD_DIM // 2)
    out = []
    for g in range(x.shape[1] // LANES):
        xg = x[:, g * LANES:(g + 1) * LANES]
        rot = jnp.where(first_half, pltpu.roll(xg, LANES - HEAD_DIM // 2, 1),
                        pltpu.roll(xg, HEAD_DIM // 2, 1))
        out.append(xg * cos + rot * sin)
    return out


WG_ROW = BRANCH_WIDTH + 3 * HEAD_DIM
V_ROWS = HEAD_DIM + 2 * SUBLANES


def _inproj_kernel(h_ref, g_ref, cos_ref, sin_ref, wa_ref, wr_ref, wc_ref, wt_ref,
                   qa_ref, ka_ref, vaT_ref, qb_ref, qi_ref, qc_ref, kb_ref, ki_ref, ks_ref, kw_ref,
                   vbT_ref, vsT_ref, vwT_ref, wg_ref, ck_ref, cv_ref):
    xn = _rmsnorm(h_ref[...], g_ref[...]).astype(MXU_DTYPE)
    cos, sin = cos_ref[...], sin_ref[...]
    head = lambda y, h: y[:, h * HEAD_DIM:(h + 1) * HEAD_DIM]

    ya = _mm(xn, wa_ref[...])
    for h in range(N_HEADS):
        qa_ref[0, h] = (head(ya, h) * Q_SCALE).astype(qa_ref.dtype)
        ka_ref[0, h] = head(ya, N_HEADS + h).astype(ka_ref.dtype)

    yr = _rope_lanes(_mm(xn, wr_ref[...]), cos, sin)
    for g, (q_ref, scale) in enumerate(((qb_ref, Q_SCALE), (qi_ref, 1.0), (qc_ref, Q_SCALE))):
        for half in range(2):
            og = yr[2 * g + half] * scale
            q_ref[0, 2 * half] = head(og, 0).astype(q_ref.dtype)
            q_ref[0, 2 * half + 1] = head(og, 1).astype(q_ref.dtype)
    kb_ref[0] = head(yr[6], 0).astype(kb_ref.dtype)
    ki_ref[0] = head(yr[6], 1).astype(ki_ref.dtype)
    ks_ref[0] = head(yr[7], 0).astype(ks_ref.dtype)
    kw_ref[0] = head(yr[7], 1).astype(kw_ref.dtype)

    yc = _mm(xn, wc_ref[...])
    ck_ref[0] = head(_rope_lanes(yc, cos, sin)[0], 0)
    cv_ref[0] = head(yc, 1)

    yT = _mm_nt(wt_ref[...], xn)
    ones_row = (lax.broadcasted_iota(jnp.int32, (V_ROWS - HEAD_DIM, TQ), 0) == 0).astype(vbT_ref.dtype)
    for c in range(TM // TQ):
        cols = slice(c * TQ, (c + 1) * TQ)
        for h in range(N_HEADS):
            vaT_ref[0, h, c] = yT[h * HEAD_DIM:(h + 1) * HEAD_DIM, cols].astype(vaT_ref.dtype)
        for n, v_ref in enumerate((vbT_ref, vsT_ref, vwT_ref)):
            lo = BRANCH_WIDTH + n * HEAD_DIM
            v_ref[0, c, 0:HEAD_DIM, :] = yT[lo:lo + HEAD_DIM, cols].astype(v_ref.dtype)
            v_ref[0, c, HEAD_DIM:, :] = ones_row
    ys = yT[WG_ROW:WG_ROW + 16, :]
    row = lax.broadcasted_iota(jnp.int32, ys.shape, 0)
    wg_ref[0] = jnp.where(row < IDX_HEADS, ys * W_IDX_SCALE, jax.nn.sigmoid(ys))


def _in_projection(h, g, w_in, cos_t, sin_t, B, S):
    cols = lambda *names: jnp.concatenate([w_in[:, _OFF[n][0]:_OFF[n][1]] for n in names], axis=1)
    wa = cols('q_a', 'k_a').astype(MXU_DTYPE)
    wr = cols('q_b', 'q_idx', 'q_c', 'k_b', 'k_idx', 'k_sel', 'k_win').astype(MXU_DTYPE)
    wc = cols('k_cmp', 'v_cmp').astype(MXU_DTYPE)
    wt = cols('v_a', 'v_b', 'v_sel', 'v_win', 'w_idx', 'g_c')
    wt = jnp.pad(wt, ((0, 0), (0, 512 - wt.shape[1]))).T.astype(MXU_DTYPE)
    nb, nc, cpb = S // TM, S // TQ, TM // TQ
    bf = lambda *shape: jax.ShapeDtypeStruct(shape, MXU_DTYPE)
    f32 = lambda *shape: jax.ShapeDtypeStruct(shape, F32)
    heads_spec = pl.BlockSpec((1, N_HEADS, TM, HEAD_DIM), lambda i: (i // nb, 0, i % nb, 0))
    tok_spec = pl.BlockSpec((1, TM, HEAD_DIM), lambda i: (i // nb, i % nb, 0))
    vT_spec = pl.BlockSpec((1, cpb, V_ROWS, TQ), lambda i: (i // nb, i % nb, 0, 0))
    pos = pl.BlockSpec((TM, LANES), lambda i: (i % nb, 0))
    q4, k3, v4 = bf(B, N_HEADS, S, HEAD_DIM), bf(B, S, HEAD_DIM), bf(B, nc, V_ROWS, TQ)
    return pl.pallas_call(
        _inproj_kernel,
        grid=(B * nb,),
        in_specs=[pl.BlockSpec((TM, D_MODEL), lambda i: (i, 0)), _const_spec((1, D_MODEL)), pos, pos,
                  _const_spec(wa.shape), _const_spec(wr.shape), _const_spec(wc.shape),
                  _const_spec(wt.shape)],
        out_specs=[heads_spec, heads_spec,
                   pl.BlockSpec((1, N_HEADS, cpb, HEAD_DIM, TQ), lambda i: (i // nb, 0, i % nb, 0, 0)),
                   heads_spec, heads_spec, heads_spec, tok_spec, tok_spec, tok_spec, tok_spec,
                   vT_spec, vT_spec, vT_spec,
                   pl.BlockSpec((1, 16, TM), lambda i: (i // nb, 0, i % nb)), tok_spec, tok_spec],
        out_shape=[q4, q4, bf(B, N_HEADS, nc, HEAD_DIM, TQ), q4, q4, q4, k3, k3, k3, k3, v4, v4, v4,
                   f32(B, 16, S), f32(B, S, HEAD_DIM), f32(B, S, HEAD_DIM)],
        compiler_params=_params("parallel"),
        name="in_projection",
    )(h, g.reshape(1, D_MODEL), cos_t, sin_t, wa, wr, wc, wt)


def _chunk_slice(j):
    return pl.ds(pl.multiple_of(j * TQ, TQ), TQ)


def _sb_kernel(q_ref, k_ref, vT_ref, tri_ref, o_ref, acc_ref):
    i = pl.program_id(1)
    row = lax.broadcasted_iota(jnp.int32, (TQ, TQ), 0)
    col = lax.broadcasted_iota(jnp.int32, (TQ, TQ), 1)
    strict = row < col
    tri = tri_ref[...]
    qs = [q_ref[0, h] for h in range(N_HEADS)]
    acc_ref[...] = jnp.zeros(acc_ref.shape, F32)

    def chunk(j, cs, diag):
        heads = range(N_HEADS)
        zs = [_mm_nt(k_ref[0, h, _chunk_slice(j), :], qs[h]) for h in heads]
        cums = []
        for z in zs:
            sp = jnp.maximum(z, 0.0) + jnp.log(1.0 + jnp.exp(-jnp.abs(z)))
            if diag:
                sp = jnp.where(strict, sp, 0.0)
            hi = sp.astype(MXU_DTYPE)
            lo = (sp - hi.astype(F32)).astype(MXU_DTYPE)
            cums.append(_mm(tri, hi) + _mm(tri, lo))
        ws = []
        for h in heads:
            w = jnp.exp(zs[h] - cums[h] - cs[h])
            ws.append(jnp.where(strict, w, 0.0) if diag else w)
        for h in heads:
            acc_ref[h] += _mm(vT_ref[0, h, j], ws[h])
        return tuple(cs[h] + cums[h][0:1, :] for h in heads)

    cs = chunk(i, tuple(jnp.zeros((1, TQ), F32) for _ in range(N_HEADS)), True)
    lax.fori_loop(0, i, lambda t, cs: chunk(i - 1 - t, cs, False), cs)
    o_ref[0] = acc_ref[...].reshape(BRANCH_WIDTH, TQ).T


def _stick_breaking(q, k, vT):
    B, H, S, _ = q.shape
    tri = jnp.asarray(np.triu(np.ones((TQ, TQ), np.float32)), MXU_DTYPE)
    return pl.pallas_call(
        _sb_kernel,
        grid=(B, S // TQ),
        in_specs=[pl.BlockSpec((1, H, TQ, HEAD_DIM), lambda b, i: (b, 0, i, 0)),
                  pl.BlockSpec((1, H, S, HEAD_DIM), lambda b, i: (b, 0, 0, 0)),
                  pl.BlockSpec((1, H, S // TQ, HEAD_DIM, TQ), lambda b, i: (b, 0, 0, 0, 0)),
                  _const_spec((TQ, TQ))],
        out_specs=pl.BlockSpec((1, TQ, BRANCH_WIDTH), lambda b, i: (b, i, 0)),
        out_shape=jax.ShapeDtypeStruct((B, S, BRANCH_WIDTH), F32),
        scratch_shapes=[pltpu.VMEM((H, HEAD_DIM, TQ), F32)],
        compiler_params=_params("parallel", "arbitrary"),
        name="stick_breaking",
    )(q, k, vT, tri)


def _colsum8(x):
    return x.reshape(x.shape[0] // SUBLANES, SUBLANES, x.shape[1]).sum(axis=0)


COUNT_ROWS = 64


def _softmax_step(k, qs, bias, vT, ms, acc_ref):
    ss = [_mm_nt(k, q) + bias for q in qs]
    m_new = [jnp.maximum(m, jnp.max(s, axis=0, keepdims=True)) for m, s in zip(ms, ss)]
    ps = [jnp.exp(s - m) for s, m in zip(ss, m_new)]
    for h in range(len(qs)):
        acc_ref[h] = jnp.exp(ms[h] - m_new[h]) * acc_ref[h] + _mm(vT, ps[h])
    return tuple(m_new)


def _softmax_init(acc_ref):
    acc_ref[...] = jnp.zeros(acc_ref.shape, F32)
    return tuple(jnp.full((1, TQ), NEG, F32) for _ in range(N_HEADS))


def _softmax_result(acc_ref, h):
    return acc_ref[h, 0:HEAD_DIM, :] / acc_ref[h, HEAD_DIM:HEAD_DIM + 1, :]


def _dsa_kernel(qi_ref, ki_ref, wg_ref, q_ref, k_ref, vT_ref, tril_ref, o_ref, sc_ref, acc_ref, *, topk):
    i = pl.program_id(1)
    nchunks = i + 1
    npairs = (i + 2) // 2
    qpos = i * TQ + lax.broadcasted_iota(jnp.int32, (1, TQ), 1)
    rowi = lax.broadcasted_iota(jnp.int32, (TQ, TQ), 0)
    int_min = jnp.int32(-2 ** 31)

    qis = [qi_ref[0, hh] for hh in range(IDX_HEADS)]
    wT = wg_ref[0]

    def idx_chunk(j, _):
        kx = ki_ref[0, _chunk_slice(j), :]
        rs = [_mm_nt(kx, qh) for qh in qis]
        acc = jnp.zeros((TQ, TQ), F32)
        for hh in range(IDX_HEADS):
            acc = acc + wT[hh:hh + 1, :] * jnp.maximum(rs[hh], 0.0)
        sc_ref[_chunk_slice(j), :] = jnp.where(j * TQ + rowi <= qpos, acc, -jnp.inf)
        return 0

    lax.fori_loop(0, nchunks, idx_chunk, 0)

    @pl.when(nchunks % 2 == 1)
    def _():
        sc_ref[_chunk_slice(nchunks), :] = jnp.full((TQ, TQ), -jnp.inf, F32)

    def count(pred):
        def body(j, acc):
            for r in range(2 * TQ // COUNT_ROWS):
                base = pl.multiple_of(j * 2 * TQ + r * COUNT_ROWS, COUNT_ROWS)
                acc = acc + _colsum8(jnp.where(pred(sc_ref[pl.ds(base, COUNT_ROWS), :]), 1.0, 0.0))
            return acc
        return jnp.sum(lax.fori_loop(0, npairs, body, jnp.zeros((SUBLANES, TQ), F32)),
                       axis=0, keepdims=True)

    take_all = qpos < topk

    def search():
        def key_to_float(u):
            s = u ^ int_min
            return lax.bitcast_convert_type(jnp.where(s < 0, s ^ jnp.int32(0x7FFFFFFF), s), F32)

        def bit_body(t, carry):
            res, n_ge, n_gt = carry
            cand = res | jnp.left_shift(jnp.int32(1), 31 - t)
            thr = key_to_float(cand)
            n = count(lambda blk: blk >= thr)
            ok = n >= topk
            return jnp.where(ok, cand, res), jnp.where(ok, n, n_ge), jnp.where(ok, n_gt, n)

        zero = jnp.zeros((1, TQ), F32)
        res, n_ge, n_gt = lax.fori_loop(0, 32, bit_body, (jnp.zeros((1, TQ), jnp.int32), zero, zero))
        thr = key_to_float(res)

        @pl.when(jnp.max(jnp.where(take_all, 0.0, n_ge)) > topk)
        def _():
            need = jnp.where(take_all, float(2 ** 30), topk - n_gt)
            tril = tril_ref[...]

            def drop(j, before):
                blk = sc_ref[_chunk_slice(j), :]
                tied = blk == thr
                rank = _mm(tril, jnp.where(tied, 1.0, 0.0)) + before
                sc_ref[_chunk_slice(j), :] = jnp.where(tied & (rank > need), -jnp.inf, blk)
                return rank[TQ - 1:TQ, :]

            lax.fori_loop(0, nchunks, drop, zero)

        return thr

    lowest = float(np.finfo(np.float32).min)
    thr = lax.cond(i * TQ + TQ > topk, search, lambda: jnp.full((1, TQ), lowest, F32))
    thr = jnp.where(take_all, lowest, thr)

    qs = [q_ref[0, h] for h in range(N_HEADS)]

    def att_chunk(j, ms):
        bias = jnp.where(sc_ref[_chunk_slice(j), :] >= thr, 0.0, NEG)
        return _softmax_step(k_ref[0, _chunk_slice(j), :], qs, bias, vT_ref[0, j], ms, acc_ref)

    lax.fori_loop(0, nchunks, att_chunk, _softmax_init(acc_ref))
    o_ref[0] = jnp.concatenate([_softmax_result(acc_ref, h) for h in range(N_HEADS)], axis=0).T


def _dsa_attention(qidx, kidx, wg, qb, kb, vbT, B, S):
    topk = min(DSA_TOPK, S // 4)
    assert topk <= TQ
    nc = S // TQ
    q_spec = pl.BlockSpec((1, N_HEADS, TQ, HEAD_DIM), lambda b, i: (b, 0, i, 0))
    k_spec = pl.BlockSpec((1, S, HEAD_DIM), lambda b, i: (b, 0, 0))
    tril = jnp.asarray(np.tril(np.ones((TQ, TQ), np.float32)), MXU_DTYPE)
    return pl.pallas_call(
        functools.partial(_dsa_kernel, topk=topk),
        grid=(B, nc),
        in_specs=[q_spec, k_spec, pl.BlockSpec((1, 16, TQ), lambda b, i: (b, 0, i)), q_spec, k_spec,
                  pl.BlockSpec((1, nc, V_ROWS, TQ), lambda b, i: (b, 0, 0, 0)), _const_spec((TQ, TQ))],
        out_specs=pl.BlockSpec((1, TQ, BRANCH_WIDTH), lambda b, i: (b, i, 0)),
        out_shape=jax.ShapeDtypeStruct((B, S, BRANCH_WIDTH), F32),
        scratch_shapes=[pltpu.VMEM(((nc + nc % 2) * TQ, TQ), F32),
                        pltpu.VMEM((N_HEADS, V_ROWS, TQ), F32)],
        compiler_params=_params("parallel", "arbitrary"),
        name="dsa_attention",
    )(qidx, kidx, wg, qb, kb, vbT, tril)


def _compress_kernel(c_ref, pos_ref, w1_ref, w2_ref, o_ref, *, transpose_out):
    n = c_ref.shape[1] // CMP_STRIDE
    top = jnp.zeros((n, CMP_HIDDEN), F32)
    bot = jnp.zeros((n, CMP_HIDDEN), F32)
    for p in range(CMP_STRIDE):
        x = c_ref[0, pl.ds(p, n, stride=CMP_STRIDE), :]
        q = p + CMP_STRIDE
        top = top + _mm(x + pos_ref[p:p + 1, :], w1_ref[p * HEAD_DIM:(p + 1) * HEAD_DIM, :])
        bot = bot + _mm(x + pos_ref[q:q + 1, :], w1_ref[q * HEAD_DIM:(q + 1) * HEAD_DIM, :])
    hid = jax.nn.gelu(top + pltpu.roll(bot, n - 1, 0), approximate=True)
    out = _mm(hid, w2_ref[...])
    rowi = lax.broadcasted_iota(jnp.int32, out.shape, 0)
    out = jnp.where(rowi < n - 1, out, 0.0)
    if transpose_out:
        out = jnp.concatenate([out, jnp.zeros_like(out)], axis=1).T[0:HEAD_DIM, :]
    o_ref[0] = out.astype(o_ref.dtype)


def _compress(tok, pos_emb, w1, w2, transpose_out):
    B, S, _ = tok.shape
    assert CMP_LEN == 2 * CMP_STRIDE
    nch = S // CMP_STRIDE
    out_block = (1, HEAD_DIM, nch) if transpose_out else (1, nch, HEAD_DIM)
    return pl.pallas_call(
        functools.partial(_compress_kernel, transpose_out=transpose_out),
        grid=(B,),
        in_specs=[pl.BlockSpec((1, S, HEAD_DIM), lambda b: (b, 0, 0)),
                  _const_spec(pos_emb.shape), _const_spec(w1.shape), _const_spec(w2.shape)],
        out_specs=pl.BlockSpec(out_block, lambda b: (b, 0, 0)),
        out_shape=jax.ShapeDtypeStruct((B,) + out_block[1:], MXU_DTYPE),
        compiler_params=_params("parallel"),
        name="nsa_compress",
    )(tok, pos_emb, w1.astype(MXU_DTYPE), w2.astype(MXU_DTYPE))


def _nsa_kernel(q_ref, wg_ref, kc_ref, vcT_ref, ovT_ref, ks_ref, vsT_ref, kw_ref, vwT_ref,
                o_ref, selb_ref, acc_s_ref, acc_w_ref, *, n_cmp, n_sel, topn):
    i = pl.program_id(1)
    qpos = i * TQ + lax.broadcasted_iota(jnp.int32, (1, TQ), 1)
    rowi = lax.broadcasted_iota(jnp.int32, (TQ, TQ), 0)
    heads = [q_ref[0, h] for h in range(N_HEADS)]
    gates = wg_ref[0]

    ncp = kc_ref.shape[1]
    cidx = lax.broadcasted_iota(jnp.int32, (ncp, TQ), 0)
    cmask = (cidx * CMP_STRIDE + (CMP_LEN - 1) <= qpos) & (cidx < n_cmp)
    kc, vcT, ovT = kc_ref[0], vcT_ref[0], ovT_ref[...]
    ss = [jnp.where(cmask, _mm_nt(kc, q), -jnp.inf) for q in heads]
    ps = []
    for s in ss:
        m = jnp.max(s, axis=0, keepdims=True)
        m = jnp.where(m > -jnp.inf, m, 0.0)
        e = jnp.where(cmask, jnp.exp(s - m), 0.0)
        ps.append(e / jnp.maximum(jnp.sum(e, axis=0, keepdims=True), 1e-30))
    o_cmp = [_mm(vcT, p) for p in ps]
    imp = jnp.zeros((n_sel, TQ), F32)
    for p in ps:
        p_hi = p.astype(MXU_DTYPE)
        imp = imp + _mm(ovT, p_hi) + _mm(ovT, p - p_hi.astype(F32))

    bidx = lax.broadcasted_iota(jnp.int32, (n_sel, TQ), 0)
    cur = qpos // SEL_LEN
    forced = (bidx == 0) | (bidx == cur) | (bidx == cur - 1)
    work = jnp.where(bidx * SEL_LEN <= qpos, jnp.where(forced, FORCED_SCORE, imp), -jnp.inf)
    sel = jnp.zeros((n_sel, TQ), F32)
    for _ in range(topn):
        m = jnp.max(work, axis=0, keepdims=True)
        first = jnp.min(jnp.where((work == m) & (m > -jnp.inf), bidx, n_sel), axis=0, keepdims=True)
        pick = bidx == first
        sel = jnp.where(pick, 1.0, sel)
        work = jnp.where(pick, -jnp.inf, work)
    selb_ref[...] = jnp.where(sel > 0.0, 0.0, NEG)

    per_chunk = TQ // SEL_LEN

    def sel_chunk(j, ms, diag):
        rows = [jnp.broadcast_to(selb_ref[pl.ds(j * per_chunk + bb, 1), :], (SEL_LEN, TQ))
                for bb in range(per_chunk)]
        bias = jnp.concatenate(rows, axis=0)
        if diag:
            bias = jnp.where(rowi <= qpos - i * TQ, bias, NEG)
        return _softmax_step(ks_ref[0, _chunk_slice(j), :], heads, bias, vsT_ref[0, j], ms, acc_s_ref)

    ms = lax.fori_loop(0, i, lambda j, ms: sel_chunk(j, ms, False), _softmax_init(acc_s_ref))
    sel_chunk(i, ms, True)

    def win_chunk(j, ms):
        diff = qpos - (j * TQ + rowi)
        bias = jnp.where((diff >= 0) & (diff < WINDOW), 0.0, NEG)
        return _softmax_step(kw_ref[0, _chunk_slice(j), :], heads, bias, vwT_ref[0, j], ms, acc_w_ref)

    lax.fori_loop(jnp.maximum(i - WINDOW // TQ, 0), i + 1, win_chunk, _softmax_init(acc_w_ref))

    outs = []
    for h in range(N_HEADS):
        g = lambda br: gates[IDX_HEADS + 3 * h + br:IDX_HEADS + 3 * h + br + 1, :]
        outs.append(g(0) * o_cmp[h] + g(1) * _softmax_result(acc_s_ref, h)
                    + g(2) * _softmax_result(acc_w_ref, h))
    o_ref[0] = jnp.concatenate(outs, axis=0).T


def _nsa_attention(qc, wg, kc, vcT, ksel, vselT, kwin, vwinT, B, S):
    n_cmp = (S - CMP_LEN) // CMP_STRIDE + 1
    n_sel = S // SEL_LEN
    topn = min(SEL_TOPN, n_sel)
    ncp = kc.shape[1]
    cs = np.arange(ncp) * CMP_STRIDE
    ss = np.arange(n_sel) * SEL_LEN
    overlapT = ((cs[None, :] < ss[:, None] + SEL_LEN) & (cs[None, :] + CMP_LEN > ss[:, None])
                & (np.arange(ncp)[None, :] < n_cmp)).astype(np.float32)
    k_spec = pl.BlockSpec((1, S, HEAD_DIM), lambda b, i: (b, 0, 0))
    vT_spec = pl.BlockSpec((1, S // TQ, V_ROWS, TQ), lambda b, i: (b, 0, 0, 0))
    return pl.pallas_call(
        functools.partial(_nsa_kernel, n_cmp=n_cmp, n_sel=n_sel, topn=topn),
        grid=(B, S // TQ),
        in_specs=[pl.BlockSpec((1, N_HEADS, TQ, HEAD_DIM), lambda b, i: (b, 0, i, 0)),
                  pl.BlockSpec((1, 16, TQ), lambda b, i: (b, 0, i)),
                  pl.BlockSpec((1, ncp, HEAD_DIM), lambda b, i: (b, 0, 0)),
                  pl.BlockSpec((1, HEAD_DIM, ncp), lambda b, i: (b, 0, 0)),
                  _const_spec((n_sel, ncp)), k_spec, vT_spec, k_spec, vT_spec],
        out_specs=pl.BlockSpec((1, TQ, BRANCH_WIDTH), lambda b, i: (b, i, 0)),
        out_shape=jax.ShapeDtypeStruct((B, S, BRANCH_WIDTH), F32),
        scratch_shapes=[pltpu.VMEM((n_sel, TQ), F32), pltpu.VMEM((N_HEADS, V_ROWS, TQ), F32),
                        pltpu.VMEM((N_HEADS, V_ROWS, TQ), F32)],
        compiler_params=_params("parallel", "arbitrary"),
        name="nsa_attention",
    )(qc, wg, kc, vcT, jnp.asarray(overlapT, MXU_DTYPE), ksel, vselT, kwin, vwinT)


def _merge_kernel(h_ref, g_ref, oa_ref, ob_ref, oc_ref, wg_ref, wup_ref, wout_ref, o_ref):
    x = h_ref[...]
    xn = _rmsnorm(x, g_ref[...]).astype(MXU_DTYPE)
    merged = jnp.zeros((TM, D_MODEL), F32)
    for j, br_ref in enumerate((oa_ref, ob_ref, oc_ref)):
        gate = jax.nn.sigmoid(_mm(xn, wg_ref[:, j * D_MODEL:(j + 1) * D_MODEL]))
        merged = merged + gate * _mm(br_ref[...], wup_ref[j])
    o_ref[...] = x + _mm(merged, wout_ref[...])


def _merge(h, g, o_a, o_b, o_c, w_gm, w_up, w_out):
    T = h.shape[0]
    row = lambda w: pl.BlockSpec((TM, w), lambda i: (i, 0))
    return pl.pallas_call(
        _merge_kernel,
        grid=(T // TM,),
        in_specs=[row(D_MODEL), _const_spec((1, D_MODEL)), row(BRANCH_WIDTH), row(BRANCH_WIDTH),
                  row(BRANCH_WIDTH), _const_spec(w_gm.shape), _const_spec(w_up.shape),
                  _const_spec(w_out.shape)],
        out_specs=row(D_MODEL),
        out_shape=jax.ShapeDtypeStruct((T, D_MODEL), F32),
        compiler_params=_params("parallel"),
        name="merge",
    )(h, g.reshape(1, D_MODEL), o_a, o_b, o_c, w_gm.astype(MXU_DTYPE), w_up.astype(MXU_DTYPE),
      w_out.astype(MXU_DTYPE))


def _norm_matmul_kernel(x_ref, g_ref, w_ref, o_ref):
    o_ref[...] = _mm(_rmsnorm(x_ref[...], g_ref[...]), w_ref[...]).astype(o_ref.dtype)


def _norm_matmul(x, g, w, tm):
    T, n = x.shape[0], w.shape[1]
    return pl.pallas_call(
        _norm_matmul_kernel,
        grid=(T // tm,),
        in_specs=[pl.BlockSpec((tm, D_MODEL), lambda i: (i, 0)), _const_spec((1, D_MODEL)),
                  _const_spec(w.shape)],
        out_specs=pl.BlockSpec((tm, n), lambda i: (i, 0)),
        out_shape=jax.ShapeDtypeStruct((T, n), MXU_DTYPE),
        compiler_params=_params("parallel"),
        name="norm_matmul",
    )(x, g.reshape(1, D_MODEL), w.astype(MXU_DTYPE))


def _mem_attn_kernel(h_ref, g_ref, wq_ref, kv_ref, wo_ref, o_ref):
    x = h_ref[...]
    q = (_mm(_rmsnorm(x, g_ref[...]), wq_ref[...]) * Q_SCALE).astype(MXU_DTYPE)
    kv = kv_ref[0]
    out = x
    for h in range(N_HEADS):
        lo, hi = h * HEAD_DIM, (h + 1) * HEAD_DIM
        s = _mm_nt(q[:, lo:hi], kv[:, lo:hi])
        e = jnp.exp(s - jnp.max(s, axis=-1, keepdims=True))
        p = e / jnp.sum(e, axis=-1, keepdims=True)
        o_h = _mm(p, kv[:, BRANCH_WIDTH + lo:BRANCH_WIDTH + hi])
        out = out + _mm(o_h, wo_ref[lo:hi, :])
    o_ref[...] = out


def _memory_attention(h, kv, g_q, w_q, w_o, S):
    T = h.shape[0]
    M = kv.shape[1]
    row = pl.BlockSpec((TM, D_MODEL), lambda i: (i, 0))
    nb = S // TM
    return pl.pallas_call(
        _mem_attn_kernel,
        grid=(T // TM,),
        in_specs=[row, _const_spec((1, D_MODEL)), _const_spec(w_q.shape),
                  pl.BlockSpec((1, M, 2 * BRANCH_WIDTH), lambda i: (i // nb, 0, 0)),
                  _const_spec(w_o.shape)],
        out_specs=row,
        out_shape=jax.ShapeDtypeStruct((T, D_MODEL), F32),
        compiler_params=_params("parallel"),
        name="memory_attention",
    )(h, g_q.reshape(1, D_MODEL), w_q.astype(MXU_DTYPE), kv, w_o.astype(MXU_DTYPE))


FF_CHUNK = 256


def _swiglu_kernel(h_ref, g_ref, win_ref, wout_ref, gf_ref, o_ref, *, final_norm):
    x = h_ref[...]
    xn = _rmsnorm(x, g_ref[...]).astype(MXU_DTYPE)
    out = x
    for c in range(D_FF // FF_CHUNK):
        lo, hi = c * FF_CHUNK, (c + 1) * FF_CHUNK
        a = _mm(xn, win_ref[:, lo:hi])
        b = _mm(xn, win_ref[:, D_FF + lo:D_FF + hi])
        out = out + _mm(a * jax.nn.sigmoid(a) * b, wout_ref[lo:hi, :])
    if final_norm:
        out = _rmsnorm(out, gf_ref[...])
    o_ref[...] = out


def _swiglu(h, g, w_in, w_out, g_final, final_norm):
    T = h.shape[0]
    row = pl.BlockSpec((TM, D_MODEL), lambda i: (i, 0))
    return pl.pallas_call(
        functools.partial(_swiglu_kernel, final_norm=final_norm),
        grid=(T // TM,),
        in_specs=[row, _const_spec((1, D_MODEL)), _const_spec(w_in.shape), _const_spec(w_out.shape),
                  _const_spec((1, D_MODEL))],
        out_specs=row,
        out_shape=jax.ShapeDtypeStruct((T, D_MODEL), F32),
        compiler_params=_params("parallel"),
        name="swiglu",
    )(h, g.reshape(1, D_MODEL), w_in.astype(MXU_DTYPE), w_out.astype(MXU_DTYPE),
      g_final.reshape(1, D_MODEL))


def _rope_tables(S):
    inv = ROPE_THETA ** (-jnp.arange(0, HEAD_DIM, 2, dtype=F32) / HEAD_DIM)
    ang = jnp.arange(S, dtype=F32)[:, None] * inv[None, :]
    cos, sin = jnp.cos(ang), jnp.sin(ang)
    return (jnp.concatenate([cos, cos, cos, cos], axis=1),
            jnp.concatenate([-sin, sin, -sin, sin], axis=1))


def _hybrid_mixer(h, B, S, g, w_in, cmp_k, cmp_v, w_up, w_out, cos_t, sin_t):
    T = B * S
    (qa, ka, vaT, qb, qidx, qc, kb, kidx, ksel, kwin, vbT, vselT, vwinT, wg, ck, cv) = _in_projection(
        h, g, w_in, cos_t, sin_t, B, S)
    o_a = _stick_breaking(qa, ka, vaT)
    o_b = _dsa_attention(qidx, kidx, wg, qb, kb, vbT, B, S)
    kc = _compress(ck, *cmp_k, transpose_out=False)
    vcT = _compress(cv, *cmp_v, transpose_out=True)
    o_c = _nsa_attention(qc, wg, kc, vcT, ksel, vselT, kwin, vwinT, B, S)

    w_gm = w_in[:, _OFF['g_merge'][0]:]
    return _merge(h, g, o_a.reshape(T, 256), o_b.reshape(T, 256), o_c.reshape(T, 256), w_gm, w_up, w_out)


def kernel(x, mem, norm_mix, w_in, cmp_pos_k, cmp_w1_k, cmp_w2_k, cmp_pos_v, cmp_w1_v, cmp_w2_v,
           w_up, w_out, norm_mem_q, norm_mem_kv, w_mem_q, w_mem_kv, w_mem_o,
           norm_ffn, w_ffn_in, w_ffn_out, norm_final):
    B, S, D = x.shape
    M = mem.shape[1]
    depth = w_in.shape[0]
    assert D == D_MODEL and S % TM == 0 and S % TQ == 0 and WINDOW % TQ == 0
    cos_t, sin_t = _rope_tables(S)
    h = x.reshape(B * S, D)
    mem2 = mem.reshape(B * M, D)
    for l in range(depth):
        h = _hybrid_mixer(h, B, S, norm_mix[l], w_in[l],
                          (cmp_pos_k[l], cmp_w1_k[l], cmp_w2_k[l]),
                          (cmp_pos_v[l], cmp_w1_v[l], cmp_w2_v[l]), w_up[l], w_out[l], cos_t, sin_t)
        kv = _norm_matmul(mem2, norm_mem_kv[l], w_mem_kv[l], min(TM, B * M))
        h = _memory_attention(h, kv.reshape(B, M, 2 * BRANCH_WIDTH), norm_mem_q[l], w_mem_q[l],
                              w_mem_o[l], S)
        h = _swiglu(h, norm_ffn[l], w_ffn_in[l], w_ffn_out[l], norm_final, l == depth - 1)
    return h.reshape(B, S, D)
```

```python
import functools

import jax
import jax.numpy as jnp
import numpy as np
from jax import lax
from jax.experimental import pallas as pl
from jax.experimental.pallas import tpu as pltpu

F32 = jnp.float32
MXU_DTYPE = jnp.bfloat16

D_MODEL = 1024
HEAD_DIM = 64
N_HEADS = 4
BRANCH_WIDTH = N_HEADS * HEAD_DIM
IDX_HEADS = 4
DSA_TOPK = 256
CMP_LEN = 32
CMP_STRIDE = 16
CMP_HIDDEN = 128
SEL_LEN = 64
SEL_TOPN = 16
WINDOW = 512
D_FF = 2816
ROPE_THETA = 10000.0
RMS_EPS = 1e-6
FORCED_SCORE = 1e4
Q_SCALE = HEAD_DIM ** -0.5
W_IDX_SCALE = (IDX_HEADS * HEAD_DIM) ** -0.5
NEG = -1e30

LANES = 128
SUBLANES = 8
VMEM_LIMIT = 56 * 1024 * 1024
TM = 512
TQ = 256

_OFF = {}
_o = 0
for _n, _w in (('q_a', 256), ('k_a', 256), ('v_a', 256), ('q_b', 256), ('k_b', 64), ('v_b', 64),
               ('q_idx', 256), ('k_idx', 64), ('w_idx', 4), ('q_c', 256), ('k_cmp', 64),
               ('v_cmp', 64), ('k_sel', 64), ('v_sel', 64), ('k_win', 64), ('v_win', 64),
               ('g_c', 12), ('g_merge', 3 * D_MODEL)):
    _OFF[_n] = (_o, _o + _w)
    _o += _w
D_IN = _o


def _mm(a, b):
    return jnp.dot(a.astype(MXU_DTYPE), b.astype(MXU_DTYPE), preferred_element_type=F32)


def _mm_nt(a, b):
    return lax.dot_general(a.astype(MXU_DTYPE), b.astype(MXU_DTYPE),
                           (((1,), (1,)), ((), ())), preferred_element_type=F32)


def _params(*sem):
    return pltpu.CompilerParams(dimension_semantics=sem, vmem_limit_bytes=VMEM_LIMIT)


def _const_spec(shape):
    nd = len(shape)
    return pl.BlockSpec(shape, lambda *_: (0,) * nd, pipeline_mode=pl.Buffered(1))


def _rmsnorm(x, g):
    return x * lax.rsqrt(jnp.mean(x * x, axis=-1, keepdims=True) + RMS_EPS) * g


def _rope_lanes(x, cos, sin):
    lane = lax.broadcasted_iota(jnp.int32, (1, LANES), 1)
    first_half = (lane % HEAD_DIM) < (HEAD_DIM // 2)
    out = []
    for g in range(x.shape[1] // LANES):
        xg = x[:, g * LANES:(g + 1) * LANES]
        rot = jnp.where(first_half, pltpu.roll(xg, LANES - HEAD_DIM // 2, 1),
                        pltpu.roll(xg, HEAD_DIM // 2, 1))
        out.append(xg * cos + rot * sin)
    return out


WG_ROW = BRANCH_WIDTH + 3 * HEAD_DIM
V_ROWS = HEAD_DIM + 2 * SUBLANES


def _inproj_kernel(h_ref, g_ref, cos_ref, sin_ref, wa_ref, wr_ref, wc_ref, wt_ref,
                   qa_ref, ka_ref, vaT_ref, qb_ref, qi_ref, qc_ref, kb_ref, ki_ref, ks_ref, kw_ref,
                   vbT_ref, vsT_ref, vwT_ref, wg_ref, ck_ref, cv_ref):
    xn = _rmsnorm(h_ref[...], g_ref[...]).astype(MXU_DTYPE)
    cos, sin = cos_ref[...], sin_ref[...]
    head = lambda y, h: y[:, h * HEAD_DIM:(h + 1) * HEAD_DIM]

    ya = _mm(xn, wa_ref[...])
    for h in range(N_HEADS):
        qa_ref[0, h] = (head(ya, h) * Q_SCALE).astype(qa_ref.dtype)
        ka_ref[0, h] = head(ya, N_HEADS + h).astype(ka_ref.dtype)

    yr = _rope_lanes(_mm(xn, wr_ref[...]), cos, sin)
    for g, (q_ref, scale) in enumerate(((qb_ref, Q_SCALE), (qi_ref, 1.0), (qc_ref, Q_SCALE))):
        for half in range(2):
            og = yr[2 * g + half] * scale
            q_ref[0, 2 * half] = head(og, 0).astype(q_ref.dtype)
            q_ref[0, 2 * half + 1] = head(og, 1).astype(q_ref.dtype)
    kb_ref[0] = head(yr[6], 0).astype(kb_ref.dtype)
    ki_ref[0] = head(yr[6], 1).astype(ki_ref.dtype)
    ks_ref[0] = head(yr[7], 0).astype(ks_ref.dtype)
    kw_ref[0] = head(yr[7], 1).astype(kw_ref.dtype)

    yc = _mm(xn, wc_ref[...])
    ck_ref[0] = head(_rope_lanes(yc, cos, sin)[0], 0)
    cv_ref[0] = head(yc, 1)

    yT = _mm_nt(wt_ref[...], xn)
    ones_row = (lax.broadcasted_iota(jnp.int32, (V_ROWS - HEAD_DIM, TQ), 0) == 0).astype(vbT_ref.dtype)
    for c in range(TM // TQ):
        cols = slice(c * TQ, (c + 1) * TQ)
        for h in range(N_HEADS):
            vaT_ref[0, h, c] = yT[h * HEAD_DIM:(h + 1) * HEAD_DIM, cols].astype(vaT_ref.dtype)
        for n, v_ref in enumerate((vbT_ref, vsT_ref, vwT_ref)):
            lo = BRANCH_WIDTH + n * HEAD_DIM
            v_ref[0, c, 0:HEAD_DIM, :] = yT[lo:lo + HEAD_DIM, cols].astype(v_ref.dtype)
            v_ref[0, c, HEAD_DIM:, :] = ones_row
    ys = yT[WG_ROW:WG_ROW + 16, :]
    row = lax.broadcasted_iota(jnp.int32, ys.shape, 0)
    wg_ref[0] = jnp.where(row < IDX_HEADS, ys * W_IDX_SCALE, jax.nn.sigmoid(ys))


def _in_projection(h, g, w_in, cos_t, sin_t, B, S):
    cols = lambda *names: jnp.concatenate([w_in[:, _OFF[n][0]:_OFF[n][1]] for n in names], axis=1)
    wa = cols('q_a', 'k_a').astype(MXU_DTYPE)
    wr = cols('q_b', 'q_idx', 'q_c', 'k_b', 'k_idx', 'k_sel', 'k_win').astype(MXU_DTYPE)
    wc = cols('k_cmp', 'v_cmp').astype(MXU_DTYPE)
    wt = cols('v_a', 'v_b', 'v_sel', 'v_win', 'w_idx', 'g_c')
    wt = jnp.pad(wt, ((0, 0), (0, 512 - wt.shape[1]))).T.astype(MXU_DTYPE)
    nb, nc, cpb = S // TM, S // TQ, TM // TQ
    bf = lambda *shape: jax.ShapeDtypeStruct(shape, MXU_DTYPE)
    f32 = lambda *shape: jax.ShapeDtypeStruct(shape, F32)
    heads_spec = pl.BlockSpec((1, N_HEADS, TM, HEAD_DIM), lambda i: (i // nb, 0, i % nb, 0))
    tok_spec = pl.BlockSpec((1, TM, HEAD_DIM), lambda i: (i // nb, i % nb, 0))
    vT_spec = pl.BlockSpec((1, cpb, V_ROWS, TQ), lambda i: (i // nb, i % nb, 0, 0))
    pos = pl.BlockSpec((TM, LANES), lambda i: (i % nb, 0))
    q4, k3, v4 = bf(B, N_HEADS, S, HEAD_DIM), bf(B, S, HEAD_DIM), bf(B, nc, V_ROWS, TQ)
    return pl.pallas_call(
        _inproj_kernel,
        grid=(B * nb,),
        in_specs=[pl.BlockSpec((TM, D_MODEL), lambda i: (i, 0)), _const_spec((1, D_MODEL)), pos, pos,
                  _const_spec(wa.shape), _const_spec(wr.shape), _const_spec(wc.shape),
                  _const_spec(wt.shape)],
        out_specs=[heads_spec, heads_spec,
                   pl.BlockSpec((1, N_HEADS, cpb, HEAD_DIM, TQ), lambda i: (i // nb, 0, i % nb, 0, 0)),
                   heads_spec, heads_spec, heads_spec, tok_spec, tok_spec, tok_spec, tok_spec,
                   vT_spec, vT_spec, vT_spec,
                   pl.BlockSpec((1, 16, TM), lambda i: (i // nb, 0, i % nb)), tok_spec, tok_spec],
        out_shape=[q4, q4, bf(B, N_HEADS, nc, HEAD_DIM, TQ), q4, q4, q4, k3, k3, k3, k3, v4, v4, v4,
                   f32(B, 16, S), f32(B, S, HEAD_DIM), f32(B, S, HEAD_DIM)],
        compiler_params=_params("parallel"),
        name="in_projection",
    )(h, g.reshape(1, D_MODEL), cos_t, sin_t, wa, wr, wc, wt)


def _chunk_slice(j):
    return pl.ds(pl.multiple_of(j * TQ, TQ), TQ)


def _sb_kernel(q_ref, k_ref, vT_ref, tri_ref, o_ref, acc_ref):
    i = pl.program_id(1)
    row = lax.broadcasted_iota(jnp.int32, (TQ, TQ), 0)
    col = lax.broadcasted_iota(jnp.int32, (TQ, TQ), 1)
    strict = row < col
    tri = tri_ref[...]
    qs = [q_ref[0, h] for h in range(N_HEADS)]
    acc_ref[...] = jnp.zeros(acc_ref.shape, F32)

    def chunks(js, cs, diag=False):
        heads = range(N_HEADS)
        zs = [[_mm_nt(k_ref[0, h, _chunk_slice(j), :], qs[h]) for h in heads] for j in js]
        cums = []
        for zj in zs:
            cums.append([])
            for z in zj:
                sp = jnp.maximum(z, 0.0) + jnp.log(1.0 + jnp.exp(-jnp.abs(z)))
                if diag:
                    sp = jnp.where(strict, sp, 0.0)
                hi = sp.astype(MXU_DTYPE)
                lo = (sp - hi.astype(F32)).astype(MXU_DTYPE)
                cums[-1].append(_mm(tri, hi) + _mm(tri, lo))
        outs = [jnp.zeros((HEAD_DIM, TQ), F32) for _ in heads]
        cs = list(cs)
        for n, j in enumerate(js):
            for h in heads:
                w = jnp.exp(zs[n][h] - cums[n][h] - cs[h])
                if diag:
                    w = jnp.where(strict, w, 0.0)
                outs[h] = outs[h] + _mm(vT_ref[0, h, j], w)
                cs[h] = cs[h] + cums[n][h][0:1, :]
        for h in heads:
            acc_ref[h] += outs[h]
        return tuple(cs)

    cs = chunks([i], tuple(jnp.zeros((1, TQ), F32) for _ in range(N_HEADS)), diag=True)
    odd = i % 2
    cs = lax.cond(odd == 1, lambda cs: chunks([i - 1], cs), lambda cs: cs, cs)
    top = i - 1 - odd
    lax.fori_loop(0, i // 2, lambda t, cs: chunks([top - 2 * t, top - 2 * t - 1], cs), cs)
    o_ref[0] = acc_ref[...].reshape(BRANCH_WIDTH, TQ).T


def _stick_breaking(q, k, vT):
    B, H, S, _ = q.shape
    tri = jnp.asarray(np.triu(np.ones((TQ, TQ), np.float32)), MXU_DTYPE)
    return pl.pallas_call(
        _sb_kernel,
        grid=(B, S // TQ),
        in_specs=[pl.BlockSpec((1, H, TQ, HEAD_DIM), lambda b, i: (b, 0, i, 0)),
                  pl.BlockSpec((1, H, S, HEAD_DIM), lambda b, i: (b, 0, 0, 0)),
                  pl.BlockSpec((1, H, S // TQ, HEAD_DIM, TQ), lambda b, i: (b, 0, 0, 0, 0)),
                  _const_spec((TQ, TQ))],
        out_specs=pl.BlockSpec((1, TQ, BRANCH_WIDTH), lambda b, i: (b, i, 0)),
        out_shape=jax.ShapeDtypeStruct((B, S, BRANCH_WIDTH), F32),
        scratch_shapes=[pltpu.VMEM((H, HEAD_DIM, TQ), F32)],
        compiler_params=_params("parallel", "arbitrary"),
        name="stick_breaking",
    )(q, k, vT, tri)


def _colsum8(x):
    return x.reshape(x.shape[0] // SUBLANES, SUBLANES, x.shape[1]).sum(axis=0)


COUNT_ROWS = 64


def _softmax_step(ks, qs, biases, vTs, ms, acc_ref):
    heads = range(len(qs))
    ss = [[_mm_nt(k, q) + b for q in qs] for k, b in zip(ks, biases)]
    m_new = []
    for h in heads:
        m = ms[h]
        for st in ss:
            m = jnp.maximum(m, jnp.max(st[h], axis=0, keepdims=True))
        m_new.append(m)
    pvs = []
    for h in heads:
        pv = jnp.zeros((V_ROWS, TQ), F32)
        for st, vT in zip(ss, vTs):
            pv = pv + _mm(vT, jnp.exp(st[h] - m_new[h]))
        pvs.append(pv)
    for h in heads:
        acc_ref[h] = jnp.exp(ms[h] - m_new[h]) * acc_ref[h] + pvs[h]
    return tuple(m_new)


def _for_chunks(lo, hi, step, ms):
    odd = (hi - lo) % 2
    ms = lax.cond(odd == 1, lambda ms: step([lo], ms), lambda ms: ms, ms)
    first = lo + odd
    return lax.fori_loop(0, (hi - lo) // 2,
                         lambda t, ms: step([first + 2 * t, first + 2 * t + 1], ms), ms)


def _softmax_init(acc_ref):
    acc_ref[...] = jnp.zeros(acc_ref.shape, F32)
    return tuple(jnp.full((1, TQ), NEG, F32) for _ in range(N_HEADS))


def _softmax_result(acc_ref, h):
    return acc_ref[h, 0:HEAD_DIM, :] / acc_ref[h, HEAD_DIM:HEAD_DIM + 1, :]


def _dsa_kernel(qi_ref, ki_ref, wg_ref, q_ref, k_ref, vT_ref, tril_ref, o_ref, sc_ref, acc_ref, *, topk):
    i = pl.program_id(1)
    nchunks = i + 1
    npairs = (i + 2) // 2
    qpos = i * TQ + lax.broadcasted_iota(jnp.int32, (1, TQ), 1)
    rowi = lax.broadcasted_iota(jnp.int32, (TQ, TQ), 0)
    int_min = jnp.int32(-2 ** 31)

    qis = [qi_ref[0, hh] for hh in range(IDX_HEADS)]
    wT = wg_ref[0]

    def idx_chunks(js, carry):
        rs = [[_mm_nt(ki_ref[0, _chunk_slice(j), :], qh) for qh in qis] for j in js]
        for j, rj in zip(js, rs):
            acc = jnp.zeros((TQ, TQ), F32)
            for hh in range(IDX_HEADS):
                acc = acc + wT[hh:hh + 1, :] * jnp.maximum(rj[hh], 0.0)
            sc_ref[_chunk_slice(j), :] = jnp.where(j * TQ + rowi <= qpos, acc, -jnp.inf)
        return carry

    _for_chunks(0, nchunks, idx_chunks, 0)

    @pl.when(nchunks % 2 == 1)
    def _():
        sc_ref[_chunk_slice(nchunks), :] = jnp.full((TQ, TQ), -jnp.inf, F32)

    def count(pred):
        def body(j, acc):
            for r in range(2 * TQ // COUNT_ROWS):
                base = pl.multiple_of(j * 2 * TQ + r * COUNT_ROWS, COUNT_ROWS)
                acc = acc + _colsum8(jnp.where(pred(sc_ref[pl.ds(base, COUNT_ROWS), :]), 1.0, 0.0))
            return acc
        return jnp.sum(lax.fori_loop(0, npairs, body, jnp.zeros((SUBLANES, TQ), F32)),
                       axis=0, keepdims=True)

    take_all = qpos < topk

    def search():
        def key_to_float(u):
            s = u ^ int_min
            return lax.bitcast_convert_type(jnp.where(s < 0, s ^ jnp.int32(0x7FFFFFFF), s), F32)

        def bit_body(t, carry):
            res, n_ge, n_gt = carry
            cand = res | jnp.left_shift(jnp.int32(1), 31 - t)
            thr = key_to_float(cand)
            n = count(lambda blk: blk >= thr)
            ok = n >= topk
            return jnp.where(ok, cand, res), jnp.where(ok, n, n_ge), jnp.where(ok, n_gt, n)

        zero = jnp.zeros((1, TQ), F32)
        res, n_ge, n_gt = lax.fori_loop(0, 32, bit_body, (jnp.zeros((1, TQ), jnp.int32), zero, zero))
        thr = key_to_float(res)

        @pl.when(jnp.max(jnp.where(take_all, 0.0, n_ge)) > topk)
        def _():
            need = jnp.where(take_all, float(2 ** 30), topk - n_gt)
            tril = tril_ref[...]

            def drop(j, before):
                blk = sc_ref[_chunk_slice(j), :]
                tied = blk == thr
                rank = _mm(tril, jnp.where(tied, 1.0, 0.0)) + before
                sc_ref[_chunk_slice(j), :] = jnp.where(tied & (rank > need), -jnp.inf, blk)
                return rank[TQ - 1:TQ, :]

            lax.fori_loop(0, nchunks, drop, zero)

        return thr

    lowest = float(np.finfo(np.float32).min)
    thr = lax.cond(i * TQ + TQ > topk, search, lambda: jnp.full((1, TQ), lowest, F32))
    thr = jnp.where(take_all, lowest, thr)

    qs = [q_ref[0, h] for h in range(N_HEADS)]

    def att_chunks(js, ms):
        biases = [jnp.where(sc_ref[_chunk_slice(j), :] >= thr, 0.0, NEG) for j in js]
        return _softmax_step([k_ref[0, _chunk_slice(j), :] for j in js], qs, biases,
                             [vT_ref[0, j] for j in js], ms, acc_ref)

    _for_chunks(0, nchunks, att_chunks, _softmax_init(acc_ref))
    o_ref[0] = jnp.concatenate([_softmax_result(acc_ref, h) for h in range(N_HEADS)], axis=0).T


def _dsa_attention(qidx, kidx, wg, qb, kb, vbT, B, S):
    topk = min(DSA_TOPK, S // 4)
    assert topk <= TQ
    nc = S // TQ
    q_spec = pl.BlockSpec((1, N_HEADS, TQ, HEAD_DIM), lambda b, i: (b, 0, i, 0))
    k_spec = pl.BlockSpec((1, S, HEAD_DIM), lambda b, i: (b, 0, 0))
    tril = jnp.asarray(np.tril(np.ones((TQ, TQ), np.float32)), MXU_DTYPE)
    return pl.pallas_call(
        functools.partial(_dsa_kernel, topk=topk),
        grid=(B, nc),
        in_specs=[q_spec, k_spec, pl.BlockSpec((1, 16, TQ), lambda b, i: (b, 0, i)), q_spec, k_spec,
                  pl.BlockSpec((1, nc, V_ROWS, TQ), lambda b, i: (b, 0, 0, 0)), _const_spec((TQ, TQ))],
        out_specs=pl.BlockSpec((1, TQ, BRANCH_WIDTH), lambda b, i: (b, i, 0)),
        out_shape=jax.ShapeDtypeStruct((B, S, BRANCH_WIDTH), F32),
        scratch_shapes=[pltpu.VMEM(((nc + nc % 2) * TQ, TQ), F32),
                        pltpu.VMEM((N_HEADS, V_ROWS, TQ), F32)],
        compiler_params=_params("parallel", "arbitrary"),
        name="dsa_attention",
    )(qidx, kidx, wg, qb, kb, vbT, tril)


def _compress_kernel(c_ref, pos_ref, w1_ref, w2_ref, o_ref, *, transpose_out):
    n = c_ref.shape[1] // CMP_STRIDE
    top = jnp.zeros((n, CMP_HIDDEN), F32)
    bot = jnp.zeros((n, CMP_HIDDEN), F32)
    for p in range(CMP_STRIDE):
        x = c_ref[0, pl.ds(p, n, stride=CMP_STRIDE), :]
        q = p + CMP_STRIDE
        top = top + _mm(x + pos_ref[p:p + 1, :], w1_ref[p * HEAD_DIM:(p + 1) * HEAD_DIM, :])
        bot = bot + _mm(x + pos_ref[q:q + 1, :], w1_ref[q * HEAD_DIM:(q + 1) * HEAD_DIM, :])
    hid = jax.nn.gelu(top + pltpu.roll(bot, n - 1, 0), approximate=True)
    out = _mm(hid, w2_ref[...])
    rowi = lax.broadcasted_iota(jnp.int32, out.shape, 0)
    out = jnp.where(rowi < n - 1, out, 0.0)
    if transpose_out:
        out = jnp.concatenate([out, jnp.zeros_like(out)], axis=1).T[0:HEAD_DIM, :]
    o_ref[0] = out.astype(o_ref.dtype)


def _compress(tok, pos_emb, w1, w2, transpose_out):
    B, S, _ = tok.shape
    assert CMP_LEN == 2 * CMP_STRIDE
    nch = S // CMP_STRIDE
    out_block = (1, HEAD_DIM, nch) if transpose_out else (1, nch, HEAD_DIM)
    return pl.pallas_call(
        functools.partial(_compress_kernel, transpose_out=transpose_out),
        grid=(B,),
        in_specs=[pl.BlockSpec((1, S, HEAD_DIM), lambda b: (b, 0, 0)),
                  _const_spec(pos_emb.shape), _const_spec(w1.shape), _const_spec(w2.shape)],
        out_specs=pl.BlockSpec(out_block, lambda b: (b, 0, 0)),
        out_shape=jax.ShapeDtypeStruct((B,) + out_block[1:], MXU_DTYPE),
        compiler_params=_params("parallel"),
        name="nsa_compress",
    )(tok, pos_emb, w1.astype(MXU_DTYPE), w2.astype(MXU_DTYPE))


def _nsa_kernel(q_ref, wg_ref, kc_ref, vcT_ref, ovT_ref, ks_ref, vsT_ref, kw_ref, vwT_ref,
                o_ref, selb_ref, acc_s_ref, acc_w_ref, *, n_cmp, n_sel, topn):
    i = pl.program_id(1)
    qpos = i * TQ + lax.broadcasted_iota(jnp.int32, (1, TQ), 1)
    rowi = lax.broadcasted_iota(jnp.int32, (TQ, TQ), 0)
    heads = [q_ref[0, h] for h in range(N_HEADS)]
    gates = wg_ref[0]

    ncp = kc_ref.shape[1]
    cidx = lax.broadcasted_iota(jnp.int32, (ncp, TQ), 0)
    cmask = (cidx * CMP_STRIDE + (CMP_LEN - 1) <= qpos) & (cidx < n_cmp)
    kc, vcT, ovT = kc_ref[0], vcT_ref[0], ovT_ref[...]
    ss = [jnp.where(cmask, _mm_nt(kc, q), -jnp.inf) for q in heads]
    ps = []
    for s in ss:
        m = jnp.max(s, axis=0, keepdims=True)
        m = jnp.where(m > -jnp.inf, m, 0.0)
        e = jnp.where(cmask, jnp.exp(s - m), 0.0)
        ps.append(e / jnp.maximum(jnp.sum(e, axis=0, keepdims=True), 1e-30))
    o_cmp = [_mm(vcT, p) for p in ps]
    imp = jnp.zeros((n_sel, TQ), F32)
    for p in ps:
        p_hi = p.astype(MXU_DTYPE)
        imp = imp + _mm(ovT, p_hi) + _mm(ovT, p - p_hi.astype(F32))

    bidx = lax.broadcasted_iota(jnp.int32, (n_sel, TQ), 0)
    cur = qpos // SEL_LEN
    forced = (bidx == 0) | (bidx == cur) | (bidx == cur - 1)
    work = jnp.where(bidx * SEL_LEN <= qpos, jnp.where(forced, FORCED_SCORE, imp), -jnp.inf)
    sel = jnp.zeros((n_sel, TQ), F32)
    for _ in range(topn):
        m = jnp.max(work, axis=0, keepdims=True)
        first = jnp.min(jnp.where((work == m) & (m > -jnp.inf), bidx, n_sel), axis=0, keepdims=True)
        pick = bidx == first
        sel = jnp.where(pick, 1.0, sel)
        work = jnp.where(pick, -jnp.inf, work)
    selb_ref[...] = jnp.where(sel > 0.0, 0.0, NEG)

    per_chunk = TQ // SEL_LEN

    def sel_bias(j):
        rows = [jnp.broadcast_to(selb_ref[pl.ds(j * per_chunk + bb, 1), :], (SEL_LEN, TQ))
                for bb in range(per_chunk)]
        return jnp.where(j * TQ + rowi <= qpos, jnp.concatenate(rows, axis=0), NEG)

    def sel_chunks(js, ms):
        return _softmax_step([ks_ref[0, _chunk_slice(j), :] for j in js], heads,
                             [sel_bias(j) for j in js], [vsT_ref[0, j] for j in js], ms, acc_s_ref)

    _for_chunks(0, i + 1, sel_chunks, _softmax_init(acc_s_ref))

    def win_bias(j):
        diff = qpos - (j * TQ + rowi)
        return jnp.where((diff >= 0) & (diff < WINDOW), 0.0, NEG)

    def win_chunks(js, ms):
        return _softmax_step([kw_ref[0, _chunk_slice(j), :] for j in js], heads,
                             [win_bias(j) for j in js], [vwT_ref[0, j] for j in js], ms, acc_w_ref)

    _for_chunks(jnp.maximum(i - WINDOW // TQ, 0), i + 1, win_chunks, _softmax_init(acc_w_ref))

    outs = []
    for h in range(N_HEADS):
        g = lambda br: gates[IDX_HEADS + 3 * h + br:IDX_HEADS + 3 * h + br + 1, :]
        outs.append(g(0) * o_cmp[h] + g(1) * _softmax_result(acc_s_ref, h)
                    + g(2) * _softmax_result(acc_w_ref, h))
    o_ref[0] = jnp.concatenate(outs, axis=0).T


def _nsa_attention(qc, wg, kc, vcT, ksel, vselT, kwin, vwinT, B, S):
    n_cmp = (S - CMP_LEN) // CMP_STRIDE + 1
    n_sel = S // SEL_LEN
    topn = min(SEL_TOPN, n_sel)
    ncp = kc.shape[1]
    cs = np.arange(ncp) * CMP_STRIDE
    ss = np.arange(n_sel) * SEL_LEN
    overlapT = ((cs[None, :] < ss[:, None] + SEL_LEN) & (cs[None, :] + CMP_LEN > ss[:, None])
                & (np.arange(ncp)[None, :] < n_cmp)).astype(np.float32)
    k_spec = pl.BlockSpec((1, S, HEAD_DIM), lambda b, i: (b, 0, 0))
    vT_spec = pl.BlockSpec((1, S // TQ, V_ROWS, TQ), lambda b, i: (b, 0, 0, 0))
    return pl.pallas_call(
        functools.partial(_nsa_kernel, n_cmp=n_cmp, n_sel=n_sel, topn=topn),
        grid=(B, S // TQ),
        in_specs=[pl.BlockSpec((1, N_HEADS, TQ, HEAD_DIM), lambda b, i: (b, 0, i, 0)),
                  pl.BlockSpec((1, 16, TQ), lambda b, i: (b, 0, i)),
                  pl.BlockSpec((1, ncp, HEAD_DIM), lambda b, i: (b, 0, 0)),
                  pl.BlockSpec((1, HEAD_DIM, ncp), lambda b, i: (b, 0, 0)),
                  _const_spec((n_sel, ncp)), k_spec, vT_spec, k_spec, vT_spec],
        out_specs=pl.BlockSpec((1, TQ, BRANCH_WIDTH), lambda b, i: (b, i, 0)),
        out_shape=jax.ShapeDtypeStruct((B, S, BRANCH_WIDTH), F32),
        scratch_shapes=[pltpu.VMEM((n_sel, TQ), F32), pltpu.VMEM((N_HEADS, V_ROWS, TQ), F32),
                        pltpu.VMEM((N_HEADS, V_ROWS, TQ), F32)],
        compiler_params=_params("parallel", "arbitrary"),
        name="nsa_attention",
    )(qc, wg, kc, vcT, jnp.asarray(overlapT, MXU_DTYPE), ksel, vselT, kwin, vwinT)


def _merge_kernel(h_ref, g_ref, oa_ref, ob_ref, oc_ref, wg_ref, wup_ref, wout_ref, o_ref):
    x = h_ref[...]
    xn = _rmsnorm(x, g_ref[...]).astype(MXU_DTYPE)
    merged = jnp.zeros((TM, D_MODEL), F32)
    for j, br_ref in enumerate((oa_ref, ob_ref, oc_ref)):
        gate = jax.nn.sigmoid(_mm(xn, wg_ref[:, j * D_MODEL:(j + 1) * D_MODEL]))
        merged = merged + gate * _mm(br_ref[...], wup_ref[j])
    o_ref[...] = x + _mm(merged, wout_ref[...])


def _merge(h, g, o_a, o_b, o_c, w_gm, w_up, w_out):
    T = h.shape[0]
    row = lambda w: pl.BlockSpec((TM, w), lambda i: (i, 0))
    return pl.pallas_call(
        _merge_kernel,
        grid=(T // TM,),
        in_specs=[row(D_MODEL), _const_spec((1, D_MODEL)), row(BRANCH_WIDTH), row(BRANCH_WIDTH),
                  row(BRANCH_WIDTH), _const_spec(w_gm.shape), _const_spec(w_up.shape),
                  _const_spec(w_out.shape)],
        out_specs=row(D_MODEL),
        out_shape=jax.ShapeDtypeStruct((T, D_MODEL), F32),
        compiler_params=_params("parallel"),
        name="merge",
    )(h, g.reshape(1, D_MODEL), o_a, o_b, o_c, w_gm.astype(MXU_DTYPE), w_up.astype(MXU_DTYPE),
      w_out.astype(MXU_DTYPE))


def _norm_matmul_kernel(x_ref, g_ref, w_ref, o_ref):
    o_ref[...] = _mm(_rmsnorm(x_ref[...], g_ref[...]), w_ref[...]).astype(o_ref.dtype)


def _norm_matmul(x, g, w, tm):
    T, n = x.shape[0], w.shape[1]
    return pl.pallas_call(
        _norm_matmul_kernel,
        grid=(T // tm,),
        in_specs=[pl.BlockSpec((tm, D_MODEL), lambda i: (i, 0)), _const_spec((1, D_MODEL)),
                  _const_spec(w.shape)],
        out_specs=pl.BlockSpec((tm, n), lambda i: (i, 0)),
        out_shape=jax.ShapeDtypeStruct((T, n), MXU_DTYPE),
        compiler_params=_params("parallel"),
        name="norm_matmul",
    )(x, g.reshape(1, D_MODEL), w.astype(MXU_DTYPE))


def _mem_attn_kernel(h_ref, g_ref, wq_ref, kv_ref, wo_ref, o_ref):
    x = h_ref[...]
    q = (_mm(_rmsnorm(x, g_ref[...]), wq_ref[...]) * Q_SCALE).astype(MXU_DTYPE)
    kv = kv_ref[0]
    out = x
    for h in range(N_HEADS):
        lo, hi = h * HEAD_DIM, (h + 1) * HEAD_DIM
        s = _mm_nt(q[:, lo:hi], kv[:, lo:hi])
        e = jnp.exp(s - jnp.max(s, axis=-1, keepdims=True))
        p = e / jnp.sum(e, axis=-1, keepdims=True)
        o_h = _mm(p, kv[:, BRANCH_WIDTH + lo:BRANCH_WIDTH + hi])
        out = out + _mm(o_h, wo_ref[lo:hi, :])
    o_ref[...] = out


def _memory_attention(h, kv, g_q, w_q, w_o, S):
    T = h.shape[0]
    M = kv.shape[1]
    row = pl.BlockSpec((TM, D_MODEL), lambda i: (i, 0))
    nb = S // TM
    return pl.pallas_call(
        _mem_attn_kernel,
        grid=(T // TM,),
        in_specs=[row, _const_spec((1, D_MODEL)), _const_spec(w_q.shape),
                  pl.BlockSpec((1, M, 2 * BRANCH_WIDTH), lambda i: (i // nb, 0, 0)),
                  _const_spec(w_o.shape)],
        out_specs=row,
        out_shape=jax.ShapeDtypeStruct((T, D_MODEL), F32),
        compiler_params=_params("parallel"),
        name="memory_attention",
    )(h, g_q.reshape(1, D_MODEL), w_q.astype(MXU_DTYPE), kv, w_o.astype(MXU_DTYPE))


FF_CHUNK = 256


def _swiglu_kernel(h_ref, g_ref, win_ref, wout_ref, gf_ref, o_ref, *, final_norm):
    x = h_ref[...]
    xn = _rmsnorm(x, g_ref[...]).astype(MXU_DTYPE)
    out = x
    for c in range(D_FF // FF_CHUNK):
        lo, hi = c * FF_CHUNK, (c + 1) * FF_CHUNK
        a = _mm(xn, win_ref[:, lo:hi])
        b = _mm(xn, win_ref[:, D_FF + lo:D_FF + hi])
        out = out + _mm(a * jax.nn.sigmoid(a) * b, wout_ref[lo:hi, :])
    if final_norm:
        out = _rmsnorm(out, gf_ref[...])
    o_ref[...] = out


def _swiglu(h, g, w_in, w_out, g_final, final_norm):
    T = h.shape[0]
    row = pl.BlockSpec((TM, D_MODEL), lambda i: (i, 0))
    return pl.pallas_call(
        functools.partial(_swiglu_kernel, final_norm=final_norm),
        grid=(T // TM,),
        in_specs=[row, _const_spec((1, D_MODEL)), _const_spec(w_in.shape), _const_spec(w_out.shape),
                  _const_spec((1, D_MODEL))],
        out_specs=row,
        out_shape=jax.ShapeDtypeStruct((T, D_MODEL), F32),
        compiler_params=_params("parallel"),
        name="swiglu",
    )(h, g.reshape(1, D_MODEL), w_in.astype(MXU_DTYPE), w_out.astype(MXU_DTYPE),
      g_final.reshape(1, D_MODEL))


def _rope_tables(S):
    inv = ROPE_THETA ** (-jnp.arange(0, HEAD_DIM, 2, dtype=F32) / HEAD_DIM)
    ang = jnp.arange(S, dtype=F32)[:, None] * inv[None, :]
    cos, sin = jnp.cos(ang), jnp.sin(ang)
    return (jnp.concatenate([cos, cos, cos, cos], axis=1),
            jnp.concatenate([-sin, sin, -sin, sin], axis=1))


def _hybrid_mixer(h, B, S, g, w_in, cmp_k, cmp_v, w_up, w_out, cos_t, sin_t):
    T = B * S
    (qa, ka, vaT, qb, qidx, qc, kb, kidx, ksel, kwin, vbT, vselT, vwinT, wg, ck, cv) = _in_projection(
        h, g, w_in, cos_t, sin_t, B, S)
    o_a = _stick_breaking(qa, ka, vaT)
    o_b = _dsa_attention(qidx, kidx, wg, qb, kb, vbT, B, S)
    kc = _compress(ck, *cmp_k, transpose_out=False)
    vcT = _compress(cv, *cmp_v, transpose_out=True)
    o_c = _nsa_attention(qc, wg, kc, vcT, ksel, vselT, kwin, vwinT, B, S)

    w_gm = w_in[:, _OFF['g_merge'][0]:]
    return _merge(h, g, o_a.reshape(T, 256), o_b.reshape(T, 256), o_c.reshape(T, 256), w_gm, w_up, w_out)


def kernel(x, mem, norm_mix, w_in, cmp_pos_k, cmp_w1_k, cmp_w2_k, cmp_pos_v, cmp_w1_v, cmp_w2_v,
           w_up, w_out, norm_mem_q, norm_mem_kv, w_mem_q, w_mem_kv, w_mem_o,
           norm_ffn, w_ffn_in, w_ffn_out, norm_final):
    B, S, D = x.shape
    M = mem.shape[1]
    depth = w_in.shape[0]
    assert D == D_MODEL and S % TM == 0 and S % TQ == 0 and WINDOW % TQ == 0
    cos_t, sin_t = _rope_tables(S)
    h = x.reshape(B * S, D)
    mem2 = mem.reshape(B * M, D)
    for l in range(depth):
        h = _hybrid_mixer(h, B, S, norm_mix[l], w_in[l],
                          (cmp_pos_k[l], cmp_w1_k[l], cmp_w2_k[l]),
                          (cmp_pos_v[l], cmp_w1_v[l], cmp_w2_v[l]), w_up[l], w_out[l], cos_t, sin_t)
        kv = _norm_matmul(mem2, norm_mem_kv[l], w_mem_kv[l], min(TM, B * M))
        h = _memory_attention(h, kv.reshape(B, M, 2 * BRANCH_WIDTH), norm_mem_q[l], w_mem_q[l],
                              w_mem_o[l], S)
        h = _swiglu(h, norm_ffn[l], w_ffn_in[l], w_ffn_out[l], norm_final, l == depth - 1)
    return h.reshape(B, S, D)
```

```python
import functools

import jax
import jax.numpy as jnp
import numpy as np
from jax import lax
from jax.experimental import pallas as pl
from jax.experimental.pallas import tpu as pltpu

F32 = jnp.float32
MXU_DTYPE = jnp.bfloat16

D_MODEL = 1024
HEAD_DIM = 64
N_HEADS = 4
BRANCH_WIDTH = N_HEADS * HEAD_DIM
IDX_HEADS = 4
DSA_TOPK = 256
CMP_LEN = 32
CMP_STRIDE = 16
CMP_HIDDEN = 128
SEL_LEN = 64
SEL_TOPN = 16
WINDOW = 512
D_FF = 2816
ROPE_THETA = 10000.0
RMS_EPS = 1e-6
FORCED_SCORE = 1e4
Q_SCALE = HEAD_DIM ** -0.5
LOG2E = 1.4426950408889634
SOFTMAX_Q_SCALE = Q_SCALE * LOG2E
W_IDX_SCALE = (IDX_HEADS * HEAD_DIM) ** -0.5
NEG = -1e30

LANES = 128
SUBLANES = 8
VMEM_LIMIT = 56 * 1024 * 1024
TM = 512
TQ = 256

_OFF = {}
_o = 0
for _n, _w in (('q_a', 256), ('k_a', 256), ('v_a', 256), ('q_b', 256), ('k_b', 64), ('v_b', 64),
               ('q_idx', 256), ('k_idx', 64), ('w_idx', 4), ('q_c', 256), ('k_cmp', 64),
               ('v_cmp', 64), ('k_sel', 64), ('v_sel', 64), ('k_win', 64), ('v_win', 64),
               ('g_c', 12), ('g_merge', 3 * D_MODEL)):
    _OFF[_n] = (_o, _o + _w)
    _o += _w
D_IN = _o


def _mm(a, b):
    return jnp.dot(a.astype(MXU_DTYPE), b.astype(MXU_DTYPE), preferred_element_type=F32)


def _mm_nt(a, b):
    return lax.dot_general(a.astype(MXU_DTYPE), b.astype(MXU_DTYPE),
                           (((1,), (1,)), ((), ())), preferred_element_type=F32)


def _params(*sem):
    return pltpu.CompilerParams(dimension_semantics=sem, vmem_limit_bytes=VMEM_LIMIT)


def _const_spec(shape):
    nd = len(shape)
    return pl.BlockSpec(shape, lambda *_: (0,) * nd, pipeline_mode=pl.Buffered(1))


def _rmsnorm(x, g):
    return x * lax.rsqrt(jnp.mean(x * x, axis=-1, keepdims=True) + RMS_EPS) * g


def _rope_lanes(x, cos, sin):
    lane = lax.broadcasted_iota(jnp.int32, (1, LANES), 1)
    first_half = (lane % HEAD_DIM) < (HEAD_DIM // 2)
    out = []
    for g in range(x.shape[1] // LANES):
        xg = x[:, g * LANES:(g + 1) * LANES]
        rot = jnp.where(first_half, pltpu.roll(xg, LANES - HEAD_DIM // 2, 1),
                        pltpu.roll(xg, HEAD_DIM // 2, 1))
        out.append(xg * cos + rot * sin)
    return out


WG_ROW = BRANCH_WIDTH + 3 * HEAD_DIM
V_ROWS = HEAD_DIM + 2 * SUBLANES


def _inproj_kernel(h_ref, g_ref, cos_ref, sin_ref, wa_ref, wr_ref, wc_ref, wt_ref,
                   qa_ref, ka_ref, vaT_ref, qb_ref, qi_ref, qc_ref, kb_ref, ki_ref, ks_ref, kw_ref,
                   vbT_ref, vsT_ref, vwT_ref, wg_ref, ck_ref, cv_ref):
    xn = _rmsnorm(h_ref[...], g_ref[...]).astype(MXU_DTYPE)
    cos, sin = cos_ref[...], sin_ref[...]
    head = lambda y, h: y[:, h * HEAD_DIM:(h + 1) * HEAD_DIM]

    ya = _mm(xn, wa_ref[...])
    for h in range(N_HEADS):
        qa_ref[0, h] = (head(ya, h) * Q_SCALE).astype(qa_ref.dtype)
        ka_ref[0, h] = head(ya, N_HEADS + h).astype(ka_ref.dtype)

    yr = _rope_lanes(_mm(xn, wr_ref[...]), cos, sin)
    for g, (q_ref, scale) in enumerate(((qb_ref, SOFTMAX_Q_SCALE), (qi_ref, 1.0),
                                        (qc_ref, SOFTMAX_Q_SCALE))):
        for half in range(2):
            og = yr[2 * g + half] * scale
            q_ref[0, 2 * half] = head(og, 0).astype(q_ref.dtype)
            q_ref[0, 2 * half + 1] = head(og, 1).astype(q_ref.dtype)
    kb_ref[0] = head(yr[6], 0).astype(kb_ref.dtype)
    ki_ref[0] = head(yr[6], 1).astype(ki_ref.dtype)
    ks_ref[0] = head(yr[7], 0).astype(ks_ref.dtype)
    kw_ref[0] = head(yr[7], 1).astype(kw_ref.dtype)

    yc = _mm(xn, wc_ref[...])
    ck_ref[0] = head(_rope_lanes(yc, cos, sin)[0], 0)
    cv_ref[0] = head(yc, 1)

    yT = _mm_nt(wt_ref[...], xn)
    ones_row = (lax.broadcasted_iota(jnp.int32, (V_ROWS - HEAD_DIM, TQ), 0) == 0).astype(vbT_ref.dtype)
    for c in range(TM // TQ):
        cols = slice(c * TQ, (c + 1) * TQ)
        for h in range(N_HEADS):
            vaT_ref[0, h, c] = yT[h * HEAD_DIM:(h + 1) * HEAD_DIM, cols].astype(vaT_ref.dtype)
        for n, v_ref in enumerate((vbT_ref, vsT_ref, vwT_ref)):
            lo = BRANCH_WIDTH + n * HEAD_DIM
            v_ref[0, c, 0:HEAD_DIM, :] = yT[lo:lo + HEAD_DIM, cols].astype(v_ref.dtype)
            v_ref[0, c, HEAD_DIM:, :] = ones_row
    ys = yT[WG_ROW:WG_ROW + 16, :]
    row = lax.broadcasted_iota(jnp.int32, ys.shape, 0)
    wg_ref[0] = jnp.where(row < IDX_HEADS, ys * W_IDX_SCALE, jax.nn.sigmoid(ys))


def _in_projection(h, g, w_in, cos_t, sin_t, B, S):
    cols = lambda *names: jnp.concatenate([w_in[:, _OFF[n][0]:_OFF[n][1]] for n in names], axis=1)
    wa = cols('q_a', 'k_a').astype(MXU_DTYPE)
    wr = cols('q_b', 'q_idx', 'q_c', 'k_b', 'k_idx', 'k_sel', 'k_win').astype(MXU_DTYPE)
    wc = cols('k_cmp', 'v_cmp').astype(MXU_DTYPE)
    wt = cols('v_a', 'v_b', 'v_sel', 'v_win', 'w_idx', 'g_c')
    wt = jnp.pad(wt, ((0, 0), (0, 512 - wt.shape[1]))).T.astype(MXU_DTYPE)
    nb, nc, cpb = S // TM, S // TQ, TM // TQ
    bf = lambda *shape: jax.ShapeDtypeStruct(shape, MXU_DTYPE)
    f32 = lambda *shape: jax.ShapeDtypeStruct(shape, F32)
    heads_spec = pl.BlockSpec((1, N_HEADS, TM, HEAD_DIM), lambda i: (i // nb, 0, i % nb, 0))
    tok_spec = pl.BlockSpec((1, TM, HEAD_DIM), lambda i: (i // nb, i % nb, 0))
    vT_spec = pl.BlockSpec((1, cpb, V_ROWS, TQ), lambda i: (i // nb, i % nb, 0, 0))
    pos = pl.BlockSpec((TM, LANES), lambda i: (i % nb, 0))
    q4, k3, v4 = bf(B, N_HEADS, S, HEAD_DIM), bf(B, S, HEAD_DIM), bf(B, nc, V_ROWS, TQ)
    return pl.pallas_call(
        _inproj_kernel,
        grid=(B * nb,),
        in_specs=[pl.BlockSpec((TM, D_MODEL), lambda i: (i, 0)), _const_spec((1, D_MODEL)), pos, pos,
                  _const_spec(wa.shape), _const_spec(wr.shape), _const_spec(wc.shape),
                  _const_spec(wt.shape)],
        out_specs=[heads_spec, heads_spec,
                   pl.BlockSpec((1, N_HEADS, cpb, HEAD_DIM, TQ), lambda i: (i // nb, 0, i % nb, 0, 0)),
                   heads_spec, heads_spec, heads_spec, tok_spec, tok_spec, tok_spec, tok_spec,
                   vT_spec, vT_spec, vT_spec,
                   pl.BlockSpec((1, 16, TM), lambda i: (i // nb, 0, i % nb)), tok_spec, tok_spec],
        out_shape=[q4, q4, bf(B, N_HEADS, nc, HEAD_DIM, TQ), q4, q4, q4, k3, k3, k3, k3, v4, v4, v4,
                   f32(B, 16, S), f32(B, S, HEAD_DIM), f32(B, S, HEAD_DIM)],
        compiler_params=_params("parallel"),
        name="in_projection",
    )(h, g.reshape(1, D_MODEL), cos_t, sin_t, wa, wr, wc, wt)


def _chunk_slice(j):
    return pl.ds(pl.multiple_of(j * TQ, TQ), TQ)


def _sb_kernel(q_ref, k_ref, vT_ref, tri_ref, o_ref, acc_ref):
    i = pl.program_id(1)
    row = lax.broadcasted_iota(jnp.int32, (TQ, TQ), 0)
    col = lax.broadcasted_iota(jnp.int32, (TQ, TQ), 1)
    strict = row < col
    tri = tri_ref[...]
    qs = [q_ref[0, h] for h in range(N_HEADS)]
    acc_ref[...] = jnp.zeros(acc_ref.shape, F32)

    def chunks(js, cs, diag=False):
        heads = range(N_HEADS)
        zs = [[_mm_nt(k_ref[0, h, _chunk_slice(j), :], qs[h]) for h in heads] for j in js]
        cums = []
        for zj in zs:
            cums.append([])
            for z in zj:
                sp = jnp.maximum(z, 0.0) + jnp.log(1.0 + jnp.exp(-jnp.abs(z)))
                if diag:
                    sp = jnp.where(strict, sp, 0.0)
                hi = sp.astype(MXU_DTYPE)
                lo = (sp - hi.astype(F32)).astype(MXU_DTYPE)
                cums[-1].append(_mm(tri, hi) + _mm(tri, lo))
        outs = [jnp.zeros((HEAD_DIM, TQ), F32) for _ in heads]
        cs = list(cs)
        for n, j in enumerate(js):
            for h in heads:
                w = jnp.exp(zs[n][h] - cums[n][h] - cs[h])
                if diag:
                    w = jnp.where(strict, w, 0.0)
                outs[h] = outs[h] + _mm(vT_ref[0, h, j], w)
                cs[h] = cs[h] + cums[n][h][0:1, :]
        for h in heads:
            acc_ref[h] += outs[h]
        return tuple(cs)

    cs = chunks([i], tuple(jnp.zeros((1, TQ), F32) for _ in range(N_HEADS)), diag=True)
    odd = i % 2
    cs = lax.cond(odd == 1, lambda cs: chunks([i - 1], cs), lambda cs: cs, cs)
    top = i - 1 - odd
    lax.fori_loop(0, i // 2, lambda t, cs: chunks([top - 2 * t, top - 2 * t - 1], cs), cs)
    o_ref[0] = acc_ref[...].reshape(BRANCH_WIDTH, TQ).T


def _stick_breaking(q, k, vT):
    B, H, S, _ = q.shape
    tri = jnp.asarray(np.triu(np.ones((TQ, TQ), np.float32)), MXU_DTYPE)
    return pl.pallas_call(
        _sb_kernel,
        grid=(B, S // TQ),
        in_specs=[pl.BlockSpec((1, H, TQ, HEAD_DIM), lambda b, i: (b, 0, i, 0)),
                  pl.BlockSpec((1, H, S, HEAD_DIM), lambda b, i: (b, 0, 0, 0)),
                  pl.BlockSpec((1, H, S // TQ, HEAD_DIM, TQ), lambda b, i: (b, 0, 0, 0, 0)),
                  _const_spec((TQ, TQ))],
        out_specs=pl.BlockSpec((1, TQ, BRANCH_WIDTH), lambda b, i: (b, i, 0)),
        out_shape=jax.ShapeDtypeStruct((B, S, BRANCH_WIDTH), F32),
        scratch_shapes=[pltpu.VMEM((H, HEAD_DIM, TQ), F32)],
        compiler_params=_params("parallel", "arbitrary"),
        name="stick_breaking",
    )(q, k, vT, tri)


def _colsum8(x):
    return x.reshape(x.shape[0] // SUBLANES, SUBLANES, x.shape[1]).sum(axis=0)


COUNT_ROWS = 64
PACKED_ROWS = 2 * SUBLANES


def _softmax_step(ks, qs, biases, vTs, ms, acc_ref):
    heads = range(len(qs))
    ss = [[_mm_nt(k, q) + b for q in qs] for k, b in zip(ks, biases)]
    m_new = []
    for h in heads:
        m = ms[h]
        for st in ss:
            m = jnp.maximum(m, jnp.max(st[h], axis=0, keepdims=True))
        m_new.append(m)
    pvs = []
    for h in heads:
        pv = jnp.zeros((V_ROWS, TQ), F32)
        for st, vT in zip(ss, vTs):
            pv = pv + _mm(vT, jnp.exp2(st[h] - m_new[h]))
        pvs.append(pv)
    for h in heads:
        acc_ref[h] = jnp.exp2(ms[h] - m_new[h]) * acc_ref[h] + pvs[h]
    return tuple(m_new)


def _for_chunks(lo, hi, step, ms):
    odd = (hi - lo) % 2
    ms = lax.cond(odd == 1, lambda ms: step([lo], ms), lambda ms: ms, ms)
    first = lo + odd
    return lax.fori_loop(0, (hi - lo) // 2,
                         lambda t, ms: step([first + 2 * t, first + 2 * t + 1], ms), ms)


def _softmax_init(acc_ref):
    acc_ref[...] = jnp.zeros(acc_ref.shape, F32)
    return tuple(jnp.full((1, TQ), NEG, F32) for _ in range(N_HEADS))


def _softmax_result(acc_ref, h):
    return acc_ref[h, 0:HEAD_DIM, :] / acc_ref[h, HEAD_DIM:HEAD_DIM + 1, :]


def _dsa_kernel(qi_ref, ki_ref, wg_ref, q_ref, k_ref, vT_ref, tril_ref, o_ref, sc_ref, hi_ref, acc_ref,
                *, topk):
    i = pl.program_id(1)
    nchunks = i + 1
    npairs = (i + 2) // 2
    qpos = i * TQ + lax.broadcasted_iota(jnp.int32, (1, TQ), 1)
    rowi = lax.broadcasted_iota(jnp.int32, (TQ, TQ), 0)
    int_min = jnp.int32(-2 ** 31)

    qis = [qi_ref[0, hh] for hh in range(IDX_HEADS)]
    wT = wg_ref[0]

    def idx_chunks(js, carry):
        rs = [[_mm_nt(ki_ref[0, _chunk_slice(j), :], qh) for qh in qis] for j in js]
        for j, rj in zip(js, rs):
            acc = jnp.zeros((TQ, TQ), F32)
            for hh in range(IDX_HEADS):
                acc = acc + wT[hh:hh + 1, :] * jnp.maximum(rj[hh], 0.0)
            sc = jnp.where(j * TQ + rowi <= qpos, acc, -jnp.inf)
            sc_ref[_chunk_slice(j), :] = sc
            hi_ref[_chunk_slice(j), :] = high_half(sc)
        return carry

    def high_half(x):
        bits = lax.bitcast_convert_type(x, jnp.int32) & jnp.int32(-65536)
        return lax.bitcast_convert_type(bits, F32).astype(jnp.bfloat16)

    _for_chunks(0, nchunks, idx_chunks, 0)

    @pl.when(nchunks % 2 == 1)
    def _():
        sc_ref[_chunk_slice(nchunks), :] = jnp.full((TQ, TQ), -jnp.inf, F32)
        hi_ref[_chunk_slice(nchunks), :] = jnp.full((TQ, TQ), -jnp.inf, jnp.bfloat16)

    def count_high(thr_hi):
        one, zero = jnp.ones((), jnp.bfloat16), jnp.zeros((), jnp.bfloat16)

        def body(j, acc):
            for r in range(2):
                blk = hi_ref[pl.ds(pl.multiple_of(j * 2 * TQ + r * TQ, TQ), TQ), :]
                ind = jnp.where(blk >= thr_hi, one, zero)
                part = ind[0:PACKED_ROWS]
                for g in range(1, TQ // PACKED_ROWS):
                    part = part + ind[g * PACKED_ROWS:(g + 1) * PACKED_ROWS]
                acc = acc + part.astype(F32)
            return acc
        return jnp.sum(lax.fori_loop(0, npairs, body, jnp.zeros((PACKED_ROWS, TQ), F32)),
                       axis=0, keepdims=True)

    def count(pred):
        def body(j, acc):
            for r in range(2 * TQ // COUNT_ROWS):
                base = pl.multiple_of(j * 2 * TQ + r * COUNT_ROWS, COUNT_ROWS)
                acc = acc + _colsum8(jnp.where(pred(sc_ref[pl.ds(base, COUNT_ROWS), :]), 1.0, 0.0))
            return acc
        return jnp.sum(lax.fori_loop(0, npairs, body, jnp.zeros((SUBLANES, TQ), F32)),
                       axis=0, keepdims=True)

    take_all = qpos < topk

    def search():
        def key_to_float(u):
            s = u ^ int_min
            return lax.bitcast_convert_type(jnp.where(s < 0, s ^ jnp.int32(0x7FFFFFFF), s), F32)

        def bit_body(t, carry, packed):
            res, n_ge, n_gt = carry
            cand = res | jnp.left_shift(jnp.int32(1), 31 - t)
            thr = key_to_float(cand)
            n = count_high(high_half(thr)) if packed else count(lambda blk: blk >= thr)
            ok = n >= topk
            return jnp.where(ok, cand, res), jnp.where(ok, n, n_ge), jnp.where(ok, n_gt, n)

        zero = jnp.zeros((1, TQ), F32)
        carry = (jnp.zeros((1, TQ), jnp.int32), zero, zero)
        carry = lax.fori_loop(0, 16, functools.partial(bit_body, packed=True), carry)
        res, n_ge, n_gt = lax.fori_loop(16, 32, functools.partial(bit_body, packed=False), carry)
        thr = key_to_float(res)

        @pl.when(jnp.max(jnp.where(take_all, 0.0, n_ge)) > topk)
        def _():
            need = jnp.where(take_all, float(2 ** 30), topk - n_gt)
            tril = tril_ref[...]

            def drop(j, before):
                blk = sc_ref[_chunk_slice(j), :]
                tied = blk == thr
                rank = _mm(tril, jnp.where(tied, 1.0, 0.0)) + before
                sc_ref[_chunk_slice(j), :] = jnp.where(tied & (rank > need), -jnp.inf, blk)
                return rank[TQ - 1:TQ, :]

            lax.fori_loop(0, nchunks, drop, zero)

        return thr

    lowest = float(np.finfo(np.float32).min)
    thr = lax.cond(i * TQ + TQ > topk, search, lambda: jnp.full((1, TQ), lowest, F32))
    thr = jnp.where(take_all, lowest, thr)

    qs = [q_ref[0, h] for h in range(N_HEADS)]

    def att_chunks(js, ms):
        biases = [jnp.where(sc_ref[_chunk_slice(j), :] >= thr, 0.0, NEG) for j in js]
        return _softmax_step([k_ref[0, _chunk_slice(j), :] for j in js], qs, biases,
                             [vT_ref[0, j] for j in js], ms, acc_ref)

    _for_chunks(0, nchunks, att_chunks, _softmax_init(acc_ref))
    o_ref[0] = jnp.concatenate([_softmax_result(acc_ref, h) for h in range(N_HEADS)], axis=0).T


def _dsa_attention(qidx, kidx, wg, qb, kb, vbT, B, S):
    topk = min(DSA_TOPK, S // 4)
    assert topk <= TQ
    nc = S // TQ
    q_spec = pl.BlockSpec((1, N_HEADS, TQ, HEAD_DIM), lambda b, i: (b, 0, i, 0))
    k_spec = pl.BlockSpec((1, S, HEAD_DIM), lambda b, i: (b, 0, 0))
    tril = jnp.asarray(np.tril(np.ones((TQ, TQ), np.float32)), MXU_DTYPE)
    return pl.pallas_call(
        functools.partial(_dsa_kernel, topk=topk),
        grid=(B, nc),
        in_specs=[q_spec, k_spec, pl.BlockSpec((1, 16, TQ), lambda b, i: (b, 0, i)), q_spec, k_spec,
                  pl.BlockSpec((1, nc, V_ROWS, TQ), lambda b, i: (b, 0, 0, 0)), _const_spec((TQ, TQ))],
        out_specs=pl.BlockSpec((1, TQ, BRANCH_WIDTH), lambda b, i: (b, i, 0)),
        out_shape=jax.ShapeDtypeStruct((B, S, BRANCH_WIDTH), F32),
        scratch_shapes=[pltpu.VMEM(((nc + nc % 2) * TQ, TQ), F32),
                        pltpu.VMEM(((nc + nc % 2) * TQ, TQ), jnp.bfloat16),
                        pltpu.VMEM((N_HEADS, V_ROWS, TQ), F32)],
        compiler_params=_params("parallel", "arbitrary"),
        name="dsa_attention",
    )(qidx, kidx, wg, qb, kb, vbT, tril)


def _compress_kernel(c_ref, pos_ref, w1_ref, w2_ref, o_ref, *, transpose_out):
    n = c_ref.shape[1] // CMP_STRIDE
    top = jnp.zeros((n, CMP_HIDDEN), F32)
    bot = jnp.zeros((n, CMP_HIDDEN), F32)
    for p in range(CMP_STRIDE):
        x = c_ref[0, pl.ds(p, n, stride=CMP_STRIDE), :]
        q = p + CMP_STRIDE
        top = top + _mm(x + pos_ref[p:p + 1, :], w1_ref[p * HEAD_DIM:(p + 1) * HEAD_DIM, :])
        bot = bot + _mm(x + pos_ref[q:q + 1, :], w1_ref[q * HEAD_DIM:(q + 1) * HEAD_DIM, :])
    hid = jax.nn.gelu(top + pltpu.roll(bot, n - 1, 0), approximate=True)
    out = _mm(hid, w2_ref[...])
    rowi = lax.broadcasted_iota(jnp.int32, out.shape, 0)
    out = jnp.where(rowi < n - 1, out, 0.0)
    if transpose_out:
        out = jnp.concatenate([out, jnp.zeros_like(out)], axis=1).T[0:HEAD_DIM, :]
    o_ref[0] = out.astype(o_ref.dtype)


def _compress(tok, pos_emb, w1, w2, transpose_out):
    B, S, _ = tok.shape
    assert CMP_LEN == 2 * CMP_STRIDE
    nch = S // CMP_STRIDE
    out_block = (1, HEAD_DIM, nch) if transpose_out else (1, nch, HEAD_DIM)
    return pl.pallas_call(
        functools.partial(_compress_kernel, transpose_out=transpose_out),
        grid=(B,),
        in_specs=[pl.BlockSpec((1, S, HEAD_DIM), lambda b: (b, 0, 0)),
                  _const_spec(pos_emb.shape), _const_spec(w1.shape), _const_spec(w2.shape)],
        out_specs=pl.BlockSpec(out_block, lambda b: (b, 0, 0)),
        out_shape=jax.ShapeDtypeStruct((B,) + out_block[1:], MXU_DTYPE),
        compiler_params=_params("parallel"),
        name="nsa_compress",
    )(tok, pos_emb, w1.astype(MXU_DTYPE), w2.astype(MXU_DTYPE))


def _nsa_kernel(q_ref, wg_ref, kc_ref, vcT_ref, ovT_ref, ks_ref, vsT_ref, kw_ref, vwT_ref,
                o_ref, selb_ref, acc_s_ref, acc_w_ref, *, n_cmp, n_sel, topn):
    i = pl.program_id(1)
    qpos = i * TQ + lax.broadcasted_iota(jnp.int32, (1, TQ), 1)
    rowi = lax.broadcasted_iota(jnp.int32, (TQ, TQ), 0)
    heads = [q_ref[0, h] for h in range(N_HEADS)]
    gates = wg_ref[0]

    ncp = kc_ref.shape[1]
    cidx = lax.broadcasted_iota(jnp.int32, (ncp, TQ), 0)
    cmask = (cidx * CMP_STRIDE + (CMP_LEN - 1) <= qpos) & (cidx < n_cmp)
    kc, vcT, ovT = kc_ref[0], vcT_ref[0], ovT_ref[...]
    ss = [jnp.where(cmask, _mm_nt(kc, q), -jnp.inf) for q in heads]
    ps = []
    for s in ss:
        m = jnp.max(s, axis=0, keepdims=True)
        m = jnp.where(m > -jnp.inf, m, 0.0)
        e = jnp.where(cmask, jnp.exp2(s - m), 0.0)
        ps.append(e / jnp.maximum(jnp.sum(e, axis=0, keepdims=True), 1e-30))
    o_cmp = [_mm(vcT, p) for p in ps]
    imp = jnp.zeros((n_sel, TQ), F32)
    for p in ps:
        p_hi = p.astype(MXU_DTYPE)
        imp = imp + _mm(ovT, p_hi) + _mm(ovT, p - p_hi.astype(F32))

    bidx = lax.broadcasted_iota(jnp.int32, (n_sel, TQ), 0)
    cur = qpos // SEL_LEN
    forced = (bidx == 0) | (bidx == cur) | (bidx == cur - 1)
    work = jnp.where(bidx * SEL_LEN <= qpos, jnp.where(forced, FORCED_SCORE, imp), -jnp.inf)
    sel = jnp.zeros((n_sel, TQ), F32)
    for _ in range(topn):
        m = jnp.max(work, axis=0, keepdims=True)
        first = jnp.min(jnp.where((work == m) & (m > -jnp.inf), bidx, n_sel), axis=0, keepdims=True)
        pick = bidx == first
        sel = jnp.where(pick, 1.0, sel)
        work = jnp.where(pick, -jnp.inf, work)
    selb_ref[...] = jnp.where(sel > 0.0, 0.0, NEG)

    per_chunk = TQ // SEL_LEN

    def sel_bias(j):
        rows = [jnp.broadcast_to(selb_ref[pl.ds(j * per_chunk + bb, 1), :], (SEL_LEN, TQ))
                for bb in range(per_chunk)]
        return jnp.where(j * TQ + rowi <= qpos, jnp.concatenate(rows, axis=0), NEG)

    def sel_chunks(js, ms):
        return _softmax_step([ks_ref[0, _chunk_slice(j), :] for j in js], heads,
                             [sel_bias(j) for j in js], [vsT_ref[0, j] for j in js], ms, acc_s_ref)

    _for_chunks(0, i + 1, sel_chunks, _softmax_init(acc_s_ref))

    def win_bias(j):
        diff = qpos - (j * TQ + rowi)
        return jnp.where((diff >= 0) & (diff < WINDOW), 0.0, NEG)

    def win_chunks(js, ms):
        return _softmax_step([kw_ref[0, _chunk_slice(j), :] for j in js], heads,
                             [win_bias(j) for j in js], [vwT_ref[0, j] for j in js], ms, acc_w_ref)

    _for_chunks(jnp.maximum(i - WINDOW // TQ, 0), i + 1, win_chunks, _softmax_init(acc_w_ref))

    outs = []
    for h in range(N_HEADS):
        g = lambda br: gates[IDX_HEADS + 3 * h + br:IDX_HEADS + 3 * h + br + 1, :]
        outs.append(g(0) * o_cmp[h] + g(1) * _softmax_result(acc_s_ref, h)
                    + g(2) * _softmax_result(acc_w_ref, h))
    o_ref[0] = jnp.concatenate(outs, axis=0).T


def _nsa_attention(qc, wg, kc, vcT, ksel, vselT, kwin, vwinT, B, S):
    n_cmp = (S - CMP_LEN) // CMP_STRIDE + 1
    n_sel = S // SEL_LEN
    topn = min(SEL_TOPN, n_sel)
    ncp = kc.shape[1]
    cs = np.arange(ncp) * CMP_STRIDE
    ss = np.arange(n_sel) * SEL_LEN
    overlapT = ((cs[None, :] < ss[:, None] + SEL_LEN) & (cs[None, :] + CMP_LEN > ss[:, None])
                & (np.arange(ncp)[None, :] < n_cmp)).astype(np.float32)
    k_spec = pl.BlockSpec((1, S, HEAD_DIM), lambda b, i: (b, 0, 0))
    vT_spec = pl.BlockSpec((1, S // TQ, V_ROWS, TQ), lambda b, i: (b, 0, 0, 0))
    return pl.pallas_call(
        functools.partial(_nsa_kernel, n_cmp=n_cmp, n_sel=n_sel, topn=topn),
        grid=(B, S // TQ),
        in_specs=[pl.BlockSpec((1, N_HEADS, TQ, HEAD_DIM), lambda b, i: (b, 0, i, 0)),
                  pl.BlockSpec((1, 16, TQ), lambda b, i: (b, 0, i)),
                  pl.BlockSpec((1, ncp, HEAD_DIM), lambda b, i: (b, 0, 0)),
                  pl.BlockSpec((1, HEAD_DIM, ncp), lambda b, i: (b, 0, 0)),
                  _const_spec((n_sel, ncp)), k_spec, vT_spec, k_spec, vT_spec],
        out_specs=pl.BlockSpec((1, TQ, BRANCH_WIDTH), lambda b, i: (b, i, 0)),
        out_shape=jax.ShapeDtypeStruct((B, S, BRANCH_WIDTH), F32),
        scratch_shapes=[pltpu.VMEM((n_sel, TQ), F32), pltpu.VMEM((N_HEADS, V_ROWS, TQ), F32),
                        pltpu.VMEM((N_HEADS, V_ROWS, TQ), F32)],
        compiler_params=_params("parallel", "arbitrary"),
        name="nsa_attention",
    )(qc, wg, kc, vcT, jnp.asarray(overlapT, MXU_DTYPE), ksel, vselT, kwin, vwinT)


def _merge_kernel(h_ref, g_ref, oa_ref, ob_ref, oc_ref, wg_ref, wup_ref, wout_ref, o_ref):
    x = h_ref[...]
    xn = _rmsnorm(x, g_ref[...]).astype(MXU_DTYPE)
    merged = jnp.zeros((TM, D_MODEL), F32)
    for j, br_ref in enumerate((oa_ref, ob_ref, oc_ref)):
        gate = jax.nn.sigmoid(_mm(xn, wg_ref[:, j * D_MODEL:(j + 1) * D_MODEL]))
        merged = merged + gate * _mm(br_ref[...], wup_ref[j])
    o_ref[...] = x + _mm(merged, wout_ref[...])


def _merge(h, g, o_a, o_b, o_c, w_gm, w_up, w_out):
    T = h.shape[0]
    row = lambda w: pl.BlockSpec((TM, w), lambda i: (i, 0))
    return pl.pallas_call(
        _merge_kernel,
        grid=(T // TM,),
        in_specs=[row(D_MODEL), _const_spec((1, D_MODEL)), row(BRANCH_WIDTH), row(BRANCH_WIDTH),
                  row(BRANCH_WIDTH), _const_spec(w_gm.shape), _const_spec(w_up.shape),
                  _const_spec(w_out.shape)],
        out_specs=row(D_MODEL),
        out_shape=jax.ShapeDtypeStruct((T, D_MODEL), F32),
        compiler_params=_params("parallel"),
        name="merge",
    )(h, g.reshape(1, D_MODEL), o_a, o_b, o_c, w_gm.astype(MXU_DTYPE), w_up.astype(MXU_DTYPE),
      w_out.astype(MXU_DTYPE))


def _norm_matmul_kernel(x_ref, g_ref, w_ref, o_ref):
    o_ref[...] = _mm(_rmsnorm(x_ref[...], g_ref[...]), w_ref[...]).astype(o_ref.dtype)


def _norm_matmul(x, g, w, tm):
    T, n = x.shape[0], w.shape[1]
    return pl.pallas_call(
        _norm_matmul_kernel,
        grid=(T // tm,),
        in_specs=[pl.BlockSpec((tm, D_MODEL), lambda i: (i, 0)), _const_spec((1, D_MODEL)),
                  _const_spec(w.shape)],
        out_specs=pl.BlockSpec((tm, n), lambda i: (i, 0)),
        out_shape=jax.ShapeDtypeStruct((T, n), MXU_DTYPE),
        compiler_params=_params("parallel"),
        name="norm_matmul",
    )(x, g.reshape(1, D_MODEL), w.astype(MXU_DTYPE))


def _mem_attn_kernel(h_ref, g_ref, wq_ref, kv_ref, wo_ref, o_ref):
    x = h_ref[...]
    q = (_mm(_rmsnorm(x, g_ref[...]), wq_ref[...]) * Q_SCALE).astype(MXU_DTYPE)
    kv = kv_ref[0]
    out = x
    for h in range(N_HEADS):
        lo, hi = h * HEAD_DIM, (h + 1) * HEAD_DIM
        s = _mm_nt(q[:, lo:hi], kv[:, lo:hi])
        e = jnp.exp(s - jnp.max(s, axis=-1, keepdims=True))
        p = e / jnp.sum(e, axis=-1, keepdims=True)
        o_h = _mm(p, kv[:, BRANCH_WIDTH + lo:BRANCH_WIDTH + hi])
        out = out + _mm(o_h, wo_ref[lo:hi, :])
    o_ref[...] = out


def _memory_attention(h, kv, g_q, w_q, w_o, S):
    T = h.shape[0]
    M = kv.shape[1]
    row = pl.BlockSpec((TM, D_MODEL), lambda i: (i, 0))
    nb = S // TM
    return pl.pallas_call(
        _mem_attn_kernel,
        grid=(T // TM,),
        in_specs=[row, _const_spec((1, D_MODEL)), _const_spec(w_q.shape),
                  pl.BlockSpec((1, M, 2 * BRANCH_WIDTH), lambda i: (i // nb, 0, 0)),
                  _const_spec(w_o.shape)],
        out_specs=row,
        out_shape=jax.ShapeDtypeStruct((T, D_MODEL), F32),
        compiler_params=_params("parallel"),
        name="memory_attention",
    )(h, g_q.reshape(1, D_MODEL), w_q.astype(MXU_DTYPE), kv, w_o.astype(MXU_DTYPE))


FF_CHUNK = 256


def _swiglu_kernel(h_ref, g_ref, win_ref, wout_ref, gf_ref, o_ref, *, final_norm):
    x = h_ref[...]
    xn = _rmsnorm(x, g_ref[...]).astype(MXU_DTYPE)
    out = x
    for c in range(D_FF // FF_CHUNK):
        lo, hi = c * FF_CHUNK, (c + 1) * FF_CHUNK
        a = _mm(xn, win_ref[:, lo:hi])
        b = _mm(xn, win_ref[:, D_FF + lo:D_FF + hi])
        out = out + _mm(a * jax.nn.sigmoid(a) * b, wout_ref[lo:hi, :])
    if final_norm:
        out = _rmsnorm(out, gf_ref[...])
    o_ref[...] = out


def _swiglu(h, g, w_in, w_out, g_final, final_norm):
    T = h.shape[0]
    row = pl.BlockSpec((TM, D_MODEL), lambda i: (i, 0))
    return pl.pallas_call(
        functools.partial(_swiglu_kernel, final_norm=final_norm),
        grid=(T // TM,),
        in_specs=[row, _const_spec((1, D_MODEL)), _const_spec(w_in.shape), _const_spec(w_out.shape),
                  _const_spec((1, D_MODEL))],
        out_specs=row,
        out_shape=jax.ShapeDtypeStruct((T, D_MODEL), F32),
        compiler_params=_params("parallel"),
        name="swiglu",
    )(h, g.reshape(1, D_MODEL), w_in.astype(MXU_DTYPE), w_out.astype(MXU_DTYPE),
      g_final.reshape(1, D_MODEL))


def _rope_tables(S):
    inv = ROPE_THETA ** (-jnp.arange(0, HEAD_DIM, 2, dtype=F32) / HEAD_DIM)
    ang = jnp.arange(S, dtype=F32)[:, None] * inv[None, :]
    cos, sin = jnp.cos(ang), jnp.sin(ang)
    return (jnp.concatenate([cos, cos, cos, cos], axis=1),
            jnp.concatenate([-sin, sin, -sin, sin], axis=1))


def _hybrid_mixer(h, B, S, g, w_in, cmp_k, cmp_v, w_up, w_out, cos_t, sin_t):
    T = B * S
    (qa, ka, vaT, qb, qidx, qc, kb, kidx, ksel, kwin, vbT, vselT, vwinT, wg, ck, cv) = _in_projection(
        h, g, w_in, cos_t, sin_t, B, S)
    o_a = _stick_breaking(qa, ka, vaT)
    o_b = _dsa_attention(qidx, kidx, wg, qb, kb, vbT, B, S)
    kc = _compress(ck, *cmp_k, transpose_out=False)
    vcT = _compress(cv, *cmp_v, transpose_out=True)
    o_c = _nsa_attention(qc, wg, kc, vcT, ksel, vselT, kwin, vwinT, B, S)

    w_gm = w_in[:, _OFF['g_merge'][0]:]
    return _merge(h, g, o_a.reshape(T, 256), o_b.reshape(T, 256), o_c.reshape(T, 256), w_gm, w_up, w_out)


def kernel(x, mem, norm_mix, w_in, cmp_pos_k, cmp_w1_k, cmp_w2_k, cmp_pos_v, cmp_w1_v, cmp_w2_v,
           w_up, w_out, norm_mem_q, norm_mem_kv, w_mem_q, w_mem_kv, w_mem_o,
           norm_ffn, w_ffn_in, w_ffn_out, norm_final):
    B, S, D = x.shape
    M = mem.shape[1]
    depth = w_in.shape[0]
    assert D == D_MODEL and S % TM == 0 and S % TQ == 0 and WINDOW % TQ == 0
    cos_t, sin_t = _rope_tables(S)
    h = x.reshape(B * S, D)
    mem2 = mem.reshape(B * M, D)
    for l in range(depth):
        h = _hybrid_mixer(h, B, S, norm_mix[l], w_in[l],
                          (cmp_pos_k[l], cmp_w1_k[l], cmp_w2_k[l]),
                          (cmp_pos_v[l], cmp_w1_v[l], cmp_w2_v[l]), w_up[l], w_out[l], cos_t, sin_t)
        kv = _norm_matmul(mem2, norm_mem_kv[l], w_mem_kv[l], min(TM, B * M))
        h = _memory_attention(h, kv.reshape(B, M, 2 * BRANCH_WIDTH), norm_mem_q[l], w_mem_q[l],
                              w_mem_o[l], S)
        h = _swiglu(h, norm_ffn[l], w_ffn_in[l], w_ffn_out[l], norm_final, l == depth - 1)
    return h.reshape(B, S, D)
```

```python
import functools

import jax
import jax.numpy as jnp
import numpy as np
from jax import lax
from jax.experimental import pallas as pl
from jax.experimental.pallas import tpu as pltpu

F32 = jnp.float32
MXU_DTYPE = jnp.bfloat16

D_MODEL = 1024
HEAD_DIM = 64
N_HEADS = 4
BRANCH_WIDTH = N_HEADS * HEAD_DIM
IDX_HEADS = 4
DSA_TOPK = 256
CMP_LEN = 32
CMP_STRIDE = 16
CMP_HIDDEN = 128
SEL_LEN = 64
SEL_TOPN = 16
WINDOW = 512
D_FF = 2816
ROPE_THETA = 10000.0
RMS_EPS = 1e-6
FORCED_SCORE = 1e4
Q_SCALE = HEAD_DIM ** -0.5
LOG2E = 1.4426950408889634
SOFTMAX_Q_SCALE = Q_SCALE * LOG2E
W_IDX_SCALE = (IDX_HEADS * HEAD_DIM) ** -0.5
NEG = -1e30

LANES = 128
SUBLANES = 8
VMEM_LIMIT = 56 * 1024 * 1024
TM = 512
TQ = 256

_OFF = {}
_o = 0
for _n, _w in (('q_a', 256), ('k_a', 256), ('v_a', 256), ('q_b', 256), ('k_b', 64), ('v_b', 64),
               ('q_idx', 256), ('k_idx', 64), ('w_idx', 4), ('q_c', 256), ('k_cmp', 64),
               ('v_cmp', 64), ('k_sel', 64), ('v_sel', 64), ('k_win', 64), ('v_win', 64),
               ('g_c', 12), ('g_merge', 3 * D_MODEL)):
    _OFF[_n] = (_o, _o + _w)
    _o += _w
D_IN = _o


def _mm(a, b):
    return jnp.dot(a.astype(MXU_DTYPE), b.astype(MXU_DTYPE), preferred_element_type=F32)


def _mm_nt(a, b):
    return lax.dot_general(a.astype(MXU_DTYPE), b.astype(MXU_DTYPE),
                           (((1,), (1,)), ((), ())), preferred_element_type=F32)


def _params(*sem):
    return pltpu.CompilerParams(dimension_semantics=sem, vmem_limit_bytes=VMEM_LIMIT)


def _const_spec(shape):
    nd = len(shape)
    return pl.BlockSpec(shape, lambda *_: (0,) * nd, pipeline_mode=pl.Buffered(1))


def _rmsnorm(x, g):
    return x * lax.rsqrt(jnp.mean(x * x, axis=-1, keepdims=True) + RMS_EPS) * g


def _rope_lanes(x, cos, sin):
    lane = lax.broadcasted_iota(jnp.int32, (1, LANES), 1)
    first_half = (lane % HEAD_DIM) < (HEAD_DIM // 2)
    out = []
    for g in range(x.shape[1] // LANES):
        xg = x[:, g * LANES:(g + 1) * LANES]
        rot = jnp.where(first_half, pltpu.roll(xg, LANES - HEAD_DIM // 2, 1),
                        pltpu.roll(xg, HEAD_DIM // 2, 1))
        out.append(xg * cos + rot * sin)
    return out


WG_ROW = BRANCH_WIDTH + 3 * HEAD_DIM
V_ROWS = HEAD_DIM + 2 * SUBLANES


def _inproj_kernel(h_ref, g_ref, cos_ref, sin_ref, wa_ref, wr_ref, wc_ref, wt_ref,
                   qa_ref, ka_ref, vaT_ref, qb_ref, qi_ref, qc_ref, kb_ref, ki_ref, ks_ref, kw_ref,
                   vbT_ref, vsT_ref, vwT_ref, wg_ref, ck_ref, cv_ref):
    xn = _rmsnorm(h_ref[...], g_ref[...]).astype(MXU_DTYPE)
    cos, sin = cos_ref[...], sin_ref[...]
    head = lambda y, h: y[:, h * HEAD_DIM:(h + 1) * HEAD_DIM]

    ya = _mm(xn, wa_ref[...])
    for h in range(N_HEADS):
        qa_ref[0, h] = (head(ya, h) * Q_SCALE).astype(qa_ref.dtype)
        ka_ref[0, h] = head(ya, N_HEADS + h).astype(ka_ref.dtype)

    yr = _rope_lanes(_mm(xn, wr_ref[...]), cos, sin)
    for g, (q_ref, scale) in enumerate(((qb_ref, SOFTMAX_Q_SCALE), (qi_ref, 1.0),
                                        (qc_ref, SOFTMAX_Q_SCALE))):
        for half in range(2):
            og = yr[2 * g + half] * scale
            q_ref[0, 2 * half] = head(og, 0).astype(q_ref.dtype)
            q_ref[0, 2 * half + 1] = head(og, 1).astype(q_ref.dtype)
    kb_ref[0] = head(yr[6], 0).astype(kb_ref.dtype)
    ki_ref[0] = head(yr[6], 1).astype(ki_ref.dtype)
    ks_ref[0] = head(yr[7], 0).astype(ks_ref.dtype)
    kw_ref[0] = head(yr[7], 1).astype(kw_ref.dtype)

    yc = _mm(xn, wc_ref[...])
    ck_ref[0] = head(_rope_lanes(yc, cos, sin)[0], 0)
    cv_ref[0] = head(yc, 1)

    yT = _mm_nt(wt_ref[...], xn)
    ones_row = (lax.broadcasted_iota(jnp.int32, (V_ROWS - HEAD_DIM, TQ), 0) == 0).astype(vbT_ref.dtype)
    for c in range(TM // TQ):
        cols = slice(c * TQ, (c + 1) * TQ)
        for h in range(N_HEADS):
            vaT_ref[0, h, c] = yT[h * HEAD_DIM:(h + 1) * HEAD_DIM, cols].astype(vaT_ref.dtype)
        for n, v_ref in enumerate((vbT_ref, vsT_ref, vwT_ref)):
            lo = BRANCH_WIDTH + n * HEAD_DIM
            v_ref[0, c, 0:HEAD_DIM, :] = yT[lo:lo + HEAD_DIM, cols].astype(v_ref.dtype)
            v_ref[0, c, HEAD_DIM:, :] = ones_row
    ys = yT[WG_ROW:WG_ROW + 16, :]
    row = lax.broadcasted_iota(jnp.int32, ys.shape, 0)
    wg_ref[0] = jnp.where(row < IDX_HEADS, ys * W_IDX_SCALE, jax.nn.sigmoid(ys))


def _in_projection(h, g, w_in, cos_t, sin_t, B, S):
    cols = lambda *names: jnp.concatenate([w_in[:, _OFF[n][0]:_OFF[n][1]] for n in names], axis=1)
    wa = cols('q_a', 'k_a').astype(MXU_DTYPE)
    wr = cols('q_b', 'q_idx', 'q_c', 'k_b', 'k_idx', 'k_sel', 'k_win').astype(MXU_DTYPE)
    wc = cols('k_cmp', 'v_cmp').astype(MXU_DTYPE)
    wt = cols('v_a', 'v_b', 'v_sel', 'v_win', 'w_idx', 'g_c')
    wt = jnp.pad(wt, ((0, 0), (0, 512 - wt.shape[1]))).T.astype(MXU_DTYPE)
    nb, nc, cpb = S // TM, S // TQ, TM // TQ
    bf = lambda *shape: jax.ShapeDtypeStruct(shape, MXU_DTYPE)
    f32 = lambda *shape: jax.ShapeDtypeStruct(shape, F32)
    heads_spec = pl.BlockSpec((1, N_HEADS, TM, HEAD_DIM), lambda i: (i // nb, 0, i % nb, 0))
    tok_spec = pl.BlockSpec((1, TM, HEAD_DIM), lambda i: (i // nb, i % nb, 0))
    vT_spec = pl.BlockSpec((1, cpb, V_ROWS, TQ), lambda i: (i // nb, i % nb, 0, 0))
    pos = pl.BlockSpec((TM, LANES), lambda i: (i % nb, 0))
    q4, k3, v4 = bf(B, N_HEADS, S, HEAD_DIM), bf(B, S, HEAD_DIM), bf(B, nc, V_ROWS, TQ)
    return pl.pallas_call(
        _inproj_kernel,
        grid=(B * nb,),
        in_specs=[pl.BlockSpec((TM, D_MODEL), lambda i: (i, 0)), _const_spec((1, D_MODEL)), pos, pos,
                  _const_spec(wa.shape), _const_spec(wr.shape), _const_spec(wc.shape),
                  _const_spec(wt.shape)],
        out_specs=[heads_spec, heads_spec,
                   pl.BlockSpec((1, N_HEADS, cpb, HEAD_DIM, TQ), lambda i: (i // nb, 0, i % nb, 0, 0)),
                   heads_spec, heads_spec, heads_spec, tok_spec, tok_spec, tok_spec, tok_spec,
                   vT_spec, vT_spec, vT_spec,
                   pl.BlockSpec((1, 16, TM), lambda i: (i // nb, 0, i % nb)), tok_spec, tok_spec],
        out_shape=[q4, q4, bf(B, N_HEADS, nc, HEAD_DIM, TQ), q4, q4, q4, k3, k3, k3, k3, v4, v4, v4,
                   f32(B, 16, S), f32(B, S, HEAD_DIM), f32(B, S, HEAD_DIM)],
        compiler_params=_params("parallel"),
        name="in_projection",
    )(h, g.reshape(1, D_MODEL), cos_t, sin_t, wa, wr, wc, wt)


def _chunk_slice(j):
    return pl.ds(pl.multiple_of(j * TQ, TQ), TQ)


def _sb_kernel(q_ref, k_ref, vT_ref, tri_ref, o_ref, acc_ref):
    i = pl.program_id(1)
    row = lax.broadcasted_iota(jnp.int32, (TQ, TQ), 0)
    col = lax.broadcasted_iota(jnp.int32, (TQ, TQ), 1)
    strict = row < col
    tri = tri_ref[...]
    qs = [q_ref[0, h] for h in range(N_HEADS)]
    acc_ref[...] = jnp.zeros(acc_ref.shape, F32)

    def chunks(js, cs, diag=False):
        heads = range(N_HEADS)
        zs = [[_mm_nt(k_ref[0, h, _chunk_slice(j), :], qs[h]) for h in heads] for j in js]
        cums = []
        for zj in zs:
            cums.append([])
            for z in zj:
                sp = jnp.maximum(z, 0.0) + jnp.log(1.0 + jnp.exp(-jnp.abs(z)))
                if diag:
                    sp = jnp.where(strict, sp, 0.0)
                hi = sp.astype(MXU_DTYPE)
                lo = (sp - hi.astype(F32)).astype(MXU_DTYPE)
                cums[-1].append(_mm(tri, hi) + _mm(tri, lo))
        outs = [jnp.zeros((HEAD_DIM, TQ), F32) for _ in heads]
        cs = list(cs)
        for n, j in enumerate(js):
            for h in heads:
                w = jnp.exp(zs[n][h] - cums[n][h] - cs[h])
                if diag:
                    w = jnp.where(strict, w, 0.0)
                outs[h] = outs[h] + _mm(vT_ref[0, h, j], w)
                cs[h] = cs[h] + cums[n][h][0:1, :]
        for h in heads:
            acc_ref[h] += outs[h]
        return tuple(cs)

    cs = chunks([i], tuple(jnp.zeros((1, TQ), F32) for _ in range(N_HEADS)), diag=True)
    odd = i % 2
    cs = lax.cond(odd == 1, lambda cs: chunks([i - 1], cs), lambda cs: cs, cs)
    top = i - 1 - odd
    lax.fori_loop(0, i // 2, lambda t, cs: chunks([top - 2 * t, top - 2 * t - 1], cs), cs)
    o_ref[0] = acc_ref[...].reshape(BRANCH_WIDTH, TQ).T


def _stick_breaking(q, k, vT):
    B, H, S, _ = q.shape
    tri = jnp.asarray(np.triu(np.ones((TQ, TQ), np.float32)), MXU_DTYPE)
    return pl.pallas_call(
        _sb_kernel,
        grid=(B, S // TQ),
        in_specs=[pl.BlockSpec((1, H, TQ, HEAD_DIM), lambda b, i: (b, 0, i, 0)),
                  pl.BlockSpec((1, H, S, HEAD_DIM), lambda b, i: (b, 0, 0, 0)),
                  pl.BlockSpec((1, H, S // TQ, HEAD_DIM, TQ), lambda b, i: (b, 0, 0, 0, 0)),
                  _const_spec((TQ, TQ))],
        out_specs=pl.BlockSpec((1, TQ, BRANCH_WIDTH), lambda b, i: (b, i, 0)),
        out_shape=jax.ShapeDtypeStruct((B, S, BRANCH_WIDTH), F32),
        scratch_shapes=[pltpu.VMEM((H, HEAD_DIM, TQ), F32)],
        compiler_params=_params("parallel", "arbitrary"),
        name="stick_breaking",
    )(q, k, vT, tri)


PACKED_ROWS = 2 * SUBLANES


def _softmax_step(ks, qs, biases, vTs, ms, acc_ref):
    heads = range(len(qs))
    ss = [[_mm_nt(k, q) + b for q in qs] for k, b in zip(ks, biases)]
    m_new = []
    for h in heads:
        m = ms[h]
        for st in ss:
            m = jnp.maximum(m, jnp.max(st[h], axis=0, keepdims=True))
        m_new.append(m)
    pvs = []
    for h in heads:
        pv = jnp.zeros((V_ROWS, TQ), F32)
        for st, vT in zip(ss, vTs):
            pv = pv + _mm(vT, jnp.exp2(st[h] - m_new[h]))
        pvs.append(pv)
    for h in heads:
        acc_ref[h] = jnp.exp2(ms[h] - m_new[h]) * acc_ref[h] + pvs[h]
    return tuple(m_new)


def _for_chunks(lo, hi, step, ms):
    odd = (hi - lo) % 2
    ms = lax.cond(odd == 1, lambda ms: step([lo], ms), lambda ms: ms, ms)
    first = lo + odd
    return lax.fori_loop(0, (hi - lo) // 2,
                         lambda t, ms: step([first + 2 * t, first + 2 * t + 1], ms), ms)


def _softmax_init(acc_ref):
    acc_ref[...] = jnp.zeros(acc_ref.shape, F32)
    return tuple(jnp.full((1, TQ), NEG, F32) for _ in range(N_HEADS))


def _softmax_result(acc_ref, h):
    return acc_ref[h, 0:HEAD_DIM, :] / acc_ref[h, HEAD_DIM:HEAD_DIM + 1, :]


def _dsa_kernel(qi_ref, ki_ref, wg_ref, q_ref, k_ref, vT_ref, tril_ref, o_ref, sc_ref, hi_ref, lo_ref,
                acc_ref, *, topk):
    i = pl.program_id(1)
    nchunks = i + 1
    npairs = (i + 2) // 2
    qpos = i * TQ + lax.broadcasted_iota(jnp.int32, (1, TQ), 1)
    rowi = lax.broadcasted_iota(jnp.int32, (TQ, TQ), 0)
    int_min = jnp.int32(-2 ** 31)

    qis = [qi_ref[0, hh] for hh in range(IDX_HEADS)]
    wT = wg_ref[0]

    def idx_chunks(js, carry):
        rs = [[_mm_nt(ki_ref[0, _chunk_slice(j), :], qh) for qh in qis] for j in js]
        for j, rj in zip(js, rs):
            acc = jnp.zeros((TQ, TQ), F32)
            for hh in range(IDX_HEADS):
                acc = acc + wT[hh:hh + 1, :] * jnp.maximum(rj[hh], 0.0)
            sc = jnp.where(j * TQ + rowi <= qpos, acc, -jnp.inf)
            sc_ref[_chunk_slice(j), :] = sc
            hi_ref[_chunk_slice(j), :] = high_half(sc)
            lo_ref[_chunk_slice(j), :] = low_half(float_to_key(sc))
        return carry

    def float_to_key(x):
        bits = lax.bitcast_convert_type(x, jnp.int32)
        return bits ^ (jnp.right_shift(bits, 31) & jnp.int32(0x7FFFFFFF))

    def low_half(key):
        return ((key & jnp.int32(0xFFFF)) - 32768).astype(jnp.int16)

    def high_half(x):
        bits = lax.bitcast_convert_type(x, jnp.int32) & jnp.int32(-65536)
        return lax.bitcast_convert_type(bits, F32).astype(jnp.bfloat16)

    _for_chunks(0, nchunks, idx_chunks, 0)

    @pl.when(nchunks % 2 == 1)
    def _():
        sc_ref[_chunk_slice(nchunks), :] = jnp.full((TQ, TQ), -jnp.inf, F32)
        hi_ref[_chunk_slice(nchunks), :] = jnp.full((TQ, TQ), -jnp.inf, jnp.bfloat16)
        lo_ref[_chunk_slice(nchunks), :] = jnp.full((TQ, TQ), -32768, jnp.int16)

    def count_packed(ref, thr16):
        one, zero = jnp.ones((), jnp.bfloat16), jnp.zeros((), jnp.bfloat16)

        def body(j, acc):
            for r in range(2):
                blk = ref[pl.ds(pl.multiple_of(j * 2 * TQ + r * TQ, TQ), TQ), :]
                ind = jnp.where(blk >= thr16, one, zero)
                part = ind[0:PACKED_ROWS]
                for g in range(1, TQ // PACKED_ROWS):
                    part = part + ind[g * PACKED_ROWS:(g + 1) * PACKED_ROWS]
                acc = acc + part.astype(F32)
            return acc
        return jnp.sum(lax.fori_loop(0, npairs, body, jnp.zeros((PACKED_ROWS, TQ), F32)),
                       axis=0, keepdims=True)

    take_all = qpos < topk

    def search():
        def key_to_float(u):
            s = u ^ int_min
            return lax.bitcast_convert_type(jnp.where(s < 0, s ^ jnp.int32(0x7FFFFFFF), s), F32)

        def bit_body(t, carry, count_ge):
            res, n_ge, n_gt = carry
            cand = res | jnp.left_shift(jnp.int32(1), 31 - t)
            n = count_ge(cand)
            ok = n >= topk
            return jnp.where(ok, cand, res), jnp.where(ok, n, n_ge), jnp.where(ok, n_gt, n)

        zero = jnp.zeros((1, TQ), F32)
        carry = (jnp.zeros((1, TQ), jnp.int32), zero, zero)
        res, n_ge, n_above = lax.fori_loop(
            0, 16, functools.partial(
                bit_body, count_ge=lambda cand: count_packed(hi_ref, high_half(key_to_float(cand)))), carry)

        bucket = high_half(key_to_float(res))

        def keep_bucket(j, _):
            sl = pl.ds(pl.multiple_of(j * 2 * TQ, 2 * TQ), 2 * TQ)
            lo_ref[sl, :] = jnp.where(hi_ref[sl, :] == bucket, lo_ref[sl, :], jnp.int16(-32768))
            return 0

        lax.fori_loop(0, npairs, keep_bucket, 0)
        res, n_ge, n_gt = lax.fori_loop(
            16, 32, functools.partial(
                bit_body, count_ge=lambda cand: n_above + count_packed(lo_ref, low_half(cand))),
            (res, n_ge, n_above))
        thr = key_to_float(res)

        @pl.when(jnp.max(jnp.where(take_all, 0.0, n_ge)) > topk)
        def _():
            need = jnp.where(take_all, float(2 ** 30), topk - n_gt)
            tril = tril_ref[...]

            def drop(j, before):
                blk = sc_ref[_chunk_slice(j), :]
                tied = blk == thr
                rank = _mm(tril, jnp.where(tied, 1.0, 0.0)) + before
                sc_ref[_chunk_slice(j), :] = jnp.where(tied & (rank > need), -jnp.inf, blk)
                return rank[TQ - 1:TQ, :]

            lax.fori_loop(0, nchunks, drop, zero)

        return thr

    lowest = float(np.finfo(np.float32).min)
    thr = lax.cond(i * TQ + TQ > topk, search, lambda: jnp.full((1, TQ), lowest, F32))
    thr = jnp.where(take_all, lowest, thr)

    qs = [q_ref[0, h] for h in range(N_HEADS)]

    def att_chunks(js, ms):
        biases = [jnp.where(sc_ref[_chunk_slice(j), :] >= thr, 0.0, NEG) for j in js]
        return _softmax_step([k_ref[0, _chunk_slice(j), :] for j in js], qs, biases,
                             [vT_ref[0, j] for j in js], ms, acc_ref)

    _for_chunks(0, nchunks, att_chunks, _softmax_init(acc_ref))
    o_ref[0] = jnp.concatenate([_softmax_result(acc_ref, h) for h in range(N_HEADS)], axis=0).T


def _dsa_attention(qidx, kidx, wg, qb, kb, vbT, B, S):
    topk = min(DSA_TOPK, S // 4)
    assert topk <= TQ
    nc = S // TQ
    q_spec = pl.BlockSpec((1, N_HEADS, TQ, HEAD_DIM), lambda b, i: (b, 0, i, 0))
    k_spec = pl.BlockSpec((1, S, HEAD_DIM), lambda b, i: (b, 0, 0))
    tril = jnp.asarray(np.tril(np.ones((TQ, TQ), np.float32)), MXU_DTYPE)
    return pl.pallas_call(
        functools.partial(_dsa_kernel, topk=topk),
        grid=(B, nc),
        in_specs=[q_spec, k_spec, pl.BlockSpec((1, 16, TQ), lambda b, i: (b, 0, i)), q_spec, k_spec,
                  pl.BlockSpec((1, nc, V_ROWS, TQ), lambda b, i: (b, 0, 0, 0)), _const_spec((TQ, TQ))],
        out_specs=pl.BlockSpec((1, TQ, BRANCH_WIDTH), lambda b, i: (b, i, 0)),
        out_shape=jax.ShapeDtypeStruct((B, S, BRANCH_WIDTH), F32),
        scratch_shapes=[pltpu.VMEM(((nc + nc % 2) * TQ, TQ), F32),
                        pltpu.VMEM(((nc + nc % 2) * TQ, TQ), jnp.bfloat16),
                        pltpu.VMEM(((nc + nc % 2) * TQ, TQ), jnp.int16),
                        pltpu.VMEM((N_HEADS, V_ROWS, TQ), F32)],
        compiler_params=_params("parallel", "arbitrary"),
        name="dsa_attention",
    )(qidx, kidx, wg, qb, kb, vbT, tril)


def _compress_kernel(c_ref, pos_ref, w1_ref, w2_ref, o_ref, *, transpose_out):
    n = c_ref.shape[1] // CMP_STRIDE
    top = jnp.zeros((n, CMP_HIDDEN), F32)
    bot = jnp.zeros((n, CMP_HIDDEN), F32)
    for p in range(CMP_STRIDE):
        x = c_ref[0, pl.ds(p, n, stride=CMP_STRIDE), :]
        q = p + CMP_STRIDE
        top = top + _mm(x + pos_ref[p:p + 1, :], w1_ref[p * HEAD_DIM:(p + 1) * HEAD_DIM, :])
        bot = bot + _mm(x + pos_ref[q:q + 1, :], w1_ref[q * HEAD_DIM:(q + 1) * HEAD_DIM, :])
    hid = jax.nn.gelu(top + pltpu.roll(bot, n - 1, 0), approximate=True)
    out = _mm(hid, w2_ref[...])
    rowi = lax.broadcasted_iota(jnp.int32, out.shape, 0)
    out = jnp.where(rowi < n - 1, out, 0.0)
    if transpose_out:
        out = jnp.concatenate([out, jnp.zeros_like(out)], axis=1).T[0:HEAD_DIM, :]
    o_ref[0] = out.astype(o_ref.dtype)


def _compress(tok, pos_emb, w1, w2, transpose_out):
    B, S, _ = tok.shape
    assert CMP_LEN == 2 * CMP_STRIDE
    nch = S // CMP_STRIDE
    out_block = (1, HEAD_DIM, nch) if transpose_out else (1, nch, HEAD_DIM)
    return pl.pallas_call(
        functools.partial(_compress_kernel, transpose_out=transpose_out),
        grid=(B,),
        in_specs=[pl.BlockSpec((1, S, HEAD_DIM), lambda b: (b, 0, 0)),
                  _const_spec(pos_emb.shape), _const_spec(w1.shape), _const_spec(w2.shape)],
        out_specs=pl.BlockSpec(out_block, lambda b: (b, 0, 0)),
        out_shape=jax.ShapeDtypeStruct((B,) + out_block[1:], MXU_DTYPE),
        compiler_params=_params("parallel"),
        name="nsa_compress",
    )(tok, pos_emb, w1.astype(MXU_DTYPE), w2.astype(MXU_DTYPE))


def _nsa_kernel(q_ref, wg_ref, kc_ref, vcT_ref, ovT_ref, ks_ref, vsT_ref, kw_ref, vwT_ref,
                o_ref, selb_ref, acc_s_ref, acc_w_ref, *, n_cmp, n_sel, topn):
    i = pl.program_id(1)
    qpos = i * TQ + lax.broadcasted_iota(jnp.int32, (1, TQ), 1)
    rowi = lax.broadcasted_iota(jnp.int32, (TQ, TQ), 0)
    heads = [q_ref[0, h] for h in range(N_HEADS)]
    gates = wg_ref[0]

    ncp = kc_ref.shape[1]
    cidx = lax.broadcasted_iota(jnp.int32, (ncp, TQ), 0)
    cmask = (cidx * CMP_STRIDE + (CMP_LEN - 1) <= qpos) & (cidx < n_cmp)
    kc, vcT, ovT = kc_ref[0], vcT_ref[0], ovT_ref[...]
    ss = [jnp.where(cmask, _mm_nt(kc, q), -jnp.inf) for q in heads]
    ps = []
    for s in ss:
        m = jnp.max(s, axis=0, keepdims=True)
        m = jnp.where(m > -jnp.inf, m, 0.0)
        e = jnp.where(cmask, jnp.exp2(s - m), 0.0)
        ps.append(e / jnp.maximum(jnp.sum(e, axis=0, keepdims=True), 1e-30))
    o_cmp = [_mm(vcT, p) for p in ps]
    imp = jnp.zeros((n_sel, TQ), F32)
    for p in ps:
        p_hi = p.astype(MXU_DTYPE)
        imp = imp + _mm(ovT, p_hi) + _mm(ovT, p - p_hi.astype(F32))

    bidx = lax.broadcasted_iota(jnp.int32, (n_sel, TQ), 0)
    cur = qpos // SEL_LEN
    forced = (bidx == 0) | (bidx == cur) | (bidx == cur - 1)
    work = jnp.where(bidx * SEL_LEN <= qpos, jnp.where(forced, FORCED_SCORE, imp), -jnp.inf)
    sel = jnp.zeros((n_sel, TQ), F32)
    for _ in range(topn):
        m = jnp.max(work, axis=0, keepdims=True)
        first = jnp.min(jnp.where((work == m) & (m > -jnp.inf), bidx, n_sel), axis=0, keepdims=True)
        pick = bidx == first
        sel = jnp.where(pick, 1.0, sel)
        work = jnp.where(pick, -jnp.inf, work)
    selb_ref[...] = jnp.where(sel > 0.0, 0.0, NEG)

    per_chunk = TQ // SEL_LEN

    def sel_bias(j):
        rows = [jnp.broadcast_to(selb_ref[pl.ds(j * per_chunk + bb, 1), :], (SEL_LEN, TQ))
                for bb in range(per_chunk)]
        return jnp.where(j * TQ + rowi <= qpos, jnp.concatenate(rows, axis=0), NEG)

    def sel_chunks(js, ms):
        return _softmax_step([ks_ref[0, _chunk_slice(j), :] for j in js], heads,
                             [sel_bias(j) for j in js], [vsT_ref[0, j] for j in js], ms, acc_s_ref)

    _for_chunks(0, i + 1, sel_chunks, _softmax_init(acc_s_ref))

    def win_bias(j):
        diff = qpos - (j * TQ + rowi)
        return jnp.where((diff >= 0) & (diff < WINDOW), 0.0, NEG)

    def win_chunks(js, ms):
        return _softmax_step([kw_ref[0, _chunk_slice(j), :] for j in js], heads,
                             [win_bias(j) for j in js], [vwT_ref[0, j] for j in js], ms, acc_w_ref)

    _for_chunks(jnp.maximum(i - WINDOW // TQ, 0), i + 1, win_chunks, _softmax_init(acc_w_ref))

    outs = []
    for h in range(N_HEADS):
        g = lambda br: gates[IDX_HEADS + 3 * h + br:IDX_HEADS + 3 * h + br + 1, :]
        outs.append(g(0) * o_cmp[h] + g(1) * _softmax_result(acc_s_ref, h)
                    + g(2) * _softmax_result(acc_w_ref, h))
    o_ref[0] = jnp.concatenate(outs, axis=0).T


def _nsa_attention(qc, wg, kc, vcT, ksel, vselT, kwin, vwinT, B, S):
    n_cmp = (S - CMP_LEN) // CMP_STRIDE + 1
    n_sel = S // SEL_LEN
    topn = min(SEL_TOPN, n_sel)
    ncp = kc.shape[1]
    cs = np.arange(ncp) * CMP_STRIDE
    ss = np.arange(n_sel) * SEL_LEN
    overlapT = ((cs[None, :] < ss[:, None] + SEL_LEN) & (cs[None, :] + CMP_LEN > ss[:, None])
                & (np.arange(ncp)[None, :] < n_cmp)).astype(np.float32)
    k_spec = pl.BlockSpec((1, S, HEAD_DIM), lambda b, i: (b, 0, 0))
    vT_spec = pl.BlockSpec((1, S // TQ, V_ROWS, TQ), lambda b, i: (b, 0, 0, 0))
    return pl.pallas_call(
        functools.partial(_nsa_kernel, n_cmp=n_cmp, n_sel=n_sel, topn=topn),
        grid=(B, S // TQ),
        in_specs=[pl.BlockSpec((1, N_HEADS, TQ, HEAD_DIM), lambda b, i: (b, 0, i, 0)),
                  pl.BlockSpec((1, 16, TQ), lambda b, i: (b, 0, i)),
                  pl.BlockSpec((1, ncp, HEAD_DIM), lambda b, i: (b, 0, 0)),
                  pl.BlockSpec((1, HEAD_DIM, ncp), lambda b, i: (b, 0, 0)),
                  _const_spec((n_sel, ncp)), k_spec, vT_spec, k_spec, vT_spec],
        out_specs=pl.BlockSpec((1, TQ, BRANCH_WIDTH), lambda b, i: (b, i, 0)),
        out_shape=jax.ShapeDtypeStruct((B, S, BRANCH_WIDTH), F32),
        scratch_shapes=[pltpu.VMEM((n_sel, TQ), F32), pltpu.VMEM((N_HEADS, V_ROWS, TQ), F32),
                        pltpu.VMEM((N_HEADS, V_ROWS, TQ), F32)],
        compiler_params=_params("parallel", "arbitrary"),
        name="nsa_attention",
    )(qc, wg, kc, vcT, jnp.asarray(overlapT, MXU_DTYPE), ksel, vselT, kwin, vwinT)


def _merge_kernel(h_ref, g_ref, oa_ref, ob_ref, oc_ref, wg_ref, wup_ref, wout_ref, o_ref):
    x = h_ref[...]
    xn = _rmsnorm(x, g_ref[...]).astype(MXU_DTYPE)
    merged = jnp.zeros((TM, D_MODEL), F32)
    for j, br_ref in enumerate((oa_ref, ob_ref, oc_ref)):
        gate = jax.nn.sigmoid(_mm(xn, wg_ref[:, j * D_MODEL:(j + 1) * D_MODEL]))
        merged = merged + gate * _mm(br_ref[...], wup_ref[j])
    o_ref[...] = x + _mm(merged, wout_ref[...])


def _merge(h, g, o_a, o_b, o_c, w_gm, w_up, w_out):
    T = h.shape[0]
    row = lambda w: pl.BlockSpec((TM, w), lambda i: (i, 0))
    return pl.pallas_call(
        _merge_kernel,
        grid=(T // TM,),
        in_specs=[row(D_MODEL), _const_spec((1, D_MODEL)), row(BRANCH_WIDTH), row(BRANCH_WIDTH),
                  row(BRANCH_WIDTH), _const_spec(w_gm.shape), _const_spec(w_up.shape),
                  _const_spec(w_out.shape)],
        out_specs=row(D_MODEL),
        out_shape=jax.ShapeDtypeStruct((T, D_MODEL), F32),
        compiler_params=_params("parallel"),
        name="merge",
    )(h, g.reshape(1, D_MODEL), o_a, o_b, o_c, w_gm.astype(MXU_DTYPE), w_up.astype(MXU_DTYPE),
      w_out.astype(MXU_DTYPE))


def _mem_kv_kernel(m_ref, g_ref, wk_ref, wvT_ref, ks_ref, vT_ref):
    xn = _rmsnorm(m_ref[0], g_ref[...]).astype(MXU_DTYPE)
    M = xn.shape[0]
    k = _mm(xn, wk_ref[...])
    lane_head = lax.broadcasted_iota(jnp.int32, (1, BRANCH_WIDTH), 1) // HEAD_DIM
    for h in range(N_HEADS):
        ks_ref[0, h * M:(h + 1) * M, :] = jnp.where(lane_head == h, k, 0.0).astype(ks_ref.dtype)
    vT_ref[0] = _mm_nt(wvT_ref[...], xn).astype(vT_ref.dtype)


def _mem_kv(mem, g, w_kv):
    B, M, _ = mem.shape
    wk = w_kv[:, :BRANCH_WIDTH].astype(MXU_DTYPE)
    wvT = w_kv[:, BRANCH_WIDTH:].T.astype(MXU_DTYPE)
    return pl.pallas_call(
        _mem_kv_kernel,
        grid=(B,),
        in_specs=[pl.BlockSpec((1, M, D_MODEL), lambda b: (b, 0, 0)), _const_spec((1, D_MODEL)),
                  _const_spec(wk.shape), _const_spec(wvT.shape)],
        out_specs=[pl.BlockSpec((1, N_HEADS * M, BRANCH_WIDTH), lambda b: (b, 0, 0)),
                   pl.BlockSpec((1, BRANCH_WIDTH, M), lambda b: (b, 0, 0))],
        out_shape=[jax.ShapeDtypeStruct((B, N_HEADS * M, BRANCH_WIDTH), MXU_DTYPE),
                   jax.ShapeDtypeStruct((B, BRANCH_WIDTH, M), MXU_DTYPE)],
        compiler_params=_params("parallel"),
        name="memory_kv",
    )(mem, g.reshape(1, D_MODEL), wk, wvT)


def _mem_attn_kernel(h_ref, g_ref, wq_ref, ks_ref, vT_ref, wo_ref, o_ref):
    x = h_ref[...]
    q = (_mm(_rmsnorm(x, g_ref[...]), wq_ref[...]) * SOFTMAX_Q_SCALE).astype(MXU_DTYPE)
    M = vT_ref.shape[2]
    s = _mm_nt(ks_ref[0], q)
    outs = []
    for h in range(N_HEADS):
        sh = s[h * M:(h + 1) * M, :]
        p = jnp.exp2(sh - jnp.max(sh, axis=0, keepdims=True))
        o_h = _mm(vT_ref[0, h * HEAD_DIM:(h + 1) * HEAD_DIM, :], p)
        outs.append(o_h / jnp.sum(p, axis=0, keepdims=True))
    o = jnp.concatenate(outs, axis=0).T
    o_ref[...] = x + _mm(o, wo_ref[...])


def _memory_attention(h, ks, vT, g_q, w_q, w_o, S):
    T = h.shape[0]
    row = pl.BlockSpec((TM, D_MODEL), lambda i: (i, 0))
    nb = S // TM
    return pl.pallas_call(
        _mem_attn_kernel,
        grid=(T // TM,),
        in_specs=[row, _const_spec((1, D_MODEL)), _const_spec(w_q.shape),
                  pl.BlockSpec((1,) + ks.shape[1:], lambda i: (i // nb, 0, 0)),
                  pl.BlockSpec((1,) + vT.shape[1:], lambda i: (i // nb, 0, 0)),
                  _const_spec(w_o.shape)],
        out_specs=row,
        out_shape=jax.ShapeDtypeStruct((T, D_MODEL), F32),
        compiler_params=_params("parallel"),
        name="memory_attention",
    )(h, g_q.reshape(1, D_MODEL), w_q.astype(MXU_DTYPE), ks, vT, w_o.astype(MXU_DTYPE))


FF_CHUNK = 256


def _swiglu_kernel(h_ref, g_ref, win_ref, wout_ref, gf_ref, o_ref, *, final_norm):
    x = h_ref[...]
    xn = _rmsnorm(x, g_ref[...]).astype(MXU_DTYPE)
    out = x
    for c in range(D_FF // FF_CHUNK):
        lo, hi = c * FF_CHUNK, (c + 1) * FF_CHUNK
        a = _mm(xn, win_ref[:, lo:hi])
        b = _mm(xn, win_ref[:, D_FF + lo:D_FF + hi])
        out = out + _mm(a * jax.nn.sigmoid(a) * b, wout_ref[lo:hi, :])
    if final_norm:
        out = _rmsnorm(out, gf_ref[...])
    o_ref[...] = out


def _swiglu(h, g, w_in, w_out, g_final, final_norm):
    T = h.shape[0]
    row = pl.BlockSpec((TM, D_MODEL), lambda i: (i, 0))
    return pl.pallas_call(
        functools.partial(_swiglu_kernel, final_norm=final_norm),
        grid=(T // TM,),
        in_specs=[row, _const_spec((1, D_MODEL)), _const_spec(w_in.shape), _const_spec(w_out.shape),
                  _const_spec((1, D_MODEL))],
        out_specs=row,
        out_shape=jax.ShapeDtypeStruct((T, D_MODEL), F32),
        compiler_params=_params("parallel"),
        name="swiglu",
    )(h, g.reshape(1, D_MODEL), w_in.astype(MXU_DTYPE), w_out.astype(MXU_DTYPE),
      g_final.reshape(1, D_MODEL))


def _rope_tables(S):
    inv = ROPE_THETA ** (-jnp.arange(0, HEAD_DIM, 2, dtype=F32) / HEAD_DIM)
    ang = jnp.arange(S, dtype=F32)[:, None] * inv[None, :]
    cos, sin = jnp.cos(ang), jnp.sin(ang)
    return (jnp.concatenate([cos, cos, cos, cos], axis=1),
            jnp.concatenate([-sin, sin, -sin, sin], axis=1))


def _hybrid_mixer(h, B, S, g, w_in, cmp_k, cmp_v, w_up, w_out, cos_t, sin_t):
    T = B * S
    (qa, ka, vaT, qb, qidx, qc, kb, kidx, ksel, kwin, vbT, vselT, vwinT, wg, ck, cv) = _in_projection(
        h, g, w_in, cos_t, sin_t, B, S)
    o_a = _stick_breaking(qa, ka, vaT)
    o_b = _dsa_attention(qidx, kidx, wg, qb, kb, vbT, B, S)
    kc = _compress(ck, *cmp_k, transpose_out=False)
    vcT = _compress(cv, *cmp_v, transpose_out=True)
    o_c = _nsa_attention(qc, wg, kc, vcT, ksel, vselT, kwin, vwinT, B, S)

    w_gm = w_in[:, _OFF['g_merge'][0]:]
    return _merge(h, g, o_a.reshape(T, 256), o_b.reshape(T, 256), o_c.reshape(T, 256), w_gm, w_up, w_out)


def kernel(x, mem, norm_mix, w_in, cmp_pos_k, cmp_w1_k, cmp_w2_k, cmp_pos_v, cmp_w1_v, cmp_w2_v,
           w_up, w_out, norm_mem_q, norm_mem_kv, w_mem_q, w_mem_kv, w_mem_o,
           norm_ffn, w_ffn_in, w_ffn_out, norm_final):
    B, S, D = x.shape
    depth = w_in.shape[0]
    assert D == D_MODEL and S % TM == 0 and S % TQ == 0 and WINDOW % TQ == 0
    cos_t, sin_t = _rope_tables(S)
    h = x.reshape(B * S, D)
    for l in range(depth):
        h = _hybrid_mixer(h, B, S, norm_mix[l], w_in[l],
                          (cmp_pos_k[l], cmp_w1_k[l], cmp_w2_k[l]),
                          (cmp_pos_v[l], cmp_w1_v[l], cmp_w2_v[l]), w_up[l], w_out[l], cos_t, sin_t)
        ks, vT = _mem_kv(mem, norm_mem_kv[l], w_mem_kv[l])
        h = _memory_attention(h, ks, vT, norm_mem_q[l], w_mem_q[l], w_mem_o[l], S)
        h = _swiglu(h, norm_ffn[l], w_ffn_in[l], w_ffn_out[l], norm_final, l == depth - 1)
    return h.reshape(B, S, D)
```

```python
import functools

import jax
import jax.numpy as jnp
import numpy as np
from jax import lax
from jax.experimental import pallas as pl
from jax.experimental.pallas import tpu as pltpu

F32 = jnp.float32
MXU_DTYPE = jnp.bfloat16

D_MODEL = 1024
HEAD_DIM = 64
N_HEADS = 4
BRANCH_WIDTH = N_HEADS * HEAD_DIM
IDX_HEADS = 4
DSA_TOPK = 256
CMP_LEN = 32
CMP_STRIDE = 16
CMP_HIDDEN = 128
SEL_LEN = 64
SEL_TOPN = 16
WINDOW = 512
D_FF = 2816
ROPE_THETA = 10000.0
RMS_EPS = 1e-6
FORCED_SCORE = 1e4
Q_SCALE = HEAD_DIM ** -0.5
LOG2E = 1.4426950408889634
SOFTMAX_Q_SCALE = Q_SCALE * LOG2E
W_IDX_SCALE = (IDX_HEADS * HEAD_DIM) ** -0.5
NEG = -1e30

LANES = 128
SUBLANES = 8
VMEM_LIMIT = 56 * 1024 * 1024
TM = 512
TQ = 256

_OFF = {}
_o = 0
for _n, _w in (('q_a', 256), ('k_a', 256), ('v_a', 256), ('q_b', 256), ('k_b', 64), ('v_b', 64),
               ('q_idx', 256), ('k_idx', 64), ('w_idx', 4), ('q_c', 256), ('k_cmp', 64),
               ('v_cmp', 64), ('k_sel', 64), ('v_sel', 64), ('k_win', 64), ('v_win', 64),
               ('g_c', 12), ('g_merge', 3 * D_MODEL)):
    _OFF[_n] = (_o, _o + _w)
    _o += _w
D_IN = _o


def _mm(a, b):
    return jnp.dot(a.astype(MXU_DTYPE), b.astype(MXU_DTYPE), preferred_element_type=F32)


def _mm_nt(a, b):
    return lax.dot_general(a.astype(MXU_DTYPE), b.astype(MXU_DTYPE),
                           (((1,), (1,)), ((), ())), preferred_element_type=F32)


def _params(*sem):
    return pltpu.CompilerParams(dimension_semantics=sem, vmem_limit_bytes=VMEM_LIMIT)


def _const_spec(shape):
    nd = len(shape)
    return pl.BlockSpec(shape, lambda *_: (0,) * nd, pipeline_mode=pl.Buffered(1))


def _rmsnorm(x, g):
    return x * lax.rsqrt(jnp.mean(x * x, axis=-1, keepdims=True) + RMS_EPS) * g


def _rope_lanes(x, cos, sin):
    lane = lax.broadcasted_iota(jnp.int32, (1, LANES), 1)
    first_half = (lane % HEAD_DIM) < (HEAD_DIM // 2)
    out = []
    for g in range(x.shape[1] // LANES):
        xg = x[:, g * LANES:(g + 1) * LANES]
        rot = jnp.where(first_half, pltpu.roll(xg, LANES - HEAD_DIM // 2, 1),
                        pltpu.roll(xg, HEAD_DIM // 2, 1))
        out.append(xg * cos + rot * sin)
    return out


WG_ROW = BRANCH_WIDTH + 3 * HEAD_DIM
V_ROWS = HEAD_DIM + 2 * SUBLANES


def _inproj_kernel(h_ref, g_ref, cos_ref, sin_ref, wa_ref, wr_ref, wc_ref, wt_ref,
                   qa_ref, ka_ref, vaT_ref, qb_ref, qi_ref, qc_ref, kb_ref, ki_ref, ks_ref, kw_ref,
                   vbT_ref, vsT_ref, vwT_ref, wg_ref, ck_ref, cv_ref):
    xn = _rmsnorm(h_ref[...], g_ref[...]).astype(MXU_DTYPE)
    cos, sin = cos_ref[...], sin_ref[...]
    head = lambda y, h: y[:, h * HEAD_DIM:(h + 1) * HEAD_DIM]

    ya = _mm(xn, wa_ref[...])
    for h in range(N_HEADS):
        qa_ref[0, h] = (head(ya, h) * Q_SCALE).astype(qa_ref.dtype)
        ka_ref[0, h] = head(ya, N_HEADS + h).astype(ka_ref.dtype)

    yr = _rope_lanes(_mm(xn, wr_ref[...]), cos, sin)
    for g, (q_ref, scale) in enumerate(((qb_ref, SOFTMAX_Q_SCALE), (qi_ref, 1.0),
                                        (qc_ref, SOFTMAX_Q_SCALE))):
        for half in range(2):
            og = yr[2 * g + half] * scale
            q_ref[0, 2 * half] = head(og, 0).astype(q_ref.dtype)
            q_ref[0, 2 * half + 1] = head(og, 1).astype(q_ref.dtype)
    kb_ref[0] = head(yr[6], 0).astype(kb_ref.dtype)
    ki_ref[0] = head(yr[6], 1).astype(ki_ref.dtype)
    ks_ref[0] = head(yr[7], 0).astype(ks_ref.dtype)
    kw_ref[0] = head(yr[7], 1).astype(kw_ref.dtype)

    yc = _mm(xn, wc_ref[...])
    ck_ref[0] = head(_rope_lanes(yc, cos, sin)[0], 0)
    cv_ref[0] = head(yc, 1)

    yT = _mm_nt(wt_ref[...], xn)
    ones_row = (lax.broadcasted_iota(jnp.int32, (V_ROWS - HEAD_DIM, TQ), 0) == 0).astype(vbT_ref.dtype)
    for c in range(TM // TQ):
        cols = slice(c * TQ, (c + 1) * TQ)
        for h in range(N_HEADS):
            vaT_ref[0, h, c] = yT[h * HEAD_DIM:(h + 1) * HEAD_DIM, cols].astype(vaT_ref.dtype)
        for n, v_ref in enumerate((vbT_ref, vsT_ref, vwT_ref)):
            lo = BRANCH_WIDTH + n * HEAD_DIM
            v_ref[0, c, 0:HEAD_DIM, :] = yT[lo:lo + HEAD_DIM, cols].astype(v_ref.dtype)
            v_ref[0, c, HEAD_DIM:, :] = ones_row
    ys = yT[WG_ROW:WG_ROW + 16, :]
    row = lax.broadcasted_iota(jnp.int32, ys.shape, 0)
    wg_ref[0] = jnp.where(row < IDX_HEADS, ys * W_IDX_SCALE, jax.nn.sigmoid(ys))


def _in_projection(h, g, w_in, cos_t, sin_t, B, S):
    cols = lambda *names: jnp.concatenate([w_in[:, _OFF[n][0]:_OFF[n][1]] for n in names], axis=1)
    wa = cols('q_a', 'k_a').astype(MXU_DTYPE)
    wr = cols('q_b', 'q_idx', 'q_c', 'k_b', 'k_idx', 'k_sel', 'k_win').astype(MXU_DTYPE)
    wc = cols('k_cmp', 'v_cmp').astype(MXU_DTYPE)
    wt = cols('v_a', 'v_b', 'v_sel', 'v_win', 'w_idx', 'g_c')
    wt = jnp.pad(wt, ((0, 0), (0, 512 - wt.shape[1]))).T.astype(MXU_DTYPE)
    nb, nc, cpb = S // TM, S // TQ, TM // TQ
    bf = lambda *shape: jax.ShapeDtypeStruct(shape, MXU_DTYPE)
    f32 = lambda *shape: jax.ShapeDtypeStruct(shape, F32)
    heads_spec = pl.BlockSpec((1, N_HEADS, TM, HEAD_DIM), lambda i: (i // nb, 0, i % nb, 0))
    tok_spec = pl.BlockSpec((1, TM, HEAD_DIM), lambda i: (i // nb, i % nb, 0))
    vT_spec = pl.BlockSpec((1, cpb, V_ROWS, TQ), lambda i: (i // nb, i % nb, 0, 0))
    pos = pl.BlockSpec((TM, LANES), lambda i: (i % nb, 0))
    q4, k3, v4 = bf(B, N_HEADS, S, HEAD_DIM), bf(B, S, HEAD_DIM), bf(B, nc, V_ROWS, TQ)
    return pl.pallas_call(
        _inproj_kernel,
        grid=(B * nb,),
        in_specs=[pl.BlockSpec((TM, D_MODEL), lambda i: (i, 0)), _const_spec((1, D_MODEL)), pos, pos,
                  _const_spec(wa.shape), _const_spec(wr.shape), _const_spec(wc.shape),
                  _const_spec(wt.shape)],
        out_specs=[heads_spec, heads_spec,
                   pl.BlockSpec((1, N_HEADS, cpb, HEAD_DIM, TQ), lambda i: (i // nb, 0, i % nb, 0, 0)),
                   heads_spec, heads_spec, heads_spec, tok_spec, tok_spec, tok_spec, tok_spec,
                   vT_spec, vT_spec, vT_spec,
                   pl.BlockSpec((1, 16, TM), lambda i: (i // nb, 0, i % nb)), tok_spec, tok_spec],
        out_shape=[q4, q4, bf(B, N_HEADS, nc, HEAD_DIM, TQ), q4, q4, q4, k3, k3, k3, k3, v4, v4, v4,
                   f32(B, 16, S), f32(B, S, HEAD_DIM), f32(B, S, HEAD_DIM)],
        compiler_params=_params("parallel"),
        name="in_projection",
    )(h, g.reshape(1, D_MODEL), cos_t, sin_t, wa, wr, wc, wt)


def _chunk_slice(j):
    return pl.ds(pl.multiple_of(j * TQ, TQ), TQ)


EXP_UNDERFLOW = 110.0
SCORE_BOUND_SLACK = 1.05


def _sb_kernel(q_ref, k_ref, vT_ref, tri_ref, o_ref, acc_ref, kmax_ref):
    i = pl.program_id(1)
    row = lax.broadcasted_iota(jnp.int32, (TQ, TQ), 0)
    col = lax.broadcasted_iota(jnp.int32, (TQ, TQ), 1)
    strict = row < col
    tri = tri_ref[...]
    qs = [q_ref[0, h] for h in range(N_HEADS)]
    acc_ref[...] = jnp.zeros(acc_ref.shape, F32)

    def chunks(js, cs, diag=False):
        heads = range(N_HEADS)
        zs = [[_mm_nt(k_ref[0, h, _chunk_slice(j), :], qs[h]) for h in heads] for j in js]
        cums = []
        for zj in zs:
            cums.append([])
            for z in zj:
                sp = jnp.maximum(z, 0.0) + jnp.log(1.0 + jnp.exp(-jnp.abs(z)))
                if diag:
                    sp = jnp.where(strict, sp, 0.0)
                hi = sp.astype(MXU_DTYPE)
                lo = (sp - hi.astype(F32)).astype(MXU_DTYPE)
                cums[-1].append(_mm(tri, hi) + _mm(tri, lo))
        outs = [jnp.zeros((HEAD_DIM, TQ), F32) for _ in heads]
        cs = list(cs)
        for n, j in enumerate(js):
            for h in heads:
                w = jnp.exp(zs[n][h] - cums[n][h] - cs[h])
                if diag:
                    w = jnp.where(strict, w, 0.0)
                outs[h] = outs[h] + _mm(vT_ref[0, h, j], w)
                cs[h] = cs[h] + cums[n][h][0:1, :]
        for h in heads:
            acc_ref[h] += outs[h]
        return tuple(cs)

    ones = jnp.ones((SUBLANES, HEAD_DIM), F32)

    def sq_norms(x):
        xf = x.astype(F32)
        return _mm_nt(ones, xf * xf)[0:1, :]

    @pl.when(i == 0)
    def _():
        for h in range(N_HEADS):
            def body(j, m):
                return jnp.maximum(m, sq_norms(k_ref[0, h, _chunk_slice(j), :]))
            m = lax.fori_loop(0, k_ref.shape[2] // TQ, body, jnp.zeros((1, TQ), F32))
            kmax_ref[h] = jnp.full((SUBLANES, LANES), jnp.max(m), F32)

    zbound = [jnp.sqrt(sq_norms(qs[h]) * kmax_ref[h][0:1, 0:1]) * SCORE_BOUND_SLACK for h in range(N_HEADS)]

    def negligible(cs):
        slack = cs[0] - zbound[0]
        for h in range(1, N_HEADS):
            slack = jnp.minimum(slack, cs[h] - zbound[h])
        return (jnp.min(slack) > EXP_UNDERFLOW).astype(jnp.int32)

    cs = chunks([i], tuple(jnp.zeros((1, TQ), F32) for _ in range(N_HEADS)), diag=True)

    def more(carry):
        j, done, _ = carry
        return (j >= 0) & (done == 0)

    def step(carry):
        j, _, cs = carry
        cs = chunks([j], cs)
        return j - 1, negligible(cs), cs

    lax.while_loop(more, step, (i - 1, negligible(cs), cs))
    o_ref[0] = acc_ref[...].reshape(BRANCH_WIDTH, TQ).T


def _stick_breaking(q, k, vT):
    B, H, S, _ = q.shape
    tri = jnp.asarray(np.triu(np.ones((TQ, TQ), np.float32)), MXU_DTYPE)
    return pl.pallas_call(
        _sb_kernel,
        grid=(B, S // TQ),
        in_specs=[pl.BlockSpec((1, H, TQ, HEAD_DIM), lambda b, i: (b, 0, i, 0)),
                  pl.BlockSpec((1, H, S, HEAD_DIM), lambda b, i: (b, 0, 0, 0)),
                  pl.BlockSpec((1, H, S // TQ, HEAD_DIM, TQ), lambda b, i: (b, 0, 0, 0, 0)),
                  _const_spec((TQ, TQ))],
        out_specs=pl.BlockSpec((1, TQ, BRANCH_WIDTH), lambda b, i: (b, i, 0)),
        out_shape=jax.ShapeDtypeStruct((B, S, BRANCH_WIDTH), F32),
        scratch_shapes=[pltpu.VMEM((H, HEAD_DIM, TQ), F32), pltpu.VMEM((H, SUBLANES, LANES), F32)],
        compiler_params=_params("parallel", "arbitrary"),
        name="stick_breaking",
    )(q, k, vT, tri)


PACKED_ROWS = 2 * SUBLANES


def _softmax_step(ks, qs, biases, vTs, ms, acc_ref):
    heads = range(len(qs))
    ss = [[_mm_nt(k, q) + b for q in qs] for k, b in zip(ks, biases)]
    m_new = []
    for h in heads:
        m = ms[h]
        for st in ss:
            m = jnp.maximum(m, jnp.max(st[h], axis=0, keepdims=True))
        m_new.append(m)
    pvs = []
    for h in heads:
        pv = jnp.zeros((V_ROWS, TQ), F32)
        for st, vT in zip(ss, vTs):
            pv = pv + _mm(vT, jnp.exp2(st[h] - m_new[h]))
        pvs.append(pv)
    for h in heads:
        acc_ref[h] = jnp.exp2(ms[h] - m_new[h]) * acc_ref[h] + pvs[h]
    return tuple(m_new)


def _for_chunks(lo, hi, step, ms):
    odd = (hi - lo) % 2
    ms = lax.cond(odd == 1, lambda ms: step([lo], ms), lambda ms: ms, ms)
    first = lo + odd
    return lax.fori_loop(0, (hi - lo) // 2,
                         lambda t, ms: step([first + 2 * t, first + 2 * t + 1], ms), ms)


def _softmax_init(acc_ref):
    acc_ref[...] = jnp.zeros(acc_ref.shape, F32)
    return tuple(jnp.full((1, TQ), NEG, F32) for _ in range(N_HEADS))


def _softmax_result(acc_ref, h):
    return acc_ref[h, 0:HEAD_DIM, :] / acc_ref[h, HEAD_DIM:HEAD_DIM + 1, :]


def _dsa_kernel(qi_ref, ki_ref, wg_ref, q_ref, k_ref, vT_ref, tril_ref, o_ref, sc_ref, hi_ref, lo_ref,
                acc_ref, *, topk):
    i = pl.program_id(1)
    nchunks = i + 1
    npairs = (i + 2) // 2
    qpos = i * TQ + lax.broadcasted_iota(jnp.int32, (1, TQ), 1)
    rowi = lax.broadcasted_iota(jnp.int32, (TQ, TQ), 0)
    int_min = jnp.int32(-2 ** 31)

    qis = [qi_ref[0, hh] for hh in range(IDX_HEADS)]
    wT = wg_ref[0]

    def idx_chunks(js, carry):
        rs = [[_mm_nt(ki_ref[0, _chunk_slice(j), :], qh) for qh in qis] for j in js]
        for j, rj in zip(js, rs):
            acc = jnp.zeros((TQ, TQ), F32)
            for hh in range(IDX_HEADS):
                acc = acc + wT[hh:hh + 1, :] * jnp.maximum(rj[hh], 0.0)
            sc = jnp.where(j * TQ + rowi <= qpos, acc, -jnp.inf)
            sc_ref[_chunk_slice(j), :] = sc
            hi_ref[_chunk_slice(j), :] = high_half(sc)
            lo_ref[_chunk_slice(j), :] = low_half(float_to_key(sc))
        return carry

    def float_to_key(x):
        bits = lax.bitcast_convert_type(x, jnp.int32)
        return bits ^ (jnp.right_shift(bits, 31) & jnp.int32(0x7FFFFFFF))

    def low_half(key):
        return ((key & jnp.int32(0xFFFF)) - 32768).astype(jnp.int16)

    def high_half(x):
        bits = lax.bitcast_convert_type(x, jnp.int32) & jnp.int32(-65536)
        return lax.bitcast_convert_type(bits, F32).astype(jnp.bfloat16)

    _for_chunks(0, nchunks, idx_chunks, 0)

    @pl.when(nchunks % 2 == 1)
    def _():
        sc_ref[_chunk_slice(nchunks), :] = jnp.full((TQ, TQ), -jnp.inf, F32)
        hi_ref[_chunk_slice(nchunks), :] = jnp.full((TQ, TQ), -jnp.inf, jnp.bfloat16)
        lo_ref[_chunk_slice(nchunks), :] = jnp.full((TQ, TQ), -32768, jnp.int16)

    def count_packed(ref, thr16):
        one, zero = jnp.ones((), jnp.bfloat16), jnp.zeros((), jnp.bfloat16)

        def body(j, acc):
            for r in range(2):
                blk = ref[pl.ds(pl.multiple_of(j * 2 * TQ + r * TQ, TQ), TQ), :]
                ind = jnp.where(blk >= thr16, one, zero)
                part = ind[0:PACKED_ROWS]
                for g in range(1, TQ // PACKED_ROWS):
                    part = part + ind[g * PACKED_ROWS:(g + 1) * PACKED_ROWS]
                acc = acc + part.astype(F32)
            return acc
        return jnp.sum(lax.fori_loop(0, npairs, body, jnp.zeros((PACKED_ROWS, TQ), F32)),
                       axis=0, keepdims=True)

    take_all = qpos < topk

    def search():
        def key_to_float(u):
            s = u ^ int_min
            return lax.bitcast_convert_type(jnp.where(s < 0, s ^ jnp.int32(0x7FFFFFFF), s), F32)

        def bit_body(t, carry, count_ge):
            res, n_ge, n_gt = carry
            cand = res | jnp.left_shift(jnp.int32(1), 31 - t)
            n = count_ge(cand)
            ok = n >= topk
            return jnp.where(ok, cand, res), jnp.where(ok, n, n_ge), jnp.where(ok, n_gt, n)

        zero = jnp.zeros((1, TQ), F32)
        carry = (jnp.zeros((1, TQ), jnp.int32), zero, zero)
        res, n_ge, n_above = lax.fori_loop(
            0, 16, functools.partial(
                bit_body, count_ge=lambda cand: count_packed(hi_ref, high_half(key_to_float(cand)))), carry)

        bucket = high_half(key_to_float(res))

        def keep_bucket(j, _):
            sl = pl.ds(pl.multiple_of(j * 2 * TQ, 2 * TQ), 2 * TQ)
            lo_ref[sl, :] = jnp.where(hi_ref[sl, :] == bucket, lo_ref[sl, :], jnp.int16(-32768))
            return 0

        lax.fori_loop(0, npairs, keep_bucket, 0)
        res, n_ge, n_gt = lax.fori_loop(
            16, 32, functools.partial(
                bit_body, count_ge=lambda cand: n_above + count_packed(lo_ref, low_half(cand))),
            (res, n_ge, n_above))
        thr = key_to_float(res)

        @pl.when(jnp.max(jnp.where(take_all, 0.0, n_ge)) > topk)
        def _():
            need = jnp.where(take_all, float(2 ** 30), topk - n_gt)
            tril = tril_ref[...]

            def drop(j, before):
                blk = sc_ref[_chunk_slice(j), :]
                tied = blk == thr
                rank = _mm(tril, jnp.where(tied, 1.0, 0.0)) + before
                sc_ref[_chunk_slice(j), :] = jnp.where(tied & (rank > need), -jnp.inf, blk)
                return rank[TQ - 1:TQ, :]

            lax.fori_loop(0, nchunks, drop, zero)

        return thr

    lowest = float(np.finfo(np.float32).min)
    thr = lax.cond(i * TQ + TQ > topk, search, lambda: jnp.full((1, TQ), lowest, F32))
    thr = jnp.where(take_all, lowest, thr)

    qs = [q_ref[0, h] for h in range(N_HEADS)]

    def att_chunks(js, ms):
        biases = [jnp.where(sc_ref[_chunk_slice(j), :] >= thr, 0.0, NEG) for j in js]
        return _softmax_step([k_ref[0, _chunk_slice(j), :] for j in js], qs, biases,
                             [vT_ref[0, j] for j in js], ms, acc_ref)

    _for_chunks(0, nchunks, att_chunks, _softmax_init(acc_ref))
    o_ref[0] = jnp.concatenate([_softmax_result(acc_ref, h) for h in range(N_HEADS)], axis=0).T


def _dsa_attention(qidx, kidx, wg, qb, kb, vbT, B, S):
    topk = min(DSA_TOPK, S // 4)
    assert topk <= TQ
    nc = S // TQ
    q_spec = pl.BlockSpec((1, N_HEADS, TQ, HEAD_DIM), lambda b, i: (b, 0, i, 0))
    k_spec = pl.BlockSpec((1, S, HEAD_DIM), lambda b, i: (b, 0, 0))
    tril = jnp.asarray(np.tril(np.ones((TQ, TQ), np.float32)), MXU_DTYPE)
    return pl.pallas_call(
        functools.partial(_dsa_kernel, topk=topk),
        grid=(B, nc),
        in_specs=[q_spec, k_spec, pl.BlockSpec((1, 16, TQ), lambda b, i: (b, 0, i)), q_spec, k_spec,
                  pl.BlockSpec((1, nc, V_ROWS, TQ), lambda b, i: (b, 0, 0, 0)), _const_spec((TQ, TQ))],
        out_specs=pl.BlockSpec((1, TQ, BRANCH_WIDTH), lambda b, i: (b, i, 0)),
        out_shape=jax.ShapeDtypeStruct((B, S, BRANCH_WIDTH), F32),
        scratch_shapes=[pltpu.VMEM(((nc + nc % 2) * TQ, TQ), F32),
                        pltpu.VMEM(((nc + nc % 2) * TQ, TQ), jnp.bfloat16),
                        pltpu.VMEM(((nc + nc % 2) * TQ, TQ), jnp.int16),
                        pltpu.VMEM((N_HEADS, V_ROWS, TQ), F32)],
        compiler_params=_params("parallel", "arbitrary"),
        name="dsa_attention",
    )(qidx, kidx, wg, qb, kb, vbT, tril)


def _compress_kernel(c_ref, pos_ref, w1_ref, w2_ref, o_ref, *, transpose_out):
    n = c_ref.shape[1] // CMP_STRIDE
    top = jnp.zeros((n, CMP_HIDDEN), F32)
    bot = jnp.zeros((n, CMP_HIDDEN), F32)
    for p in range(CMP_STRIDE):
        x = c_ref[0, pl.ds(p, n, stride=CMP_STRIDE), :]
        q = p + CMP_STRIDE
        top = top + _mm(x + pos_ref[p:p + 1, :], w1_ref[p * HEAD_DIM:(p + 1) * HEAD_DIM, :])
        bot = bot + _mm(x + pos_ref[q:q + 1, :], w1_ref[q * HEAD_DIM:(q + 1) * HEAD_DIM, :])
    hid = jax.nn.gelu(top + pltpu.roll(bot, n - 1, 0), approximate=True)
    out = _mm(hid, w2_ref[...])
    rowi = lax.broadcasted_iota(jnp.int32, out.shape, 0)
    out = jnp.where(rowi < n - 1, out, 0.0)
    if transpose_out:
        out = jnp.concatenate([out, jnp.zeros_like(out)], axis=1).T[0:HEAD_DIM, :]
    o_ref[0] = out.astype(o_ref.dtype)


def _compress(tok, pos_emb, w1, w2, transpose_out):
    B, S, _ = tok.shape
    assert CMP_LEN == 2 * CMP_STRIDE
    nch = S // CMP_STRIDE
    out_block = (1, HEAD_DIM, nch) if transpose_out else (1, nch, HEAD_DIM)
    return pl.pallas_call(
        functools.partial(_compress_kernel, transpose_out=transpose_out),
        grid=(B,),
        in_specs=[pl.BlockSpec((1, S, HEAD_DIM), lambda b: (b, 0, 0)),
                  _const_spec(pos_emb.shape), _const_spec(w1.shape), _const_spec(w2.shape)],
        out_specs=pl.BlockSpec(out_block, lambda b: (b, 0, 0)),
        out_shape=jax.ShapeDtypeStruct((B,) + out_block[1:], MXU_DTYPE),
        compiler_params=_params("parallel"),
        name="nsa_compress",
    )(tok, pos_emb, w1.astype(MXU_DTYPE), w2.astype(MXU_DTYPE))


def _nsa_kernel(q_ref, wg_ref, kc_ref, vcT_ref, ovT_ref, ks_ref, vsT_ref, kw_ref, vwT_ref,
                o_ref, selb_ref, acc_s_ref, acc_w_ref, *, n_cmp, n_sel, topn):
    i = pl.program_id(1)
    qpos = i * TQ + lax.broadcasted_iota(jnp.int32, (1, TQ), 1)
    rowi = lax.broadcasted_iota(jnp.int32, (TQ, TQ), 0)
    heads = [q_ref[0, h] for h in range(N_HEADS)]
    gates = wg_ref[0]

    ncp = kc_ref.shape[1]
    cidx = lax.broadcasted_iota(jnp.int32, (ncp, TQ), 0)
    cmask = (cidx * CMP_STRIDE + (CMP_LEN - 1) <= qpos) & (cidx < n_cmp)
    kc, vcT, ovT = kc_ref[0], vcT_ref[0], ovT_ref[...]
    ss = [jnp.where(cmask, _mm_nt(kc, q), -jnp.inf) for q in heads]
    ps = []
    for s in ss:
        m = jnp.max(s, axis=0, keepdims=True)
        m = jnp.where(m > -jnp.inf, m, 0.0)
        e = jnp.where(cmask, jnp.exp2(s - m), 0.0)
        ps.append(e / jnp.maximum(jnp.sum(e, axis=0, keepdims=True), 1e-30))
    o_cmp = [_mm(vcT, p) for p in ps]
    imp = jnp.zeros((n_sel, TQ), F32)
    for p in ps:
        p_hi = p.astype(MXU_DTYPE)
        imp = imp + _mm(ovT, p_hi) + _mm(ovT, p - p_hi.astype(F32))

    bidx = lax.broadcasted_iota(jnp.int32, (n_sel, TQ), 0)
    cur = qpos // SEL_LEN
    forced = (bidx == 0) | (bidx == cur) | (bidx == cur - 1)
    work = jnp.where(bidx * SEL_LEN <= qpos, jnp.where(forced, FORCED_SCORE, imp), -jnp.inf)
    sel = jnp.zeros((n_sel, TQ), F32)
    for _ in range(topn):
        m = jnp.max(work, axis=0, keepdims=True)
        first = jnp.min(jnp.where((work == m) & (m > -jnp.inf), bidx, n_sel), axis=0, keepdims=True)
        pick = bidx == first
        sel = jnp.where(pick, 1.0, sel)
        work = jnp.where(pick, -jnp.inf, work)
    selb_ref[...] = jnp.where(sel > 0.0, 0.0, NEG)

    per_chunk = TQ // SEL_LEN

    def sel_bias(j):
        rows = [jnp.broadcast_to(selb_ref[pl.ds(j * per_chunk + bb, 1), :], (SEL_LEN, TQ))
                for bb in range(per_chunk)]
        return jnp.where(j * TQ + rowi <= qpos, jnp.concatenate(rows, axis=0), NEG)

    def sel_chunks(js, ms):
        return _softmax_step([ks_ref[0, _chunk_slice(j), :] for j in js], heads,
                             [sel_bias(j) for j in js], [vsT_ref[0, j] for j in js], ms, acc_s_ref)

    _for_chunks(0, i + 1, sel_chunks, _softmax_init(acc_s_ref))

    def win_bias(j):
        diff = qpos - (j * TQ + rowi)
        return jnp.where((diff >= 0) & (diff < WINDOW), 0.0, NEG)

    def win_chunks(js, ms):
        return _softmax_step([kw_ref[0, _chunk_slice(j), :] for j in js], heads,
                             [win_bias(j) for j in js], [vwT_ref[0, j] for j in js], ms, acc_w_ref)

    _for_chunks(jnp.maximum(i - WINDOW // TQ, 0), i + 1, win_chunks, _softmax_init(acc_w_ref))

    outs = []
    for h in range(N_HEADS):
        g = lambda br: gates[IDX_HEADS + 3 * h + br:IDX_HEADS + 3 * h + br + 1, :]
        outs.append(g(0) * o_cmp[h] + g(1) * _softmax_result(acc_s_ref, h)
                    + g(2) * _softmax_result(acc_w_ref, h))
    o_ref[0] = jnp.concatenate(outs, axis=0).T


def _nsa_attention(qc, wg, kc, vcT, ksel, vselT, kwin, vwinT, B, S):
    n_cmp = (S - CMP_LEN) // CMP_STRIDE + 1
    n_sel = S // SEL_LEN
    topn = min(SEL_TOPN, n_sel)
    ncp = kc.shape[1]
    cs = np.arange(ncp) * CMP_STRIDE
    ss = np.arange(n_sel) * SEL_LEN
    overlapT = ((cs[None, :] < ss[:, None] + SEL_LEN) & (cs[None, :] + CMP_LEN > ss[:, None])
                & (np.arange(ncp)[None, :] < n_cmp)).astype(np.float32)
    k_spec = pl.BlockSpec((1, S, HEAD_DIM), lambda b, i: (b, 0, 0))
    vT_spec = pl.BlockSpec((1, S // TQ, V_ROWS, TQ), lambda b, i: (b, 0, 0, 0))
    return pl.pallas_call(
        functools.partial(_nsa_kernel, n_cmp=n_cmp, n_sel=n_sel, topn=topn),
        grid=(B, S // TQ),
        in_specs=[pl.BlockSpec((1, N_HEADS, TQ, HEAD_DIM), lambda b, i: (b, 0, i, 0)),
                  pl.BlockSpec((1, 16, TQ), lambda b, i: (b, 0, i)),
                  pl.BlockSpec((1, ncp, HEAD_DIM), lambda b, i: (b, 0, 0)),
                  pl.BlockSpec((1, HEAD_DIM, ncp), lambda b, i: (b, 0, 0)),
                  _const_spec((n_sel, ncp)), k_spec, vT_spec, k_spec, vT_spec],
        out_specs=pl.BlockSpec((1, TQ, BRANCH_WIDTH), lambda b, i: (b, i, 0)),
        out_shape=jax.ShapeDtypeStruct((B, S, BRANCH_WIDTH), F32),
        scratch_shapes=[pltpu.VMEM((n_sel, TQ), F32), pltpu.VMEM((N_HEADS, V_ROWS, TQ), F32),
                        pltpu.VMEM((N_HEADS, V_ROWS, TQ), F32)],
        compiler_params=_params("parallel", "arbitrary"),
        name="nsa_attention",
    )(qc, wg, kc, vcT, jnp.asarray(overlapT, MXU_DTYPE), ksel, vselT, kwin, vwinT)


def _merge_kernel(h_ref, g_ref, oa_ref, ob_ref, oc_ref, wg_ref, wup_ref, wout_ref, o_ref):
    x = h_ref[...]
    xn = _rmsnorm(x, g_ref[...]).astype(MXU_DTYPE)
    merged = jnp.zeros((TM, D_MODEL), F32)
    for j, br_ref in enumerate((oa_ref, ob_ref, oc_ref)):
        gate = jax.nn.sigmoid(_mm(xn, wg_ref[:, j * D_MODEL:(j + 1) * D_MODEL]))
        merged = merged + gate * _mm(br_ref[...], wup_ref[j])
    o_ref[...] = x + _mm(merged, wout_ref[...])


def _merge(h, g, o_a, o_b, o_c, w_gm, w_up, w_out):
    T = h.shape[0]
    row = lambda w: pl.BlockSpec((TM, w), lambda i: (i, 0))
    return pl.pallas_call(
        _merge_kernel,
        grid=(T // TM,),
        in_specs=[row(D_MODEL), _const_spec((1, D_MODEL)), row(BRANCH_WIDTH), row(BRANCH_WIDTH),
                  row(BRANCH_WIDTH), _const_spec(w_gm.shape), _const_spec(w_up.shape),
                  _const_spec(w_out.shape)],
        out_specs=row(D_MODEL),
        out_shape=jax.ShapeDtypeStruct((T, D_MODEL), F32),
        compiler_params=_params("parallel"),
        name="merge",
    )(h, g.reshape(1, D_MODEL), o_a, o_b, o_c, w_gm.astype(MXU_DTYPE), w_up.astype(MXU_DTYPE),
      w_out.astype(MXU_DTYPE))


def _mem_kv_kernel(m_ref, g_ref, wk_ref, wvT_ref, ks_ref, vT_ref):
    xn = _rmsnorm(m_ref[0], g_ref[...]).astype(MXU_DTYPE)
    M = xn.shape[0]
    k = _mm(xn, wk_ref[...])
    lane_head = lax.broadcasted_iota(jnp.int32, (1, BRANCH_WIDTH), 1) // HEAD_DIM
    for h in range(N_HEADS):
        ks_ref[0, h * M:(h + 1) * M, :] = jnp.where(lane_head == h, k, 0.0).astype(ks_ref.dtype)
    vT_ref[0] = _mm_nt(wvT_ref[...], xn).astype(vT_ref.dtype)


def _mem_kv(mem, g, w_kv):
    B, M, _ = mem.shape
    wk = w_kv[:, :BRANCH_WIDTH].astype(MXU_DTYPE)
    wvT = w_kv[:, BRANCH_WIDTH:].T.astype(MXU_DTYPE)
    return pl.pallas_call(
        _mem_kv_kernel,
        grid=(B,),
        in_specs=[pl.BlockSpec((1, M, D_MODEL), lambda b: (b, 0, 0)), _const_spec((1, D_MODEL)),
                  _const_spec(wk.shape), _const_spec(wvT.shape)],
        out_specs=[pl.BlockSpec((1, N_HEADS * M, BRANCH_WIDTH), lambda b: (b, 0, 0)),
                   pl.BlockSpec((1, BRANCH_WIDTH, M), lambda b: (b, 0, 0))],
        out_shape=[jax.ShapeDtypeStruct((B, N_HEADS * M, BRANCH_WIDTH), MXU_DTYPE),
                   jax.ShapeDtypeStruct((B, BRANCH_WIDTH, M), MXU_DTYPE)],
        compiler_params=_params("parallel"),
        name="memory_kv",
    )(mem, g.reshape(1, D_MODEL), wk, wvT)


def _mem_attn_kernel(h_ref, g_ref, wq_ref, ks_ref, vT_ref, wo_ref, o_ref):
    x = h_ref[...]
    q = (_mm(_rmsnorm(x, g_ref[...]), wq_ref[...]) * SOFTMAX_Q_SCALE).astype(MXU_DTYPE)
    M = vT_ref.shape[2]
    s = _mm_nt(ks_ref[0], q)
    outs = []
    for h in range(N_HEADS):
        sh = s[h * M:(h + 1) * M, :]
        p = jnp.exp2(sh - jnp.max(sh, axis=0, keepdims=True))
        o_h = _mm(vT_ref[0, h * HEAD_DIM:(h + 1) * HEAD_DIM, :], p)
        outs.append(o_h / jnp.sum(p, axis=0, keepdims=True))
    o = jnp.concatenate(outs, axis=0).T
    o_ref[...] = x + _mm(o, wo_ref[...])


def _memory_attention(h, ks, vT, g_q, w_q, w_o, S):
    T = h.shape[0]
    row = pl.BlockSpec((TM, D_MODEL), lambda i: (i, 0))
    nb = S // TM
    return pl.pallas_call(
        _mem_attn_kernel,
        grid=(T // TM,),
        in_specs=[row, _const_spec((1, D_MODEL)), _const_spec(w_q.shape),
                  pl.BlockSpec((1,) + ks.shape[1:], lambda i: (i // nb, 0, 0)),
                  pl.BlockSpec((1,) + vT.shape[1:], lambda i: (i // nb, 0, 0)),
                  _const_spec(w_o.shape)],
        out_specs=row,
        out_shape=jax.ShapeDtypeStruct((T, D_MODEL), F32),
        compiler_params=_params("parallel"),
        name="memory_attention",
    )(h, g_q.reshape(1, D_MODEL), w_q.astype(MXU_DTYPE), ks, vT, w_o.astype(MXU_DTYPE))


FF_CHUNK = 256


def _swiglu_kernel(h_ref, g_ref, win_ref, wout_ref, gf_ref, o_ref, *, final_norm):
    x = h_ref[...]
    xn = _rmsnorm(x, g_ref[...]).astype(MXU_DTYPE)
    out = x
    for c in range(D_FF // FF_CHUNK):
        lo, hi = c * FF_CHUNK, (c + 1) * FF_CHUNK
        a = _mm(xn, win_ref[:, lo:hi])
        b = _mm(xn, win_ref[:, D_FF + lo:D_FF + hi])
        out = out + _mm(a * jax.nn.sigmoid(a) * b, wout_ref[lo:hi, :])
    if final_norm:
        out = _rmsnorm(out, gf_ref[...])
    o_ref[...] = out


def _swiglu(h, g, w_in, w_out, g_final, final_norm):
    T = h.shape[0]
    row = pl.BlockSpec((TM, D_MODEL), lambda i: (i, 0))
    return pl.pallas_call(
        functools.partial(_swiglu_kernel, final_norm=final_norm),
        grid=(T // TM,),
        in_specs=[row, _const_spec((1, D_MODEL)), _const_spec(w_in.shape), _const_spec(w_out.shape),
                  _const_spec((1, D_MODEL))],
        out_specs=row,
        out_shape=jax.ShapeDtypeStruct((T, D_MODEL), F32),
        compiler_params=_params("parallel"),
        name="swiglu",
    )(h, g.reshape(1, D_MODEL), w_in.astype(MXU_DTYPE), w_out.astype(MXU_DTYPE),
      g_final.reshape(1, D_MODEL))


def _rope_tables(S):
    inv = ROPE_THETA ** (-jnp.arange(0, HEAD_DIM, 2, dtype=F32) / HEAD_DIM)
    ang = jnp.arange(S, dtype=F32)[:, None] * inv[None, :]
    cos, sin = jnp.cos(ang), jnp.sin(ang)
    return (jnp.concatenate([cos, cos, cos, cos], axis=1),
            jnp.concatenate([-sin, sin, -sin, sin], axis=1))


def _hybrid_mixer(h, B, S, g, w_in, cmp_k, cmp_v, w_up, w_out, cos_t, sin_t):
    T = B * S
    (qa, ka, vaT, qb, qidx, qc, kb, kidx, ksel, kwin, vbT, vselT, vwinT, wg, ck, cv) = _in_projection(
        h, g, w_in, cos_t, sin_t, B, S)
    o_a = _stick_breaking(qa, ka, vaT)
    o_b = _dsa_attention(qidx, kidx, wg, qb, kb, vbT, B, S)
    kc = _compress(ck, *cmp_k, transpose_out=False)
    vcT = _compress(cv, *cmp_v, transpose_out=True)
    o_c = _nsa_attention(qc, wg, kc, vcT, ksel, vselT, kwin, vwinT, B, S)

    w_gm = w_in[:, _OFF['g_merge'][0]:]
    return _merge(h, g, o_a.reshape(T, 256), o_b.reshape(T, 256), o_c.reshape(T, 256), w_gm, w_up, w_out)


def kernel(x, mem, norm_mix, w_in, cmp_pos_k, cmp_w1_k, cmp_w2_k, cmp_pos_v, cmp_w1_v, cmp_w2_v,
           w_up, w_out, norm_mem_q, norm_mem_kv, w_mem_q, w_mem_kv, w_mem_o,
           norm_ffn, w_ffn_in, w_ffn_out, norm_final):
    B, S, D = x.shape
    depth = w_in.shape[0]
    assert D == D_MODEL and S % TM == 0 and S % TQ == 0 and WINDOW % TQ == 0
    cos_t, sin_t = _rope_tables(S)
    h = x.reshape(B * S, D)
    for l in range(depth):
        h = _hybrid_mixer(h, B, S, norm_mix[l], w_in[l],
                          (cmp_pos_k[l], cmp_w1_k[l], cmp_w2_k[l]),
                          (cmp_pos_v[l], cmp_w1_v[l], cmp_w2_v[l]), w_up[l], w_out[l], cos_t, sin_t)
        ks, vT = _mem_kv(mem, norm_mem_kv[l], w_mem_kv[l])
        h = _memory_attention(h, ks, vT, norm_mem_q[l], w_mem_q[l], w_mem_o[l], S)
        h = _swiglu(h, norm_ffn[l], w_ffn_in[l], w_ffn_out[l], norm_final, l == depth - 1)
    return h.reshape(B, S, D)
```

```python
import functools

import jax
import jax.numpy as jnp
import numpy as np
from jax import lax
from jax.experimental import pallas as pl
from jax.experimental.pallas import tpu as pltpu

F32 = jnp.float32
MXU_DTYPE = jnp.bfloat16

D_MODEL = 1024
HEAD_DIM = 64
N_HEADS = 4
BRANCH_WIDTH = N_HEADS * HEAD_DIM
IDX_HEADS = 4
DSA_TOPK = 256
CMP_LEN = 32
CMP_STRIDE = 16
CMP_HIDDEN = 128
SEL_LEN = 64
SEL_TOPN = 16
WINDOW = 512
D_FF = 2816
ROPE_THETA = 10000.0
RMS_EPS = 1e-6
FORCED_SCORE = 1e4
Q_SCALE = HEAD_DIM ** -0.5
LOG2E = 1.4426950408889634
SOFTMAX_Q_SCALE = Q_SCALE * LOG2E
W_IDX_SCALE = (IDX_HEADS * HEAD_DIM) ** -0.5
NEG = -1e30

LANES = 128
SUBLANES = 8
VMEM_LIMIT = 56 * 1024 * 1024
TM = 512
TQ = 256

_OFF = {}
_o = 0
for _n, _w in (('q_a', 256), ('k_a', 256), ('v_a', 256), ('q_b', 256), ('k_b', 64), ('v_b', 64),
               ('q_idx', 256), ('k_idx', 64), ('w_idx', 4), ('q_c', 256), ('k_cmp', 64),
               ('v_cmp', 64), ('k_sel', 64), ('v_sel', 64), ('k_win', 64), ('v_win', 64),
               ('g_c', 12), ('g_merge', 3 * D_MODEL)):
    _OFF[_n] = (_o, _o + _w)
    _o += _w
D_IN = _o


def _mm(a, b):
    return jnp.dot(a.astype(MXU_DTYPE), b.astype(MXU_DTYPE), preferred_element_type=F32)


def _mm_nt(a, b):
    return lax.dot_general(a.astype(MXU_DTYPE), b.astype(MXU_DTYPE),
                           (((1,), (1,)), ((), ())), preferred_element_type=F32)


def _params(*sem):
    return pltpu.CompilerParams(dimension_semantics=sem, vmem_limit_bytes=VMEM_LIMIT)


def _const_spec(shape):
    nd = len(shape)
    return pl.BlockSpec(shape, lambda *_: (0,) * nd, pipeline_mode=pl.Buffered(1))


def _rmsnorm(x, g):
    return x * lax.rsqrt(jnp.mean(x * x, axis=-1, keepdims=True) + RMS_EPS) * g


def _rope_lanes(x, cos, sin):
    lane = lax.broadcasted_iota(jnp.int32, (1, LANES), 1)
    first_half = (lane % HEAD_DIM) < (HEAD_DIM // 2)
    out = []
    for g in range(x.shape[1] // LANES):
        xg = x[:, g * LANES:(g + 1) * LANES]
        rot = jnp.where(first_half, pltpu.roll(xg, LANES - HEAD_DIM // 2, 1),
                        pltpu.roll(xg, HEAD_DIM // 2, 1))
        out.append(xg * cos + rot * sin)
    return out


WG_ROW = BRANCH_WIDTH + 3 * HEAD_DIM
V_ROWS = HEAD_DIM + 2 * SUBLANES


def _inproj_kernel(h_ref, g_ref, cos_ref, sin_ref, wa_ref, wr_ref, wc_ref, wt_ref,
                   qa_ref, ka_ref, vaT_ref, qb_ref, qi_ref, qc_ref, kb_ref, ki_ref, ks_ref, kw_ref,
                   vbT_ref, vsT_ref, vwT_ref, wg_ref, ck_ref, cv_ref):
    xn = _rmsnorm(h_ref[...], g_ref[...]).astype(MXU_DTYPE)
    cos, sin = cos_ref[...], sin_ref[...]
    head = lambda y, h: y[:, h * HEAD_DIM:(h + 1) * HEAD_DIM]

    ya = _mm(xn, wa_ref[...])
    for h in range(N_HEADS):
        qa_ref[0, h] = (head(ya, h) * Q_SCALE).astype(qa_ref.dtype)
        ka_ref[0, h] = head(ya, N_HEADS + h).astype(ka_ref.dtype)

    yr = _rope_lanes(_mm(xn, wr_ref[...]), cos, sin)
    for g, (q_ref, scale) in enumerate(((qb_ref, SOFTMAX_Q_SCALE), (qi_ref, 1.0),
                                        (qc_ref, SOFTMAX_Q_SCALE))):
        for half in range(2):
            og = yr[2 * g + half] * scale
            q_ref[0, 2 * half] = head(og, 0).astype(q_ref.dtype)
            q_ref[0, 2 * half + 1] = head(og, 1).astype(q_ref.dtype)
    kb_ref[0] = head(yr[6], 0).astype(kb_ref.dtype)
    ki_ref[0] = head(yr[6], 1).astype(ki_ref.dtype)
    ks_ref[0] = head(yr[7], 0).astype(ks_ref.dtype)
    kw_ref[0] = head(yr[7], 1).astype(kw_ref.dtype)

    yc = _mm(xn, wc_ref[...])
    ck_ref[0] = head(_rope_lanes(yc, cos, sin)[0], 0)
    cv_ref[0] = head(yc, 1)

    yT = _mm_nt(wt_ref[...], xn)
    ones_row = (lax.broadcasted_iota(jnp.int32, (V_ROWS - HEAD_DIM, TQ), 0) == 0).astype(vbT_ref.dtype)
    for c in range(TM // TQ):
        cols = slice(c * TQ, (c + 1) * TQ)
        for h in range(N_HEADS):
            vaT_ref[0, h, c] = yT[h * HEAD_DIM:(h + 1) * HEAD_DIM, cols].astype(vaT_ref.dtype)
        for n, v_ref in enumerate((vbT_ref, vsT_ref, vwT_ref)):
            lo = BRANCH_WIDTH + n * HEAD_DIM
            v_ref[0, c, 0:HEAD_DIM, :] = yT[lo:lo + HEAD_DIM, cols].astype(v_ref.dtype)
            v_ref[0, c, HEAD_DIM:, :] = ones_row
    ys = yT[WG_ROW:WG_ROW + 16, :]
    row = lax.broadcasted_iota(jnp.int32, ys.shape, 0)
    wg_ref[0] = jnp.where(row < IDX_HEADS, ys * W_IDX_SCALE, jax.nn.sigmoid(ys))


def _in_projection(h, g, w_in, cos_t, sin_t, B, S):
    cols = lambda *names: jnp.concatenate([w_in[:, _OFF[n][0]:_OFF[n][1]] for n in names], axis=1)
    wa = cols('q_a', 'k_a').astype(MXU_DTYPE)
    wr = cols('q_b', 'q_idx', 'q_c', 'k_b', 'k_idx', 'k_sel', 'k_win').astype(MXU_DTYPE)
    wc = cols('k_cmp', 'v_cmp').astype(MXU_DTYPE)
    wt = cols('v_a', 'v_b', 'v_sel', 'v_win', 'w_idx', 'g_c')
    wt = jnp.pad(wt, ((0, 0), (0, 512 - wt.shape[1]))).T.astype(MXU_DTYPE)
    nb, nc, cpb = S // TM, S // TQ, TM // TQ
    bf = lambda *shape: jax.ShapeDtypeStruct(shape, MXU_DTYPE)
    f32 = lambda *shape: jax.ShapeDtypeStruct(shape, F32)
    heads_spec = pl.BlockSpec((1, N_HEADS, TM, HEAD_DIM), lambda i: (i // nb, 0, i % nb, 0))
    tok_spec = pl.BlockSpec((1, TM, HEAD_DIM), lambda i: (i // nb, i % nb, 0))
    vT_spec = pl.BlockSpec((1, cpb, V_ROWS, TQ), lambda i: (i // nb, i % nb, 0, 0))
    pos = pl.BlockSpec((TM, LANES), lambda i: (i % nb, 0))
    q4, k3, v4 = bf(B, N_HEADS, S, HEAD_DIM), bf(B, S, HEAD_DIM), bf(B, nc, V_ROWS, TQ)
    return pl.pallas_call(
        _inproj_kernel,
        grid=(B * nb,),
        in_specs=[pl.BlockSpec((TM, D_MODEL), lambda i: (i, 0)), _const_spec((1, D_MODEL)), pos, pos,
                  _const_spec(wa.shape), _const_spec(wr.shape), _const_spec(wc.shape),
                  _const_spec(wt.shape)],
        out_specs=[heads_spec, heads_spec,
                   pl.BlockSpec((1, N_HEADS, cpb, HEAD_DIM, TQ), lambda i: (i // nb, 0, i % nb, 0, 0)),
                   heads_spec, heads_spec, heads_spec, tok_spec, tok_spec, tok_spec, tok_spec,
                   vT_spec, vT_spec, vT_spec,
                   pl.BlockSpec((1, 16, TM), lambda i: (i // nb, 0, i % nb)), tok_spec, tok_spec],
        out_shape=[q4, q4, bf(B, N_HEADS, nc, HEAD_DIM, TQ), q4, q4, q4, k3, k3, k3, k3, v4, v4, v4,
                   f32(B, 16, S), f32(B, S, HEAD_DIM), f32(B, S, HEAD_DIM)],
        compiler_params=_params("parallel"),
        name="in_projection",
    )(h, g.reshape(1, D_MODEL), cos_t, sin_t, wa, wr, wc, wt)


def _chunk_slice(j):
    return pl.ds(pl.multiple_of(j * TQ, TQ), TQ)


EXP_UNDERFLOW = 110.0
SCORE_BOUND_SLACK = 1.05


def _sb_kernel(q_ref, k_ref, vT_ref, tri_ref, o_ref, acc_ref, kmax_ref):
    i = pl.program_id(1)
    row = lax.broadcasted_iota(jnp.int32, (TQ, TQ), 0)
    col = lax.broadcasted_iota(jnp.int32, (TQ, TQ), 1)
    strict = row < col
    tri = tri_ref[...]
    qs = [q_ref[0, h] for h in range(N_HEADS)]
    acc_ref[...] = jnp.zeros(acc_ref.shape, F32)

    def chunks(js, cs, diag=False):
        heads = range(N_HEADS)
        zs = [[_mm_nt(k_ref[0, h, _chunk_slice(j), :], qs[h]) for h in heads] for j in js]
        cums = []
        for zj in zs:
            cums.append([])
            for z in zj:
                sp = jnp.maximum(z, 0.0) + jnp.log(1.0 + jnp.exp(-jnp.abs(z)))
                if diag:
                    sp = jnp.where(strict, sp, 0.0)
                hi = sp.astype(MXU_DTYPE)
                lo = (sp - hi.astype(F32)).astype(MXU_DTYPE)
                cums[-1].append(_mm(tri, hi) + _mm(tri, lo))
        outs = [jnp.zeros((HEAD_DIM, TQ), F32) for _ in heads]
        cs = list(cs)
        for n, j in enumerate(js):
            for h in heads:
                w = jnp.exp(zs[n][h] - cums[n][h] - cs[h])
                if diag:
                    w = jnp.where(strict, w, 0.0)
                outs[h] = outs[h] + _mm(vT_ref[0, h, j], w)
                cs[h] = cs[h] + cums[n][h][0:1, :]
        for h in heads:
            acc_ref[h] += outs[h]
        return tuple(cs)

    ones = jnp.ones((SUBLANES, HEAD_DIM), F32)

    def sq_norms(x):
        xf = x.astype(F32)
        return _mm_nt(ones, xf * xf)[0:1, :]

    @pl.when(i == 0)
    def _():
        for h in range(N_HEADS):
            def body(j, m):
                return jnp.maximum(m, sq_norms(k_ref[0, h, _chunk_slice(j), :]))
            m = lax.fori_loop(0, k_ref.shape[2] // TQ, body, jnp.zeros((1, TQ), F32))
            kmax_ref[h] = jnp.full((SUBLANES, LANES), jnp.max(m), F32)

    zbound = [jnp.sqrt(sq_norms(qs[h]) * kmax_ref[h][0:1, 0:1]) * SCORE_BOUND_SLACK for h in range(N_HEADS)]

    def negligible(cs):
        slack = cs[0] - zbound[0]
        for h in range(1, N_HEADS):
            slack = jnp.minimum(slack, cs[h] - zbound[h])
        return (jnp.min(slack) > EXP_UNDERFLOW).astype(jnp.int32)

    cs = chunks([i], tuple(jnp.zeros((1, TQ), F32) for _ in range(N_HEADS)), diag=True)

    def more(carry):
        j, done, _ = carry
        return (j >= 0) & (done == 0)

    def step(carry):
        j, _, cs = carry
        cs = chunks([j], cs)
        return j - 1, negligible(cs), cs

    lax.while_loop(more, step, (i - 1, negligible(cs), cs))
    o_ref[0] = acc_ref[...].reshape(BRANCH_WIDTH, TQ).T


def _stick_breaking(q, k, vT):
    B, H, S, _ = q.shape
    tri = jnp.asarray(np.triu(np.ones((TQ, TQ), np.float32)), MXU_DTYPE)
    return pl.pallas_call(
        _sb_kernel,
        grid=(B, S // TQ),
        in_specs=[pl.BlockSpec((1, H, TQ, HEAD_DIM), lambda b, i: (b, 0, i, 0)),
                  pl.BlockSpec((1, H, S, HEAD_DIM), lambda b, i: (b, 0, 0, 0)),
                  pl.BlockSpec((1, H, S // TQ, HEAD_DIM, TQ), lambda b, i: (b, 0, 0, 0, 0)),
                  _const_spec((TQ, TQ))],
        out_specs=pl.BlockSpec((1, TQ, BRANCH_WIDTH), lambda b, i: (b, i, 0)),
        out_shape=jax.ShapeDtypeStruct((B, S, BRANCH_WIDTH), F32),
        scratch_shapes=[pltpu.VMEM((H, HEAD_DIM, TQ), F32), pltpu.VMEM((H, SUBLANES, LANES), F32)],
        compiler_params=_params("parallel", "arbitrary"),
        name="stick_breaking",
    )(q, k, vT, tri)


PACKED_ROWS = 2 * SUBLANES


def _softmax_step(ks, qs, biases, vTs, ms, acc_ref):
    heads = range(len(qs))
    ss = [[_mm_nt(k, q) + b for q in qs] for k, b in zip(ks, biases)]
    m_new = []
    for h in heads:
        m = ms[h]
        for st in ss:
            m = jnp.maximum(m, jnp.max(st[h], axis=0, keepdims=True))
        m_new.append(m)
    pvs = []
    for h in heads:
        pv = jnp.zeros((V_ROWS, TQ), F32)
        for st, vT in zip(ss, vTs):
            pv = pv + _mm(vT, jnp.exp2(st[h] - m_new[h]))
        pvs.append(pv)
    for h in heads:
        acc_ref[h] = jnp.exp2(ms[h] - m_new[h]) * acc_ref[h] + pvs[h]
    return tuple(m_new)


def _for_chunks(lo, hi, step, ms):
    odd = (hi - lo) % 2
    ms = lax.cond(odd == 1, lambda ms: step([lo], ms), lambda ms: ms, ms)
    first = lo + odd
    return lax.fori_loop(0, (hi - lo) // 2,
                         lambda t, ms: step([first + 2 * t, first + 2 * t + 1], ms), ms)


def _softmax_result(acc_ref, h):
    return acc_ref[h, 0:HEAD_DIM, :] / acc_ref[h, HEAD_DIM:HEAD_DIM + 1, :]


MAX_FIXED_SHIFT = 45.0


def _sq_norms(x):
    xf = x.astype(F32)
    return _mm_nt(jnp.ones((SUBLANES, HEAD_DIM), F32), xf * xf)[0:1, :]


def _max_sq_norm(k_ref):
    def body(j, m):
        return jnp.maximum(m, _sq_norms(k_ref[0, _chunk_slice(j), :]))
    return jnp.max(lax.fori_loop(0, k_ref.shape[1] // TQ, body, jnp.zeros((1, TQ), F32)))


def _score_bound(qs, kmax2):
    qn2 = _sq_norms(qs[0])
    for q in qs[1:]:
        qn2 = jnp.maximum(qn2, _sq_norms(q))
    return jnp.sqrt(qn2 * kmax2) * SCORE_BOUND_SLACK


def _masked_attention(lo, hi, load, bias_fn, qs, bound, acc_ref):
    acc_ref[...] = jnp.zeros(acc_ref.shape, F32)
    fixed = jnp.max(bound) < MAX_FIXED_SHIFT

    @pl.when(fixed)
    def _():
        def step(js, carry):
            tiles = [load(j) for j in js]
            biases = [bias_fn(j, -bound) for j in js]
            ss = [[_mm_nt(k, q) + b for q in qs] for (k, _), b in zip(tiles, biases)]
            for h in range(len(qs)):
                pv = jnp.zeros((V_ROWS, TQ), F32)
                for st, (_, vT) in zip(ss, tiles):
                    pv = pv + _mm(vT, jnp.exp2(st[h]))
                acc_ref[h] += pv
            return carry

        _for_chunks(lo, hi, step, 0)

    @pl.when(jnp.logical_not(fixed))
    def _():
        def step(js, ms):
            tiles = [load(j) for j in js]
            return _softmax_step([k for k, _ in tiles], qs, [bias_fn(j, 0.0) for j in js],
                                 [vT for _, vT in tiles], ms, acc_ref)

        _for_chunks(lo, hi, step, tuple(jnp.full((1, TQ), NEG, F32) for _ in qs))


def _dsa_kernel(qi_ref, ki_ref, wg_ref, q_ref, k_ref, vT_ref, tril_ref, o_ref, sc_ref, hi_ref, lo_ref,
                acc_ref, kmax_ref, *, topk):
    i = pl.program_id(1)
    nchunks = i + 1
    npairs = (i + 2) // 2
    qpos = i * TQ + lax.broadcasted_iota(jnp.int32, (1, TQ), 1)
    rowi = lax.broadcasted_iota(jnp.int32, (TQ, TQ), 0)
    int_min = jnp.int32(-2 ** 31)

    qis = [qi_ref[0, hh] for hh in range(IDX_HEADS)]
    wT = wg_ref[0]

    def idx_chunks(js, carry):
        rs = [[_mm_nt(ki_ref[0, _chunk_slice(j), :], qh) for qh in qis] for j in js]
        for j, rj in zip(js, rs):
            acc = jnp.zeros((TQ, TQ), F32)
            for hh in range(IDX_HEADS):
                acc = acc + wT[hh:hh + 1, :] * jnp.maximum(rj[hh], 0.0)
            sc = jnp.where(j * TQ + rowi <= qpos, acc, -jnp.inf)
            sc_ref[_chunk_slice(j), :] = sc
            hi_ref[_chunk_slice(j), :] = high_half(sc)
            lo_ref[_chunk_slice(j), :] = low_half(float_to_key(sc))
        return carry

    def float_to_key(x):
        bits = lax.bitcast_convert_type(x, jnp.int32)
        return bits ^ (jnp.right_shift(bits, 31) & jnp.int32(0x7FFFFFFF))

    def low_half(key):
        return ((key & jnp.int32(0xFFFF)) - 32768).astype(jnp.int16)

    def high_half(x):
        bits = lax.bitcast_convert_type(x, jnp.int32) & jnp.int32(-65536)
        return lax.bitcast_convert_type(bits, F32).astype(jnp.bfloat16)

    _for_chunks(0, nchunks, idx_chunks, 0)

    @pl.when(nchunks % 2 == 1)
    def _():
        sc_ref[_chunk_slice(nchunks), :] = jnp.full((TQ, TQ), -jnp.inf, F32)
        hi_ref[_chunk_slice(nchunks), :] = jnp.full((TQ, TQ), -jnp.inf, jnp.bfloat16)
        lo_ref[_chunk_slice(nchunks), :] = jnp.full((TQ, TQ), -32768, jnp.int16)

    def count_packed(ref, thr16):
        one, zero = jnp.ones((), jnp.bfloat16), jnp.zeros((), jnp.bfloat16)

        def body(j, acc):
            for r in range(2):
                blk = ref[pl.ds(pl.multiple_of(j * 2 * TQ + r * TQ, TQ), TQ), :]
                ind = jnp.where(blk >= thr16, one, zero)
                part = ind[0:PACKED_ROWS]
                for g in range(1, TQ // PACKED_ROWS):
                    part = part + ind[g * PACKED_ROWS:(g + 1) * PACKED_ROWS]
                acc = acc + part.astype(F32)
            return acc
        return jnp.sum(lax.fori_loop(0, npairs, body, jnp.zeros((PACKED_ROWS, TQ), F32)),
                       axis=0, keepdims=True)

    take_all = qpos < topk

    def search():
        def key_to_float(u):
            s = u ^ int_min
            return lax.bitcast_convert_type(jnp.where(s < 0, s ^ jnp.int32(0x7FFFFFFF), s), F32)

        def bit_body(t, carry, count_ge):
            res, n_ge, n_gt = carry
            cand = res | jnp.left_shift(jnp.int32(1), 31 - t)
            n = count_ge(cand)
            ok = n >= topk
            return jnp.where(ok, cand, res), jnp.where(ok, n, n_ge), jnp.where(ok, n_gt, n)

        zero = jnp.zeros((1, TQ), F32)
        carry = (jnp.zeros((1, TQ), jnp.int32), zero, zero)
        res, n_ge, n_above = lax.fori_loop(
            0, 16, functools.partial(
                bit_body, count_ge=lambda cand: count_packed(hi_ref, high_half(key_to_float(cand)))), carry)

        bucket = high_half(key_to_float(res))

        def keep_bucket(j, _):
            sl = pl.ds(pl.multiple_of(j * 2 * TQ, 2 * TQ), 2 * TQ)
            lo_ref[sl, :] = jnp.where(hi_ref[sl, :] == bucket, lo_ref[sl, :], jnp.int16(-32768))
            return 0

        lax.fori_loop(0, npairs, keep_bucket, 0)
        res, n_ge, n_gt = lax.fori_loop(
            16, 32, functools.partial(
                bit_body, count_ge=lambda cand: n_above + count_packed(lo_ref, low_half(cand))),
            (res, n_ge, n_above))
        thr = key_to_float(res)

        @pl.when(jnp.max(jnp.where(take_all, 0.0, n_ge)) > topk)
        def _():
            need = jnp.where(take_all, float(2 ** 30), topk - n_gt)
            tril = tril_ref[...]

            def drop(j, before):
                blk = sc_ref[_chunk_slice(j), :]
                tied = blk == thr
                rank = _mm(tril, jnp.where(tied, 1.0, 0.0)) + before
                sc_ref[_chunk_slice(j), :] = jnp.where(tied & (rank > need), -jnp.inf, blk)
                return rank[TQ - 1:TQ, :]

            lax.fori_loop(0, nchunks, drop, zero)

        return thr

    lowest = float(np.finfo(np.float32).min)
    thr = lax.cond(i * TQ + TQ > topk, search, lambda: jnp.full((1, TQ), lowest, F32))
    thr = jnp.where(take_all, lowest, thr)

    @pl.when(i == 0)
    def _():
        kmax_ref[...] = jnp.full(kmax_ref.shape, _max_sq_norm(k_ref), F32)

    qs = [q_ref[0, h] for h in range(N_HEADS)]
    _masked_attention(0, nchunks,
                      lambda j: (k_ref[0, _chunk_slice(j), :], vT_ref[0, j]),
                      lambda j, on: jnp.where(sc_ref[_chunk_slice(j), :] >= thr, on, NEG),
                      qs, _score_bound(qs, kmax_ref[0:1, 0:1]), acc_ref)
    o_ref[0] = jnp.concatenate([_softmax_result(acc_ref, h) for h in range(N_HEADS)], axis=0).T


def _dsa_attention(qidx, kidx, wg, qb, kb, vbT, B, S):
    topk = min(DSA_TOPK, S // 4)
    assert topk <= TQ
    nc = S // TQ
    q_spec = pl.BlockSpec((1, N_HEADS, TQ, HEAD_DIM), lambda b, i: (b, 0, i, 0))
    k_spec = pl.BlockSpec((1, S, HEAD_DIM), lambda b, i: (b, 0, 0))
    tril = jnp.asarray(np.tril(np.ones((TQ, TQ), np.float32)), MXU_DTYPE)
    return pl.pallas_call(
        functools.partial(_dsa_kernel, topk=topk),
        grid=(B, nc),
        in_specs=[q_spec, k_spec, pl.BlockSpec((1, 16, TQ), lambda b, i: (b, 0, i)), q_spec, k_spec,
                  pl.BlockSpec((1, nc, V_ROWS, TQ), lambda b, i: (b, 0, 0, 0)), _const_spec((TQ, TQ))],
        out_specs=pl.BlockSpec((1, TQ, BRANCH_WIDTH), lambda b, i: (b, i, 0)),
        out_shape=jax.ShapeDtypeStruct((B, S, BRANCH_WIDTH), F32),
        scratch_shapes=[pltpu.VMEM(((nc + nc % 2) * TQ, TQ), F32),
                        pltpu.VMEM(((nc + nc % 2) * TQ, TQ), jnp.bfloat16),
                        pltpu.VMEM(((nc + nc % 2) * TQ, TQ), jnp.int16),
                        pltpu.VMEM((N_HEADS, V_ROWS, TQ), F32), pltpu.VMEM((SUBLANES, LANES), F32)],
        compiler_params=_params("parallel", "arbitrary"),
        name="dsa_attention",
    )(qidx, kidx, wg, qb, kb, vbT, tril)


def _compress_kernel(c_ref, pos_ref, w1_ref, w2_ref, o_ref, *, transpose_out):
    n = c_ref.shape[1] // CMP_STRIDE
    top = jnp.zeros((n, CMP_HIDDEN), F32)
    bot = jnp.zeros((n, CMP_HIDDEN), F32)
    for p in range(CMP_STRIDE):
        x = c_ref[0, pl.ds(p, n, stride=CMP_STRIDE), :]
        q = p + CMP_STRIDE
        top = top + _mm(x + pos_ref[p:p + 1, :], w1_ref[p * HEAD_DIM:(p + 1) * HEAD_DIM, :])
        bot = bot + _mm(x + pos_ref[q:q + 1, :], w1_ref[q * HEAD_DIM:(q + 1) * HEAD_DIM, :])
    hid = jax.nn.gelu(top + pltpu.roll(bot, n - 1, 0), approximate=True)
    out = _mm(hid, w2_ref[...])
    rowi = lax.broadcasted_iota(jnp.int32, out.shape, 0)
    out = jnp.where(rowi < n - 1, out, 0.0)
    if transpose_out:
        out = jnp.concatenate([out, jnp.zeros_like(out)], axis=1).T[0:HEAD_DIM, :]
    o_ref[0] = out.astype(o_ref.dtype)


def _compress(tok, pos_emb, w1, w2, transpose_out):
    B, S, _ = tok.shape
    assert CMP_LEN == 2 * CMP_STRIDE
    nch = S // CMP_STRIDE
    out_block = (1, HEAD_DIM, nch) if transpose_out else (1, nch, HEAD_DIM)
    return pl.pallas_call(
        functools.partial(_compress_kernel, transpose_out=transpose_out),
        grid=(B,),
        in_specs=[pl.BlockSpec((1, S, HEAD_DIM), lambda b: (b, 0, 0)),
                  _const_spec(pos_emb.shape), _const_spec(w1.shape), _const_spec(w2.shape)],
        out_specs=pl.BlockSpec(out_block, lambda b: (b, 0, 0)),
        out_shape=jax.ShapeDtypeStruct((B,) + out_block[1:], MXU_DTYPE),
        compiler_params=_params("parallel"),
        name="nsa_compress",
    )(tok, pos_emb, w1.astype(MXU_DTYPE), w2.astype(MXU_DTYPE))


def _nsa_kernel(q_ref, wg_ref, kc_ref, vcT_ref, ovT_ref, ks_ref, vsT_ref, kw_ref, vwT_ref,
                o_ref, selb_ref, acc_s_ref, acc_w_ref, kmax_ref, *, n_cmp, n_sel, topn):
    i = pl.program_id(1)
    qpos = i * TQ + lax.broadcasted_iota(jnp.int32, (1, TQ), 1)
    rowi = lax.broadcasted_iota(jnp.int32, (TQ, TQ), 0)
    heads = [q_ref[0, h] for h in range(N_HEADS)]
    gates = wg_ref[0]

    ncp = kc_ref.shape[1]
    cidx = lax.broadcasted_iota(jnp.int32, (ncp, TQ), 0)
    cmask = (cidx * CMP_STRIDE + (CMP_LEN - 1) <= qpos) & (cidx < n_cmp)
    kc, vcT, ovT = kc_ref[0], vcT_ref[0], ovT_ref[...]
    ss = [jnp.where(cmask, _mm_nt(kc, q), -jnp.inf) for q in heads]
    ps = []
    for s in ss:
        m = jnp.max(s, axis=0, keepdims=True)
        m = jnp.where(m > -jnp.inf, m, 0.0)
        e = jnp.where(cmask, jnp.exp2(s - m), 0.0)
        ps.append(e / jnp.maximum(jnp.sum(e, axis=0, keepdims=True), 1e-30))
    o_cmp = [_mm(vcT, p) for p in ps]
    imp = jnp.zeros((n_sel, TQ), F32)
    for p in ps:
        p_hi = p.astype(MXU_DTYPE)
        imp = imp + _mm(ovT, p_hi) + _mm(ovT, p - p_hi.astype(F32))

    bidx = lax.broadcasted_iota(jnp.int32, (n_sel, TQ), 0)
    cur = qpos // SEL_LEN
    forced = (bidx == 0) | (bidx == cur) | (bidx == cur - 1)
    work = jnp.where(bidx * SEL_LEN <= qpos, jnp.where(forced, FORCED_SCORE, imp), -jnp.inf)
    sel = jnp.zeros((n_sel, TQ), F32)
    for _ in range(topn):
        m = jnp.max(work, axis=0, keepdims=True)
        first = jnp.min(jnp.where((work == m) & (m > -jnp.inf), bidx, n_sel), axis=0, keepdims=True)
        pick = bidx == first
        sel = jnp.where(pick, 1.0, sel)
        work = jnp.where(pick, -jnp.inf, work)
    selb_ref[...] = jnp.where(sel > 0.0, 0.0, NEG)

    per_chunk = TQ // SEL_LEN

    @pl.when(i == 0)
    def _():
        kmax_ref[...] = jnp.full(kmax_ref.shape, jnp.maximum(_max_sq_norm(ks_ref), _max_sq_norm(kw_ref)),
                                 F32)

    bound = _score_bound(heads, kmax_ref[0:1, 0:1])

    def sel_bias(j, on):
        rows = [jnp.broadcast_to(selb_ref[pl.ds(j * per_chunk + bb, 1), :], (SEL_LEN, TQ))
                for bb in range(per_chunk)]
        return jnp.where(j * TQ + rowi <= qpos, jnp.concatenate(rows, axis=0), NEG) + on

    _masked_attention(0, i + 1, lambda j: (ks_ref[0, _chunk_slice(j), :], vsT_ref[0, j]), sel_bias,
                      heads, bound, acc_s_ref)

    def win_bias(j, on):
        diff = qpos - (j * TQ + rowi)
        return jnp.where((diff >= 0) & (diff < WINDOW), on, NEG)

    _masked_attention(jnp.maximum(i - WINDOW // TQ, 0), i + 1,
                      lambda j: (kw_ref[0, _chunk_slice(j), :], vwT_ref[0, j]), win_bias,
                      heads, bound, acc_w_ref)

    outs = []
    for h in range(N_HEADS):
        g = lambda br: gates[IDX_HEADS + 3 * h + br:IDX_HEADS + 3 * h + br + 1, :]
        outs.append(g(0) * o_cmp[h] + g(1) * _softmax_result(acc_s_ref, h)
                    + g(2) * _softmax_result(acc_w_ref, h))
    o_ref[0] = jnp.concatenate(outs, axis=0).T


def _nsa_attention(qc, wg, kc, vcT, ksel, vselT, kwin, vwinT, B, S):
    n_cmp = (S - CMP_LEN) // CMP_STRIDE + 1
    n_sel = S // SEL_LEN
    topn = min(SEL_TOPN, n_sel)
    ncp = kc.shape[1]
    cs = np.arange(ncp) * CMP_STRIDE
    ss = np.arange(n_sel) * SEL_LEN
    overlapT = ((cs[None, :] < ss[:, None] + SEL_LEN) & (cs[None, :] + CMP_LEN > ss[:, None])
                & (np.arange(ncp)[None, :] < n_cmp)).astype(np.float32)
    k_spec = pl.BlockSpec((1, S, HEAD_DIM), lambda b, i: (b, 0, 0))
    vT_spec = pl.BlockSpec((1, S // TQ, V_ROWS, TQ), lambda b, i: (b, 0, 0, 0))
    return pl.pallas_call(
        functools.partial(_nsa_kernel, n_cmp=n_cmp, n_sel=n_sel, topn=topn),
        grid=(B, S // TQ),
        in_specs=[pl.BlockSpec((1, N_HEADS, TQ, HEAD_DIM), lambda b, i: (b, 0, i, 0)),
                  pl.BlockSpec((1, 16, TQ), lambda b, i: (b, 0, i)),
                  pl.BlockSpec((1, ncp, HEAD_DIM), lambda b, i: (b, 0, 0)),
                  pl.BlockSpec((1, HEAD_DIM, ncp), lambda b, i: (b, 0, 0)),
                  _const_spec((n_sel, ncp)), k_spec, vT_spec, k_spec, vT_spec],
        out_specs=pl.BlockSpec((1, TQ, BRANCH_WIDTH), lambda b, i: (b, i, 0)),
        out_shape=jax.ShapeDtypeStruct((B, S, BRANCH_WIDTH), F32),
        scratch_shapes=[pltpu.VMEM((n_sel, TQ), F32), pltpu.VMEM((N_HEADS, V_ROWS, TQ), F32),
                        pltpu.VMEM((N_HEADS, V_ROWS, TQ), F32), pltpu.VMEM((SUBLANES, LANES), F32)],
        compiler_params=_params("parallel", "arbitrary"),
        name="nsa_attention",
    )(qc, wg, kc, vcT, jnp.asarray(overlapT, MXU_DTYPE), ksel, vselT, kwin, vwinT)


def _merge_kernel(h_ref, g_ref, oa_ref, ob_ref, oc_ref, wg_ref, wup_ref, wout_ref, o_ref):
    x = h_ref[...]
    xn = _rmsnorm(x, g_ref[...]).astype(MXU_DTYPE)
    merged = jnp.zeros((TM, D_MODEL), F32)
    for j, br_ref in enumerate((oa_ref, ob_ref, oc_ref)):
        gate = jax.nn.sigmoid(_mm(xn, wg_ref[:, j * D_MODEL:(j + 1) * D_MODEL]))
        merged = merged + gate * _mm(br_ref[...], wup_ref[j])
    o_ref[...] = x + _mm(merged, wout_ref[...])


def _merge(h, g, o_a, o_b, o_c, w_gm, w_up, w_out):
    T = h.shape[0]
    row = lambda w: pl.BlockSpec((TM, w), lambda i: (i, 0))
    return pl.pallas_call(
        _merge_kernel,
        grid=(T // TM,),
        in_specs=[row(D_MODEL), _const_spec((1, D_MODEL)), row(BRANCH_WIDTH), row(BRANCH_WIDTH),
                  row(BRANCH_WIDTH), _const_spec(w_gm.shape), _const_spec(w_up.shape),
                  _const_spec(w_out.shape)],
        out_specs=row(D_MODEL),
        out_shape=jax.ShapeDtypeStruct((T, D_MODEL), F32),
        compiler_params=_params("parallel"),
        name="merge",
    )(h, g.reshape(1, D_MODEL), o_a, o_b, o_c, w_gm.astype(MXU_DTYPE), w_up.astype(MXU_DTYPE),
      w_out.astype(MXU_DTYPE))


def _mem_kv_kernel(m_ref, g_ref, wk_ref, wvT_ref, ks_ref, vT_ref):
    xn = _rmsnorm(m_ref[0], g_ref[...]).astype(MXU_DTYPE)
    M = xn.shape[0]
    k = _mm(xn, wk_ref[...])
    lane_head = lax.broadcasted_iota(jnp.int32, (1, BRANCH_WIDTH), 1) // HEAD_DIM
    for h in range(N_HEADS):
        ks_ref[0, h * M:(h + 1) * M, :] = jnp.where(lane_head == h, k, 0.0).astype(ks_ref.dtype)
    vT_ref[0] = _mm_nt(wvT_ref[...], xn).astype(vT_ref.dtype)


def _mem_kv(mem, g, w_kv):
    B, M, _ = mem.shape
    wk = w_kv[:, :BRANCH_WIDTH].astype(MXU_DTYPE)
    wvT = w_kv[:, BRANCH_WIDTH:].T.astype(MXU_DTYPE)
    return pl.pallas_call(
        _mem_kv_kernel,
        grid=(B,),
        in_specs=[pl.BlockSpec((1, M, D_MODEL), lambda b: (b, 0, 0)), _const_spec((1, D_MODEL)),
                  _const_spec(wk.shape), _const_spec(wvT.shape)],
        out_specs=[pl.BlockSpec((1, N_HEADS * M, BRANCH_WIDTH), lambda b: (b, 0, 0)),
                   pl.BlockSpec((1, BRANCH_WIDTH, M), lambda b: (b, 0, 0))],
        out_shape=[jax.ShapeDtypeStruct((B, N_HEADS * M, BRANCH_WIDTH), MXU_DTYPE),
                   jax.ShapeDtypeStruct((B, BRANCH_WIDTH, M), MXU_DTYPE)],
        compiler_params=_params("parallel"),
        name="memory_kv",
    )(mem, g.reshape(1, D_MODEL), wk, wvT)


def _mem_attn_kernel(h_ref, g_ref, wq_ref, ks_ref, vT_ref, wo_ref, o_ref):
    x = h_ref[...]
    q = (_mm(_rmsnorm(x, g_ref[...]), wq_ref[...]) * SOFTMAX_Q_SCALE).astype(MXU_DTYPE)
    M = vT_ref.shape[2]
    s = _mm_nt(ks_ref[0], q)
    outs = []
    for h in range(N_HEADS):
        sh = s[h * M:(h + 1) * M, :]
        p = jnp.exp2(sh - jnp.max(sh, axis=0, keepdims=True))
        o_h = _mm(vT_ref[0, h * HEAD_DIM:(h + 1) * HEAD_DIM, :], p)
        outs.append(o_h / jnp.sum(p, axis=0, keepdims=True))
    o = jnp.concatenate(outs, axis=0).T
    o_ref[...] = x + _mm(o, wo_ref[...])


def _memory_attention(h, ks, vT, g_q, w_q, w_o, S):
    T = h.shape[0]
    row = pl.BlockSpec((TM, D_MODEL), lambda i: (i, 0))
    nb = S // TM
    return pl.pallas_call(
        _mem_attn_kernel,
        grid=(T // TM,),
        in_specs=[row, _const_spec((1, D_MODEL)), _const_spec(w_q.shape),
                  pl.BlockSpec((1,) + ks.shape[1:], lambda i: (i // nb, 0, 0)),
                  pl.BlockSpec((1,) + vT.shape[1:], lambda i: (i // nb, 0, 0)),
                  _const_spec(w_o.shape)],
        out_specs=row,
        out_shape=jax.ShapeDtypeStruct((T, D_MODEL), F32),
        compiler_params=_params("parallel"),
        name="memory_attention",
    )(h, g_q.reshape(1, D_MODEL), w_q.astype(MXU_DTYPE), ks, vT, w_o.astype(MXU_DTYPE))


FF_CHUNK = 256


def _swiglu_kernel(h_ref, g_ref, win_ref, wout_ref, gf_ref, o_ref, *, final_norm):
    x = h_ref[...]
    xn = _rmsnorm(x, g_ref[...]).astype(MXU_DTYPE)
    out = x
    for c in range(D_FF // FF_CHUNK):
        lo, hi = c * FF_CHUNK, (c + 1) * FF_CHUNK
        a = _mm(xn, win_ref[:, lo:hi])
        b = _mm(xn, win_ref[:, D_FF + lo:D_FF + hi])
        out = out + _mm(a * jax.nn.sigmoid(a) * b, wout_ref[lo:hi, :])
    if final_norm:
        out = _rmsnorm(out, gf_ref[...])
    o_ref[...] = out


def _swiglu(h, g, w_in, w_out, g_final, final_norm):
    T = h.shape[0]
    row = pl.BlockSpec((TM, D_MODEL), lambda i: (i, 0))
    return pl.pallas_call(
        functools.partial(_swiglu_kernel, final_norm=final_norm),
        grid=(T // TM,),
        in_specs=[row, _const_spec((1, D_MODEL)), _const_spec(w_in.shape), _const_spec(w_out.shape),
                  _const_spec((1, D_MODEL))],
        out_specs=row,
        out_shape=jax.ShapeDtypeStruct((T, D_MODEL), F32),
        compiler_params=_params("parallel"),
        name="swiglu",
    )(h, g.reshape(1, D_MODEL), w_in.astype(MXU_DTYPE), w_out.astype(MXU_DTYPE),
      g_final.reshape(1, D_MODEL))


def _rope_tables(S):
    inv = ROPE_THETA ** (-jnp.arange(0, HEAD_DIM, 2, dtype=F32) / HEAD_DIM)
    ang = jnp.arange(S, dtype=F32)[:, None] * inv[None, :]
    cos, sin = jnp.cos(ang), jnp.sin(ang)
    return (jnp.concatenate([cos, cos, cos, cos], axis=1),
            jnp.concatenate([-sin, sin, -sin, sin], axis=1))


def _hybrid_mixer(h, B, S, g, w_in, cmp_k, cmp_v, w_up, w_out, cos_t, sin_t):
    T = B * S
    (qa, ka, vaT, qb, qidx, qc, kb, kidx, ksel, kwin, vbT, vselT, vwinT, wg, ck, cv) = _in_projection(
        h, g, w_in, cos_t, sin_t, B, S)
    o_a = _stick_breaking(qa, ka, vaT)
    o_b = _dsa_attention(qidx, kidx, wg, qb, kb, vbT, B, S)
    kc = _compress(ck, *cmp_k, transpose_out=False)
    vcT = _compress(cv, *cmp_v, transpose_out=True)
    o_c = _nsa_attention(qc, wg, kc, vcT, ksel, vselT, kwin, vwinT, B, S)

    w_gm = w_in[:, _OFF['g_merge'][0]:]
    return _merge(h, g, o_a.reshape(T, 256), o_b.reshape(T, 256), o_c.reshape(T, 256), w_gm, w_up, w_out)


def kernel(x, mem, norm_mix, w_in, cmp_pos_k, cmp_w1_k, cmp_w2_k, cmp_pos_v, cmp_w1_v, cmp_w2_v,
           w_up, w_out, norm_mem_q, norm_mem_kv, w_mem_q, w_mem_kv, w_mem_o,
           norm_ffn, w_ffn_in, w_ffn_out, norm_final):
    B, S, D = x.shape
    depth = w_in.shape[0]
    assert D == D_MODEL and S % TM == 0 and S % TQ == 0 and WINDOW % TQ == 0
    cos_t, sin_t = _rope_tables(S)
    h = x.reshape(B * S, D)
    for l in range(depth):
        h = _hybrid_mixer(h, B, S, norm_mix[l], w_in[l],
                          (cmp_pos_k[l], cmp_w1_k[l], cmp_w2_k[l]),
                          (cmp_pos_v[l], cmp_w1_v[l], cmp_w2_v[l]), w_up[l], w_out[l], cos_t, sin_t)
        ks, vT = _mem_kv(mem, norm_mem_kv[l], w_mem_kv[l])
        h = _memory_attention(h, ks, vT, norm_mem_q[l], w_mem_q[l], w_mem_o[l], S)
        h = _swiglu(h, norm_ffn[l], w_ffn_in[l], w_ffn_out[l], norm_final, l == depth - 1)
    return h.reshape(B, S, D)
```

```python
import functools

import jax
import jax.numpy as jnp
import numpy as np
from jax import lax
from jax.experimental import pallas as pl
from jax.experimental.pallas import tpu as pltpu

F32 = jnp.float32
MXU_DTYPE = jnp.bfloat16

D_MODEL = 1024
HEAD_DIM = 64
N_HEADS = 4
BRANCH_WIDTH = N_HEADS * HEAD_DIM
IDX_HEADS = 4
DSA_TOPK = 256
CMP_LEN = 32
CMP_STRIDE = 16
CMP_HIDDEN = 128
SEL_LEN = 64
SEL_TOPN = 16
WINDOW = 512
D_FF = 2816
ROPE_THETA = 10000.0
RMS_EPS = 1e-6
FORCED_SCORE = 1e4
Q_SCALE = HEAD_DIM ** -0.5
LOG2E = 1.4426950408889634
SOFTMAX_Q_SCALE = Q_SCALE * LOG2E
W_IDX_SCALE = (IDX_HEADS * HEAD_DIM) ** -0.5
NEG = -1e30

LANES = 128
SUBLANES = 8
VMEM_LIMIT = 56 * 1024 * 1024
TM = 512
TQ = 256

_OFF = {}
_o = 0
for _n, _w in (('q_a', 256), ('k_a', 256), ('v_a', 256), ('q_b', 256), ('k_b', 64), ('v_b', 64),
               ('q_idx', 256), ('k_idx', 64), ('w_idx', 4), ('q_c', 256), ('k_cmp', 64),
               ('v_cmp', 64), ('k_sel', 64), ('v_sel', 64), ('k_win', 64), ('v_win', 64),
               ('g_c', 12), ('g_merge', 3 * D_MODEL)):
    _OFF[_n] = (_o, _o + _w)
    _o += _w
D_IN = _o


def _mm(a, b):
    return jnp.dot(a.astype(MXU_DTYPE), b.astype(MXU_DTYPE), preferred_element_type=F32)


def _mm_nt(a, b):
    return lax.dot_general(a.astype(MXU_DTYPE), b.astype(MXU_DTYPE),
                           (((1,), (1,)), ((), ())), preferred_element_type=F32)


def _params(*sem):
    return pltpu.CompilerParams(dimension_semantics=sem, vmem_limit_bytes=VMEM_LIMIT)


def _const_spec(shape):
    nd = len(shape)
    return pl.BlockSpec(shape, lambda *_: (0,) * nd, pipeline_mode=pl.Buffered(1))


def _rmsnorm(x, g):
    return x * lax.rsqrt(jnp.mean(x * x, axis=-1, keepdims=True) + RMS_EPS) * g


def _rope_lanes(x, cos, sin):
    lane = lax.broadcasted_iota(jnp.int32, (1, LANES), 1)
    first_half = (lane % HEAD_DIM) < (HEAD_DIM // 2)
    out = []
    for g in range(x.shape[1] // LANES):
        xg = x[:, g * LANES:(g + 1) * LANES]
        rot = jnp.where(first_half, pltpu.roll(xg, LANES - HEAD_DIM // 2, 1),
                        pltpu.roll(xg, HEAD_DIM // 2, 1))
        out.append(xg * cos + rot * sin)
    return out


WG_ROW = BRANCH_WIDTH + 3 * HEAD_DIM
WG_ROWS = IDX_HEADS + 3 * N_HEADS
WT_ROWS = 512
V_ROWS = HEAD_DIM + 2 * SUBLANES


def _inproj_kernel(h_ref, g_ref, cos_ref, sin_ref, wa_ref, wr_ref, wc_ref, wt_ref,
                   qa_ref, ka_ref, vaT_ref, qb_ref, qi_ref, qc_ref, kb_ref, ki_ref, ks_ref, kw_ref,
                   vbT_ref, vsT_ref, vwT_ref, wg_ref, ck_ref, cv_ref):
    xn = _rmsnorm(h_ref[...], g_ref[...]).astype(MXU_DTYPE)
    cos, sin = cos_ref[...], sin_ref[...]
    head = lambda y, h: y[:, h * HEAD_DIM:(h + 1) * HEAD_DIM]

    ya = _mm(xn, wa_ref[...])
    for h in range(N_HEADS):
        qa_ref[0, h] = (head(ya, h) * Q_SCALE).astype(qa_ref.dtype)
        ka_ref[0, h] = head(ya, N_HEADS + h).astype(ka_ref.dtype)

    yr = _rope_lanes(_mm(xn, wr_ref[...]), cos, sin)
    for g, (q_ref, scale) in enumerate(((qb_ref, SOFTMAX_Q_SCALE), (qi_ref, 1.0),
                                        (qc_ref, SOFTMAX_Q_SCALE))):
        for half in range(2):
            og = yr[2 * g + half] * scale
            q_ref[0, 2 * half] = head(og, 0).astype(q_ref.dtype)
            q_ref[0, 2 * half + 1] = head(og, 1).astype(q_ref.dtype)
    kb_ref[0] = head(yr[6], 0).astype(kb_ref.dtype)
    ki_ref[0] = head(yr[6], 1).astype(ki_ref.dtype)
    ks_ref[0] = head(yr[7], 0).astype(ks_ref.dtype)
    kw_ref[0] = head(yr[7], 1).astype(kw_ref.dtype)

    yc = _mm(xn, wc_ref[...])
    ck_ref[0] = head(_rope_lanes(yc, cos, sin)[0], 0)
    cv_ref[0] = head(yc, 1)

    yT = _mm_nt(wt_ref[...], xn)
    ones_row = (lax.broadcasted_iota(jnp.int32, (V_ROWS - HEAD_DIM, TQ), 0) == 0).astype(vbT_ref.dtype)
    for c in range(TM // TQ):
        cols = slice(c * TQ, (c + 1) * TQ)
        for h in range(N_HEADS):
            vaT_ref[0, h, c] = yT[h * HEAD_DIM:(h + 1) * HEAD_DIM, cols].astype(vaT_ref.dtype)
        for n, v_ref in enumerate((vbT_ref, vsT_ref, vwT_ref)):
            lo = BRANCH_WIDTH + n * HEAD_DIM
            v_ref[0, c, 0:HEAD_DIM, :] = yT[lo:lo + HEAD_DIM, cols].astype(v_ref.dtype)
            v_ref[0, c, HEAD_DIM:, :] = ones_row
    ys = yT[WG_ROW:WG_ROW + WG_ROWS, :]
    row = lax.broadcasted_iota(jnp.int32, ys.shape, 0)
    wg_ref[0] = jnp.where(row < IDX_HEADS, ys * W_IDX_SCALE, jax.nn.sigmoid(ys))


def _in_projection(h, g, w_in, cos_t, sin_t, B, S):
    cols = lambda *names: jnp.concatenate([w_in[:, _OFF[n][0]:_OFF[n][1]] for n in names], axis=1)
    wa = cols('q_a', 'k_a').astype(MXU_DTYPE)
    wr = cols('q_b', 'q_idx', 'q_c', 'k_b', 'k_idx', 'k_sel', 'k_win').astype(MXU_DTYPE)
    wc = cols('k_cmp', 'v_cmp').astype(MXU_DTYPE)
    wt = cols('v_a', 'v_b', 'v_sel', 'v_win', 'w_idx', 'g_c')
    wt = jnp.pad(wt, ((0, 0), (0, WT_ROWS - wt.shape[1]))).T.astype(MXU_DTYPE)
    nb, nc, cpb = S // TM, S // TQ, TM // TQ
    bf = lambda *shape: jax.ShapeDtypeStruct(shape, MXU_DTYPE)
    f32 = lambda *shape: jax.ShapeDtypeStruct(shape, F32)
    heads_spec = pl.BlockSpec((1, N_HEADS, TM, HEAD_DIM), lambda i: (i // nb, 0, i % nb, 0))
    tok_spec = pl.BlockSpec((1, TM, HEAD_DIM), lambda i: (i // nb, i % nb, 0))
    vT_spec = pl.BlockSpec((1, cpb, V_ROWS, TQ), lambda i: (i // nb, i % nb, 0, 0))
    pos = pl.BlockSpec((TM, LANES), lambda i: (i % nb, 0))
    q4, k3, v4 = bf(B, N_HEADS, S, HEAD_DIM), bf(B, S, HEAD_DIM), bf(B, nc, V_ROWS, TQ)
    return pl.pallas_call(
        _inproj_kernel,
        grid=(B * nb,),
        in_specs=[pl.BlockSpec((TM, D_MODEL), lambda i: (i, 0)), _const_spec((1, D_MODEL)), pos, pos,
                  _const_spec(wa.shape), _const_spec(wr.shape), _const_spec(wc.shape),
                  _const_spec(wt.shape)],
        out_specs=[heads_spec, heads_spec,
                   pl.BlockSpec((1, N_HEADS, cpb, HEAD_DIM, TQ), lambda i: (i // nb, 0, i % nb, 0, 0)),
                   heads_spec, heads_spec, heads_spec, tok_spec, tok_spec, tok_spec, tok_spec,
                   vT_spec, vT_spec, vT_spec,
                   pl.BlockSpec((1, WG_ROWS, TM), lambda i: (i // nb, 0, i % nb)), tok_spec, tok_spec],
        out_shape=[q4, q4, bf(B, N_HEADS, nc, HEAD_DIM, TQ), q4, q4, q4, k3, k3, k3, k3, v4, v4, v4,
                   f32(B, WG_ROWS, S), f32(B, S, HEAD_DIM), f32(B, S, HEAD_DIM)],
        compiler_params=_params("parallel"),
        name="in_projection",
    )(h, g.reshape(1, D_MODEL), cos_t, sin_t, wa, wr, wc, wt)


def _chunk_slice(j):
    return pl.ds(pl.multiple_of(j * TQ, TQ), TQ)


EXP_UNDERFLOW = 110.0
SCORE_BOUND_SLACK = 1.05


def _sb_kernel(q_ref, k_ref, vT_ref, tri_ref, o_ref, acc_ref, kmax_ref):
    i = pl.program_id(1)
    row = lax.broadcasted_iota(jnp.int32, (TQ, TQ), 0)
    col = lax.broadcasted_iota(jnp.int32, (TQ, TQ), 1)
    strict = row < col
    tri = tri_ref[...]
    qs = [q_ref[0, h] for h in range(N_HEADS)]
    acc_ref[...] = jnp.zeros(acc_ref.shape, F32)

    def chunks(js, cs, diag=False):
        heads = range(N_HEADS)
        zs = [[_mm_nt(k_ref[0, h, _chunk_slice(j), :], qs[h]) for h in heads] for j in js]
        cums = []
        for zj in zs:
            cums.append([])
            for z in zj:
                sp = jnp.maximum(z, 0.0) + jnp.log(1.0 + jnp.exp(-jnp.abs(z)))
                if diag:
                    sp = jnp.where(strict, sp, 0.0)
                hi = sp.astype(MXU_DTYPE)
                lo = (sp - hi.astype(F32)).astype(MXU_DTYPE)
                cums[-1].append(_mm(tri, hi) + _mm(tri, lo))
        outs = [jnp.zeros((HEAD_DIM, TQ), F32) for _ in heads]
        cs = list(cs)
        for n, j in enumerate(js):
            for h in heads:
                w = jnp.exp(zs[n][h] - cums[n][h] - cs[h])
                if diag:
                    w = jnp.where(strict, w, 0.0)
                outs[h] = outs[h] + _mm(vT_ref[0, h, j], w)
                cs[h] = cs[h] + cums[n][h][0:1, :]
        for h in heads:
            acc_ref[h] += outs[h]
        return tuple(cs)

    ones = jnp.ones((SUBLANES, HEAD_DIM), F32)

    def sq_norms(x):
        xf = x.astype(F32)
        return _mm_nt(ones, xf * xf)[0:1, :]

    @pl.when(i == 0)
    def _():
        for h in range(N_HEADS):
            def body(j, m):
                return jnp.maximum(m, sq_norms(k_ref[0, h, _chunk_slice(j), :]))
            m = lax.fori_loop(0, k_ref.shape[2] // TQ, body, jnp.zeros((1, TQ), F32))
            kmax_ref[h] = jnp.full((SUBLANES, LANES), jnp.max(m), F32)

    zbound = [jnp.sqrt(sq_norms(qs[h]) * kmax_ref[h][0:1, 0:1]) * SCORE_BOUND_SLACK for h in range(N_HEADS)]

    def negligible(cs):
        slack = cs[0] - zbound[0]
        for h in range(1, N_HEADS):
            slack = jnp.minimum(slack, cs[h] - zbound[h])
        return (jnp.min(slack) > EXP_UNDERFLOW).astype(jnp.int32)

    cs = chunks([i], tuple(jnp.zeros((1, TQ), F32) for _ in range(N_HEADS)), diag=True)

    def more(carry):
        j, done, _ = carry
        return (j >= 0) & (done == 0)

    def step(carry):
        j, _, cs = carry
        cs = chunks([j], cs)
        return j - 1, negligible(cs), cs

    lax.while_loop(more, step, (i - 1, negligible(cs), cs))
    o_ref[0] = acc_ref[...].reshape(BRANCH_WIDTH, TQ).T


def _stick_breaking(q, k, vT):
    B, H, S, _ = q.shape
    tri = jnp.asarray(np.triu(np.ones((TQ, TQ), np.float32)), MXU_DTYPE)
    return pl.pallas_call(
        _sb_kernel,
        grid=(B, S // TQ),
        in_specs=[pl.BlockSpec((1, H, TQ, HEAD_DIM), lambda b, i: (b, 0, i, 0)),
                  pl.BlockSpec((1, H, S, HEAD_DIM), lambda b, i: (b, 0, 0, 0)),
                  pl.BlockSpec((1, H, S // TQ, HEAD_DIM, TQ), lambda b, i: (b, 0, 0, 0, 0)),
                  _const_spec((TQ, TQ))],
        out_specs=pl.BlockSpec((1, TQ, BRANCH_WIDTH), lambda b, i: (b, i, 0)),
        out_shape=jax.ShapeDtypeStruct((B, S, BRANCH_WIDTH), F32),
        scratch_shapes=[pltpu.VMEM((H, HEAD_DIM, TQ), F32), pltpu.VMEM((H, SUBLANES, LANES), F32)],
        compiler_params=_params("parallel", "arbitrary"),
        name="stick_breaking",
    )(q, k, vT, tri)


PACKED_ROWS = 2 * SUBLANES


def _softmax_step(ks, qs, biases, vTs, ms, acc_ref):
    heads = range(len(qs))
    ss = [[_mm_nt(k, q) + b for q in qs] for k, b in zip(ks, biases)]
    m_new = []
    for h in heads:
        m = ms[h]
        for st in ss:
            m = jnp.maximum(m, jnp.max(st[h], axis=0, keepdims=True))
        m_new.append(m)
    pvs = []
    for h in heads:
        pv = jnp.zeros((V_ROWS, TQ), F32)
        for st, vT in zip(ss, vTs):
            pv = pv + _mm(vT, jnp.exp2(st[h] - m_new[h]))
        pvs.append(pv)
    for h in heads:
        acc_ref[h] = jnp.exp2(ms[h] - m_new[h]) * acc_ref[h] + pvs[h]
    return tuple(m_new)


def _for_chunks(lo, hi, step, ms):
    odd = (hi - lo) % 2
    ms = lax.cond(odd == 1, lambda ms: step([lo], ms), lambda ms: ms, ms)
    first = lo + odd
    return lax.fori_loop(0, (hi - lo) // 2,
                         lambda t, ms: step([first + 2 * t, first + 2 * t + 1], ms), ms)


def _softmax_result(acc_ref, h):
    return acc_ref[h, 0:HEAD_DIM, :] / acc_ref[h, HEAD_DIM:HEAD_DIM + 1, :]


MAX_FIXED_SHIFT = 45.0


def _sq_norms(x):
    xf = x.astype(F32)
    return _mm_nt(jnp.ones((SUBLANES, HEAD_DIM), F32), xf * xf)[0:1, :]


def _max_sq_norm(k_ref):
    def body(j, m):
        return jnp.maximum(m, _sq_norms(k_ref[0, _chunk_slice(j), :]))
    return jnp.max(lax.fori_loop(0, k_ref.shape[1] // TQ, body, jnp.zeros((1, TQ), F32)))


def _score_bound(qs, kmax2):
    qn2 = _sq_norms(qs[0])
    for q in qs[1:]:
        qn2 = jnp.maximum(qn2, _sq_norms(q))
    return jnp.sqrt(qn2 * kmax2) * SCORE_BOUND_SLACK


def _masked_attention(lo, hi, load, bias_fn, qs, bound, acc_ref):
    acc_ref[...] = jnp.zeros(acc_ref.shape, F32)
    fixed = jnp.max(bound) < MAX_FIXED_SHIFT

    @pl.when(fixed)
    def _():
        def step(js, carry):
            tiles = [load(j) for j in js]
            biases = [bias_fn(j, -bound) for j in js]
            ss = [[_mm_nt(k, q) + b for q in qs] for (k, _), b in zip(tiles, biases)]
            for h in range(len(qs)):
                pv = jnp.zeros((V_ROWS, TQ), F32)
                for st, (_, vT) in zip(ss, tiles):
                    pv = pv + _mm(vT, jnp.exp2(st[h]))
                acc_ref[h] += pv
            return carry

        _for_chunks(lo, hi, step, 0)

    @pl.when(jnp.logical_not(fixed))
    def _():
        def step(js, ms):
            tiles = [load(j) for j in js]
            return _softmax_step([k for k, _ in tiles], qs, [bias_fn(j, 0.0) for j in js],
                                 [vT for _, vT in tiles], ms, acc_ref)

        _for_chunks(lo, hi, step, tuple(jnp.full((1, TQ), NEG, F32) for _ in qs))


def _dsa_kernel(qi_ref, ki_ref, wg_ref, q_ref, k_ref, vT_ref, tril_ref, o_ref, sc_ref, hi_ref, lo_ref,
                acc_ref, kmax_ref, *, topk):
    i = pl.program_id(1)
    nchunks = i + 1
    npairs = (i + 2) // 2
    qpos = i * TQ + lax.broadcasted_iota(jnp.int32, (1, TQ), 1)
    rowi = lax.broadcasted_iota(jnp.int32, (TQ, TQ), 0)
    int_min = jnp.int32(-2 ** 31)

    qis = [qi_ref[0, hh] for hh in range(IDX_HEADS)]
    wT = wg_ref[0]

    def idx_chunks(js, carry):
        rs = [[_mm_nt(ki_ref[0, _chunk_slice(j), :], qh) for qh in qis] for j in js]
        for j, rj in zip(js, rs):
            acc = jnp.zeros((TQ, TQ), F32)
            for hh in range(IDX_HEADS):
                acc = acc + wT[hh:hh + 1, :] * jnp.maximum(rj[hh], 0.0)
            sc = jnp.where(j * TQ + rowi <= qpos, acc, -jnp.inf)
            sc_ref[_chunk_slice(j), :] = sc
            hi_ref[_chunk_slice(j), :] = high_half(sc)
            lo_ref[_chunk_slice(j), :] = low_half(float_to_key(sc))
        return carry

    def float_to_key(x):
        bits = lax.bitcast_convert_type(x, jnp.int32)
        return bits ^ (jnp.right_shift(bits, 31) & jnp.int32(0x7FFFFFFF))

    def low_half(key):
        return ((key & jnp.int32(0xFFFF)) - 32768).astype(jnp.int16)

    def high_half(x):
        bits = lax.bitcast_convert_type(x, jnp.int32) & jnp.int32(-65536)
        return lax.bitcast_convert_type(bits, F32).astype(jnp.bfloat16)

    _for_chunks(0, nchunks, idx_chunks, 0)

    @pl.when(nchunks % 2 == 1)
    def _():
        sc_ref[_chunk_slice(nchunks), :] = jnp.full((TQ, TQ), -jnp.inf, F32)
        hi_ref[_chunk_slice(nchunks), :] = jnp.full((TQ, TQ), -jnp.inf, jnp.bfloat16)
        lo_ref[_chunk_slice(nchunks), :] = jnp.full((TQ, TQ), -32768, jnp.int16)

    def count_packed(ref, thr16):
        one, zero = jnp.ones((), jnp.bfloat16), jnp.zeros((), jnp.bfloat16)

        def body(j, acc):
            for r in range(2):
                blk = ref[pl.ds(pl.multiple_of(j * 2 * TQ + r * TQ, TQ), TQ), :]
                ind = jnp.where(blk >= thr16, one, zero)
                parts = [ind[g * PACKED_ROWS:(g + 1) * PACKED_ROWS] for g in range(TQ // PACKED_ROWS)]
                while len(parts) > 1:
                    parts = [a + b for a, b in zip(parts[0::2], parts[1::2])]
                acc = acc + parts[0].astype(F32)
            return acc
        return jnp.sum(lax.fori_loop(0, npairs, body, jnp.zeros((PACKED_ROWS, TQ), F32)),
                       axis=0, keepdims=True)

    take_all = qpos < topk

    def search():
        def key_to_float(u):
            s = u ^ int_min
            return lax.bitcast_convert_type(jnp.where(s < 0, s ^ jnp.int32(0x7FFFFFFF), s), F32)

        def bit_body(t, carry, count_ge):
            res, n_ge, n_gt = carry
            cand = res | jnp.left_shift(jnp.int32(1), 31 - t)
            n = count_ge(cand)
            ok = n >= topk
            return jnp.where(ok, cand, res), jnp.where(ok, n, n_ge), jnp.where(ok, n_gt, n)

        zero = jnp.zeros((1, TQ), F32)
        carry = (jnp.zeros((1, TQ), jnp.int32), zero, zero)
        res, n_ge, n_above = lax.fori_loop(
            0, 16, functools.partial(
                bit_body, count_ge=lambda cand: count_packed(hi_ref, high_half(key_to_float(cand)))), carry)

        bucket = high_half(key_to_float(res))

        def keep_bucket(j, _):
            sl = pl.ds(pl.multiple_of(j * 2 * TQ, 2 * TQ), 2 * TQ)
            lo_ref[sl, :] = jnp.where(hi_ref[sl, :] == bucket, lo_ref[sl, :], jnp.int16(-32768))
            return 0

        lax.fori_loop(0, npairs, keep_bucket, 0)
        res, n_ge, n_gt = lax.fori_loop(
            16, 32, functools.partial(
                bit_body, count_ge=lambda cand: n_above + count_packed(lo_ref, low_half(cand))),
            (res, n_ge, n_above))
        thr = key_to_float(res)

        @pl.when(jnp.max(jnp.where(take_all, 0.0, n_ge)) > topk)
        def _():
            need = jnp.where(take_all, float(2 ** 30), topk - n_gt)
            tril = tril_ref[...]

            def drop(js, before):
                blks = [sc_ref[_chunk_slice(j), :] for j in js]
                counts = [_mm(tril, jnp.where(blk == thr, 1.0, 0.0)) for blk in blks]
                for j, blk, cnt in zip(js, blks, counts):
                    rank = cnt + before
                    sc_ref[_chunk_slice(j), :] = jnp.where((blk == thr) & (rank > need), -jnp.inf, blk)
                    before = rank[TQ - 1:TQ, :]
                return before

            _for_chunks(0, nchunks, drop, zero)

        return thr

    lowest = float(np.finfo(np.float32).min)
    thr = lax.cond(i * TQ + TQ > topk, search, lambda: jnp.full((1, TQ), lowest, F32))
    thr = jnp.where(take_all, lowest, thr)

    @pl.when(i == 0)
    def _():
        kmax_ref[...] = jnp.full(kmax_ref.shape, _max_sq_norm(k_ref), F32)

    qs = [q_ref[0, h] for h in range(N_HEADS)]
    _masked_attention(0, nchunks,
                      lambda j: (k_ref[0, _chunk_slice(j), :], vT_ref[0, j]),
                      lambda j, on: jnp.where(sc_ref[_chunk_slice(j), :] >= thr, on, NEG),
                      qs, _score_bound(qs, kmax_ref[0:1, 0:1]), acc_ref)
    o_ref[0] = jnp.concatenate([_softmax_result(acc_ref, h) for h in range(N_HEADS)], axis=0).T


def _dsa_attention(qidx, kidx, wg, qb, kb, vbT, B, S):
    topk = min(DSA_TOPK, S // 4)
    assert topk <= TQ
    nc = S // TQ
    q_spec = pl.BlockSpec((1, N_HEADS, TQ, HEAD_DIM), lambda b, i: (b, 0, i, 0))
    k_spec = pl.BlockSpec((1, S, HEAD_DIM), lambda b, i: (b, 0, 0))
    tril = jnp.asarray(np.tril(np.ones((TQ, TQ), np.float32)), MXU_DTYPE)
    return pl.pallas_call(
        functools.partial(_dsa_kernel, topk=topk),
        grid=(B, nc),
        in_specs=[q_spec, k_spec, pl.BlockSpec((1, WG_ROWS, TQ), lambda b, i: (b, 0, i)), q_spec, k_spec,
                  pl.BlockSpec((1, nc, V_ROWS, TQ), lambda b, i: (b, 0, 0, 0)), _const_spec((TQ, TQ))],
        out_specs=pl.BlockSpec((1, TQ, BRANCH_WIDTH), lambda b, i: (b, i, 0)),
        out_shape=jax.ShapeDtypeStruct((B, S, BRANCH_WIDTH), F32),
        scratch_shapes=[pltpu.VMEM(((nc + nc % 2) * TQ, TQ), F32),
                        pltpu.VMEM(((nc + nc % 2) * TQ, TQ), jnp.bfloat16),
                        pltpu.VMEM(((nc + nc % 2) * TQ, TQ), jnp.int16),
                        pltpu.VMEM((N_HEADS, V_ROWS, TQ), F32), pltpu.VMEM((SUBLANES, LANES), F32)],
        compiler_params=_params("parallel", "arbitrary"),
        name="dsa_attention",
    )(qidx, kidx, wg, qb, kb, vbT, tril)


def _compress_kernel(c_ref, pos_ref, w1_ref, w2_ref, o_ref, *, transpose_out):
    n = c_ref.shape[1] // CMP_STRIDE
    top = jnp.zeros((n, CMP_HIDDEN), F32)
    bot = jnp.zeros((n, CMP_HIDDEN), F32)
    for p in range(CMP_STRIDE):
        x = c_ref[0, pl.ds(p, n, stride=CMP_STRIDE), :]
        q = p + CMP_STRIDE
        top = top + _mm(x + pos_ref[p:p + 1, :], w1_ref[p * HEAD_DIM:(p + 1) * HEAD_DIM, :])
        bot = bot + _mm(x + pos_ref[q:q + 1, :], w1_ref[q * HEAD_DIM:(q + 1) * HEAD_DIM, :])
    hid = jax.nn.gelu(top + pltpu.roll(bot, n - 1, 0), approximate=True)
    out = _mm(hid, w2_ref[...])
    rowi = lax.broadcasted_iota(jnp.int32, out.shape, 0)
    out = jnp.where(rowi < n - 1, out, 0.0)
    if transpose_out:
        out = jnp.concatenate([out, jnp.zeros_like(out)], axis=1).T[0:HEAD_DIM, :]
    o_ref[0] = out.astype(o_ref.dtype)


def _compress(tok, pos_emb, w1, w2, transpose_out):
    B, S, _ = tok.shape
    assert CMP_LEN == 2 * CMP_STRIDE
    nch = S // CMP_STRIDE
    out_block = (1, HEAD_DIM, nch) if transpose_out else (1, nch, HEAD_DIM)
    return pl.pallas_call(
        functools.partial(_compress_kernel, transpose_out=transpose_out),
        grid=(B,),
        in_specs=[pl.BlockSpec((1, S, HEAD_DIM), lambda b: (b, 0, 0)),
                  _const_spec(pos_emb.shape), _const_spec(w1.shape), _const_spec(w2.shape)],
        out_specs=pl.BlockSpec(out_block, lambda b: (b, 0, 0)),
        out_shape=jax.ShapeDtypeStruct((B,) + out_block[1:], MXU_DTYPE),
        compiler_params=_params("parallel"),
        name="nsa_compress",
    )(tok, pos_emb, w1.astype(MXU_DTYPE), w2.astype(MXU_DTYPE))


def _nsa_kernel(q_ref, wg_ref, kc_ref, vcT_ref, ovT_ref, ks_ref, vsT_ref, kw_ref, vwT_ref,
                o_ref, selb_ref, acc_s_ref, acc_w_ref, kmax_ref, *, n_cmp, n_sel, topn):
    i = pl.program_id(1)
    qpos = i * TQ + lax.broadcasted_iota(jnp.int32, (1, TQ), 1)
    rowi = lax.broadcasted_iota(jnp.int32, (TQ, TQ), 0)
    heads = [q_ref[0, h] for h in range(N_HEADS)]
    gates = wg_ref[0]

    ncp = kc_ref.shape[1]
    cidx = lax.broadcasted_iota(jnp.int32, (ncp, TQ), 0)
    cmask = (cidx * CMP_STRIDE + (CMP_LEN - 1) <= qpos) & (cidx < n_cmp)
    kc, vcT, ovT = kc_ref[0], vcT_ref[0], ovT_ref[...]
    ss = [jnp.where(cmask, _mm_nt(kc, q), NEG) for q in heads]
    ps = []
    for s in ss:
        m = jnp.max(s, axis=0, keepdims=True)
        e = jnp.exp2(s - m)
        ps.append(e * jnp.where(m > 0.5 * NEG, 1.0 / jnp.sum(e, axis=0, keepdims=True), 0.0))
    o_cmp = [_mm(vcT, p) for p in ps]
    imp = jnp.zeros((n_sel, TQ), F32)
    for p in ps:
        p_hi = p.astype(MXU_DTYPE)
        imp = imp + _mm(ovT, p_hi) + _mm(ovT, p - p_hi.astype(F32))

    bidx = lax.broadcasted_iota(jnp.int32, (n_sel, TQ), 0)
    cur = qpos // SEL_LEN
    forced = (bidx == 0) | (bidx == cur) | (bidx == cur - 1)
    work = jnp.where(bidx * SEL_LEN <= qpos, jnp.where(forced, FORCED_SCORE, imp), -jnp.inf)
    sel = jnp.zeros((n_sel, TQ), F32)
    for _ in range(topn):
        m = jnp.max(work, axis=0, keepdims=True)
        first = jnp.min(jnp.where((work == m) & (m > -jnp.inf), bidx, n_sel), axis=0, keepdims=True)
        pick = bidx == first
        sel = jnp.where(pick, 1.0, sel)
        work = jnp.where(pick, -jnp.inf, work)
    selb_ref[...] = jnp.where(sel > 0.0, 0.0, NEG)

    per_chunk = TQ // SEL_LEN

    @pl.when(i == 0)
    def _():
        kmax_ref[...] = jnp.full(kmax_ref.shape, jnp.maximum(_max_sq_norm(ks_ref), _max_sq_norm(kw_ref)),
                                 F32)

    bound = _score_bound(heads, kmax_ref[0:1, 0:1])

    def sel_bias(j, on):
        rows = [jnp.broadcast_to(selb_ref[pl.ds(j * per_chunk + bb, 1), :], (SEL_LEN, TQ))
                for bb in range(per_chunk)]
        return jnp.where(j * TQ + rowi <= qpos, jnp.concatenate(rows, axis=0), NEG) + on

    _masked_attention(0, i + 1, lambda j: (ks_ref[0, _chunk_slice(j), :], vsT_ref[0, j]), sel_bias,
                      heads, bound, acc_s_ref)

    def win_bias(j, on):
        diff = qpos - (j * TQ + rowi)
        return jnp.where((diff >= 0) & (diff < WINDOW), on, NEG)

    _masked_attention(jnp.maximum(i - WINDOW // TQ, 0), i + 1,
                      lambda j: (kw_ref[0, _chunk_slice(j), :], vwT_ref[0, j]), win_bias,
                      heads, bound, acc_w_ref)

    outs = []
    for h in range(N_HEADS):
        g = lambda br: gates[IDX_HEADS + 3 * h + br:IDX_HEADS + 3 * h + br + 1, :]
        outs.append(g(0) * o_cmp[h] + g(1) * _softmax_result(acc_s_ref, h)
                    + g(2) * _softmax_result(acc_w_ref, h))
    o_ref[0] = jnp.concatenate(outs, axis=0).T


def _nsa_attention(qc, wg, kc, vcT, ksel, vselT, kwin, vwinT, B, S):
    n_cmp = (S - CMP_LEN) // CMP_STRIDE + 1
    n_sel = S // SEL_LEN
    topn = min(SEL_TOPN, n_sel)
    ncp = kc.shape[1]
    cs = np.arange(ncp) * CMP_STRIDE
    ss = np.arange(n_sel) * SEL_LEN
    overlapT = ((cs[None, :] < ss[:, None] + SEL_LEN) & (cs[None, :] + CMP_LEN > ss[:, None])
                & (np.arange(ncp)[None, :] < n_cmp)).astype(np.float32)
    k_spec = pl.BlockSpec((1, S, HEAD_DIM), lambda b, i: (b, 0, 0))
    vT_spec = pl.BlockSpec((1, S // TQ, V_ROWS, TQ), lambda b, i: (b, 0, 0, 0))
    return pl.pallas_call(
        functools.partial(_nsa_kernel, n_cmp=n_cmp, n_sel=n_sel, topn=topn),
        grid=(B, S // TQ),
        in_specs=[pl.BlockSpec((1, N_HEADS, TQ, HEAD_DIM), lambda b, i: (b, 0, i, 0)),
                  pl.BlockSpec((1, WG_ROWS, TQ), lambda b, i: (b, 0, i)),
                  pl.BlockSpec((1, ncp, HEAD_DIM), lambda b, i: (b, 0, 0)),
                  pl.BlockSpec((1, HEAD_DIM, ncp), lambda b, i: (b, 0, 0)),
                  _const_spec((n_sel, ncp)), k_spec, vT_spec, k_spec, vT_spec],
        out_specs=pl.BlockSpec((1, TQ, BRANCH_WIDTH), lambda b, i: (b, i, 0)),
        out_shape=jax.ShapeDtypeStruct((B, S, BRANCH_WIDTH), F32),
        scratch_shapes=[pltpu.VMEM((n_sel, TQ), F32), pltpu.VMEM((N_HEADS, V_ROWS, TQ), F32),
                        pltpu.VMEM((N_HEADS, V_ROWS, TQ), F32), pltpu.VMEM((SUBLANES, LANES), F32)],
        compiler_params=_params("parallel", "arbitrary"),
        name="nsa_attention",
    )(qc, wg, kc, vcT, jnp.asarray(overlapT, MXU_DTYPE), ksel, vselT, kwin, vwinT)


def _mem_kv_kernel(m_ref, g_ref, wk_ref, wvT_ref, ks_ref, vT_ref):
    xn = _rmsnorm(m_ref[0], g_ref[...]).astype(MXU_DTYPE)
    M = xn.shape[0]
    k = _mm(xn, wk_ref[...])
    lane_head = lax.broadcasted_iota(jnp.int32, (1, BRANCH_WIDTH), 1) // HEAD_DIM
    for h in range(N_HEADS):
        ks_ref[0, h * M:(h + 1) * M, :] = jnp.where(lane_head == h, k, 0.0).astype(ks_ref.dtype)
    vT_ref[0] = _mm_nt(wvT_ref[...], xn).astype(vT_ref.dtype)


def _mem_kv(mem, g, w_kv):
    B, M, _ = mem.shape
    wk = w_kv[:, :BRANCH_WIDTH].astype(MXU_DTYPE)
    wvT = w_kv[:, BRANCH_WIDTH:].T.astype(MXU_DTYPE)
    return pl.pallas_call(
        _mem_kv_kernel,
        grid=(B,),
        in_specs=[pl.BlockSpec((1, M, D_MODEL), lambda b: (b, 0, 0)), _const_spec((1, D_MODEL)),
                  _const_spec(wk.shape), _const_spec(wvT.shape)],
        out_specs=[pl.BlockSpec((1, N_HEADS * M, BRANCH_WIDTH), lambda b: (b, 0, 0)),
                   pl.BlockSpec((1, BRANCH_WIDTH, M), lambda b: (b, 0, 0))],
        out_shape=[jax.ShapeDtypeStruct((B, N_HEADS * M, BRANCH_WIDTH), MXU_DTYPE),
                   jax.ShapeDtypeStruct((B, BRANCH_WIDTH, M), MXU_DTYPE)],
        compiler_params=_params("parallel"),
        name="memory_kv",
    )(mem, g.reshape(1, D_MODEL), wk, wvT)


FF_CHUNK = 256


def _tail_kernel(h_ref, oa_ref, ob_ref, oc_ref, gmix_ref, wgm_ref, wup_ref, wout_ref,
                 gq_ref, wq_ref, ks_ref, vT_ref, wo_ref, gffn_ref, win_ref, wff_ref, gfin_ref, o_ref,
                 *, final_norm):
    x = h_ref[...]
    xn = _rmsnorm(x, gmix_ref[...]).astype(MXU_DTYPE)
    merged = jnp.zeros((TM, D_MODEL), F32)
    for j, br_ref in enumerate((oa_ref, ob_ref, oc_ref)):
        gate = jax.nn.sigmoid(_mm(xn, wgm_ref[:, j * D_MODEL:(j + 1) * D_MODEL]))
        merged = merged + gate * _mm(br_ref[...], wup_ref[j])
    x = x + _mm(merged, wout_ref[...])

    q = (_mm(_rmsnorm(x, gq_ref[...]), wq_ref[...]) * SOFTMAX_Q_SCALE).astype(MXU_DTYPE)
    M = vT_ref.shape[2]
    s = _mm_nt(ks_ref[0], q)
    outs = []
    for h in range(N_HEADS):
        sh = s[h * M:(h + 1) * M, :]
        p = jnp.exp2(sh - jnp.max(sh, axis=0, keepdims=True))
        o_h = _mm(vT_ref[0, h * HEAD_DIM:(h + 1) * HEAD_DIM, :], p)
        outs.append(o_h / jnp.sum(p, axis=0, keepdims=True))
    x = x + _mm(jnp.concatenate(outs, axis=0).T, wo_ref[...])

    xn = _rmsnorm(x, gffn_ref[...]).astype(MXU_DTYPE)
    for c in range(D_FF // FF_CHUNK):
        lo, hi = c * FF_CHUNK, (c + 1) * FF_CHUNK
        a = _mm(xn, win_ref[:, lo:hi])
        b = _mm(xn, win_ref[:, D_FF + lo:D_FF + hi])
        x = x + _mm(a * jax.nn.sigmoid(a) * b, wff_ref[lo:hi, :])
    if final_norm:
        x = _rmsnorm(x, gfin_ref[...])
    o_ref[...] = x


def _dense_tail(h, o_a, o_b, o_c, g_mix, w_gm, w_up, w_out, g_q, w_q, ks, vT, w_o,
                g_ffn, w_ffn_in, w_ffn_out, g_final, final_norm, S):
    T = h.shape[0]
    nb = S // TM
    row = lambda w: pl.BlockSpec((TM, w), lambda i: (i, 0))
    gain = _const_spec((1, D_MODEL))
    w_gm, w_up, w_out, w_q, w_o, w_ffn_in, w_ffn_out = [
        w.astype(MXU_DTYPE) for w in (w_gm, w_up, w_out, w_q, w_o, w_ffn_in, w_ffn_out)]
    per_batch = lambda a: pl.BlockSpec((1,) + a.shape[1:], lambda i: (i // nb, 0, 0))
    g = lambda v: v.reshape(1, D_MODEL)
    return pl.pallas_call(
        functools.partial(_tail_kernel, final_norm=final_norm),
        grid=(T // TM,),
        in_specs=[row(D_MODEL), row(BRANCH_WIDTH), row(BRANCH_WIDTH), row(BRANCH_WIDTH),
                  gain, _const_spec(w_gm.shape), _const_spec(w_up.shape), _const_spec(w_out.shape),
                  gain, _const_spec(w_q.shape), per_batch(ks), per_batch(vT), _const_spec(w_o.shape),
                  gain, _const_spec(w_ffn_in.shape), _const_spec(w_ffn_out.shape), gain],
        out_specs=row(D_MODEL),
        out_shape=jax.ShapeDtypeStruct((T, D_MODEL), F32),
        compiler_params=_params("parallel"),
        name="dense_tail",
    )(h, o_a, o_b, o_c, g(g_mix), w_gm, w_up, w_out, g(g_q), w_q, ks, vT, w_o,
      g(g_ffn), w_ffn_in, w_ffn_out, g(g_final))


def _rope_tables(S):
    inv = ROPE_THETA ** (-jnp.arange(0, HEAD_DIM, 2, dtype=F32) / HEAD_DIM)
    ang = jnp.arange(S, dtype=F32)[:, None] * inv[None, :]
    cos, sin = jnp.cos(ang), jnp.sin(ang)
    return (jnp.concatenate([cos, cos, cos, cos], axis=1),
            jnp.concatenate([-sin, sin, -sin, sin], axis=1))


def _mixers(h, B, S, g, w_in, cmp_k, cmp_v, cos_t, sin_t):
    (qa, ka, vaT, qb, qidx, qc, kb, kidx, ksel, kwin, vbT, vselT, vwinT, wg, ck, cv) = _in_projection(
        h, g, w_in, cos_t, sin_t, B, S)
    o_a = _stick_breaking(qa, ka, vaT)
    o_b = _dsa_attention(qidx, kidx, wg, qb, kb, vbT, B, S)
    kc = _compress(ck, *cmp_k, transpose_out=False)
    vcT = _compress(cv, *cmp_v, transpose_out=True)
    o_c = _nsa_attention(qc, wg, kc, vcT, ksel, vselT, kwin, vwinT, B, S)
    return [o.reshape(B * S, BRANCH_WIDTH) for o in (o_a, o_b, o_c)]


def kernel(x, mem, norm_mix, w_in, cmp_pos_k, cmp_w1_k, cmp_w2_k, cmp_pos_v, cmp_w1_v, cmp_w2_v,
           w_up, w_out, norm_mem_q, norm_mem_kv, w_mem_q, w_mem_kv, w_mem_o,
           norm_ffn, w_ffn_in, w_ffn_out, norm_final):
    B, S, D = x.shape
    depth = w_in.shape[0]
    assert D == D_MODEL and S % TM == 0 and S % TQ == 0 and WINDOW % TQ == 0
    cos_t, sin_t = _rope_tables(S)
    h = x.reshape(B * S, D)
    for l in range(depth):
        o_a, o_b, o_c = _mixers(h, B, S, norm_mix[l], w_in[l],
                                (cmp_pos_k[l], cmp_w1_k[l], cmp_w2_k[l]),
                                (cmp_pos_v[l], cmp_w1_v[l], cmp_w2_v[l]), cos_t, sin_t)
        ks, vT = _mem_kv(mem, norm_mem_kv[l], w_mem_kv[l])
        h = _dense_tail(h, o_a, o_b, o_c, norm_mix[l], w_in[l][:, _OFF['g_merge'][0]:], w_up[l], w_out[l],
                        norm_mem_q[l], w_mem_q[l], ks, vT, w_mem_o[l],
                        norm_ffn[l], w_ffn_in[l], w_ffn_out[l], norm_final, l == depth - 1, S)
    return h.reshape(B, S, D)
```

```python
import functools

import jax
import jax.numpy as jnp
import numpy as np
from jax import lax
from jax.experimental import pallas as pl
from jax.experimental.pallas import tpu as pltpu

F32 = jnp.float32
MXU_DTYPE = jnp.bfloat16

D_MODEL = 1024
HEAD_DIM = 64
N_HEADS = 4
BRANCH_WIDTH = N_HEADS * HEAD_DIM
IDX_HEADS = 4
DSA_TOPK = 256
CMP_LEN = 32
CMP_STRIDE = 16
CMP_HIDDEN = 128
SEL_LEN = 64
SEL_TOPN = 16
WINDOW = 512
D_FF = 2816
ROPE_THETA = 10000.0
RMS_EPS = 1e-6
FORCED_SCORE = 1e4
Q_SCALE = HEAD_DIM ** -0.5
LOG2E = 1.4426950408889634
SOFTMAX_Q_SCALE = Q_SCALE * LOG2E
W_IDX_SCALE = (IDX_HEADS * HEAD_DIM) ** -0.5
NEG = -1e30

LANES = 128
SUBLANES = 8
VMEM_LIMIT = 56 * 1024 * 1024
TM = 512
TQ = 256

_OFF = {}
_o = 0
for _n, _w in (('q_a', 256), ('k_a', 256), ('v_a', 256), ('q_b', 256), ('k_b', 64), ('v_b', 64),
               ('q_idx', 256), ('k_idx', 64), ('w_idx', 4), ('q_c', 256), ('k_cmp', 64),
               ('v_cmp', 64), ('k_sel', 64), ('v_sel', 64), ('k_win', 64), ('v_win', 64),
               ('g_c', 12), ('g_merge', 3 * D_MODEL)):
    _OFF[_n] = (_o, _o + _w)
    _o += _w
D_IN = _o


def _mm(a, b):
    return jnp.dot(a.astype(MXU_DTYPE), b.astype(MXU_DTYPE), preferred_element_type=F32)


def _mm_nt(a, b):
    return lax.dot_general(a.astype(MXU_DTYPE), b.astype(MXU_DTYPE),
                           (((1,), (1,)), ((), ())), preferred_element_type=F32)


def _params(*sem):
    return pltpu.CompilerParams(dimension_semantics=sem, vmem_limit_bytes=VMEM_LIMIT)


def _const_spec(shape):
    nd = len(shape)
    return pl.BlockSpec(shape, lambda *_: (0,) * nd, pipeline_mode=pl.Buffered(1))


def _rmsnorm(x, g):
    return x * lax.rsqrt(jnp.mean(x * x, axis=-1, keepdims=True) + RMS_EPS) * g


def _rope_lanes(x, cos, sin):
    lane = lax.broadcasted_iota(jnp.int32, (1, LANES), 1)
    first_half = (lane % HEAD_DIM) < (HEAD_DIM // 2)
    out = []
    for g in range(x.shape[1] // LANES):
        xg = x[:, g * LANES:(g + 1) * LANES]
        rot = jnp.where(first_half, pltpu.roll(xg, LANES - HEAD_DIM // 2, 1),
                        pltpu.roll(xg, HEAD_DIM // 2, 1))
        out.append(xg * cos + rot * sin)
    return out


WG_ROW = BRANCH_WIDTH + 3 * HEAD_DIM
WG_ROWS = IDX_HEADS + 3 * N_HEADS
WT_ROWS = 512
V_ROWS = HEAD_DIM + 2 * SUBLANES


def _inproj_kernel(h_ref, g_ref, cos_ref, sin_ref, wa_ref, wr_ref, wc_ref, wt_ref,
                   qa_ref, ka_ref, vaT_ref, qb_ref, qi_ref, qc_ref, kb_ref, ki_ref, ks_ref, kw_ref,
                   vbT_ref, vsT_ref, vwT_ref, wg_ref, ck_ref, cv_ref):
    xn = _rmsnorm(h_ref[...], g_ref[...]).astype(MXU_DTYPE)
    cos, sin = cos_ref[...], sin_ref[...]
    head = lambda y, h: y[:, h * HEAD_DIM:(h + 1) * HEAD_DIM]

    ya = _mm(xn, wa_ref[...])
    for h in range(N_HEADS):
        qa_ref[0, h] = (head(ya, h) * Q_SCALE).astype(qa_ref.dtype)
        ka_ref[0, h] = head(ya, N_HEADS + h).astype(ka_ref.dtype)

    yr = _rope_lanes(_mm(xn, wr_ref[...]), cos, sin)
    for g, (q_ref, scale) in enumerate(((qb_ref, SOFTMAX_Q_SCALE), (qi_ref, 1.0),
                                        (qc_ref, SOFTMAX_Q_SCALE))):
        for half in range(2):
            og = yr[2 * g + half] * scale
            q_ref[0, 2 * half] = head(og, 0).astype(q_ref.dtype)
            q_ref[0, 2 * half + 1] = head(og, 1).astype(q_ref.dtype)
    kb_ref[0] = head(yr[6], 0).astype(kb_ref.dtype)
    ki_ref[0] = head(yr[6], 1).astype(ki_ref.dtype)
    ks_ref[0] = head(yr[7], 0).astype(ks_ref.dtype)
    kw_ref[0] = head(yr[7], 1).astype(kw_ref.dtype)

    yc = _mm(xn, wc_ref[...])
    ck_ref[0] = head(_rope_lanes(yc, cos, sin)[0], 0)
    cv_ref[0] = head(yc, 1)

    yT = _mm_nt(wt_ref[...], xn)
    ones_row = (lax.broadcasted_iota(jnp.int32, (V_ROWS - HEAD_DIM, TQ), 0) == 0).astype(vbT_ref.dtype)
    for c in range(TM // TQ):
        cols = slice(c * TQ, (c + 1) * TQ)
        for h in range(N_HEADS):
            vaT_ref[0, h, c] = yT[h * HEAD_DIM:(h + 1) * HEAD_DIM, cols].astype(vaT_ref.dtype)
        for n, v_ref in enumerate((vbT_ref, vsT_ref, vwT_ref)):
            lo = BRANCH_WIDTH + n * HEAD_DIM
            v_ref[0, c, 0:HEAD_DIM, :] = yT[lo:lo + HEAD_DIM, cols].astype(v_ref.dtype)
            v_ref[0, c, HEAD_DIM:, :] = ones_row
    ys = yT[WG_ROW:WG_ROW + WG_ROWS, :]
    row = lax.broadcasted_iota(jnp.int32, ys.shape, 0)
    wg_ref[0] = jnp.where(row < IDX_HEADS, ys * W_IDX_SCALE, jax.nn.sigmoid(ys))


def _in_projection(h, g, w_in, cos_t, sin_t, B, S):
    cols = lambda *names: jnp.concatenate([w_in[:, _OFF[n][0]:_OFF[n][1]] for n in names], axis=1)
    wa = cols('q_a', 'k_a').astype(MXU_DTYPE)
    wr = cols('q_b', 'q_idx', 'q_c', 'k_b', 'k_idx', 'k_sel', 'k_win').astype(MXU_DTYPE)
    wc = cols('k_cmp', 'v_cmp').astype(MXU_DTYPE)
    wt = cols('v_a', 'v_b', 'v_sel', 'v_win', 'w_idx', 'g_c')
    wt = jnp.pad(wt, ((0, 0), (0, WT_ROWS - wt.shape[1]))).T.astype(MXU_DTYPE)
    nb, nc, cpb = S // TM, S // TQ, TM // TQ
    bf = lambda *shape: jax.ShapeDtypeStruct(shape, MXU_DTYPE)
    f32 = lambda *shape: jax.ShapeDtypeStruct(shape, F32)
    heads_spec = pl.BlockSpec((1, N_HEADS, TM, HEAD_DIM), lambda i: (i // nb, 0, i % nb, 0))
    tok_spec = pl.BlockSpec((1, TM, HEAD_DIM), lambda i: (i // nb, i % nb, 0))
    vT_spec = pl.BlockSpec((1, cpb, V_ROWS, TQ), lambda i: (i // nb, i % nb, 0, 0))
    pos = pl.BlockSpec((TM, LANES), lambda i: (i % nb, 0))
    q4, k3, v4 = bf(B, N_HEADS, S, HEAD_DIM), bf(B, S, HEAD_DIM), bf(B, nc, V_ROWS, TQ)
    return pl.pallas_call(
        _inproj_kernel,
        grid=(B * nb,),
        in_specs=[pl.BlockSpec((TM, D_MODEL), lambda i: (i, 0)), _const_spec((1, D_MODEL)), pos, pos,
                  _const_spec(wa.shape), _const_spec(wr.shape), _const_spec(wc.shape),
                  _const_spec(wt.shape)],
        out_specs=[heads_spec, heads_spec,
                   pl.BlockSpec((1, N_HEADS, cpb, HEAD_DIM, TQ), lambda i: (i // nb, 0, i % nb, 0, 0)),
                   heads_spec, heads_spec, heads_spec, tok_spec, tok_spec, tok_spec, tok_spec,
                   vT_spec, vT_spec, vT_spec,
                   pl.BlockSpec((1, WG_ROWS, TM), lambda i: (i // nb, 0, i % nb)), tok_spec, tok_spec],
        out_shape=[q4, q4, bf(B, N_HEADS, nc, HEAD_DIM, TQ), q4, q4, q4, k3, k3, k3, k3, v4, v4, v4,
                   f32(B, WG_ROWS, S), f32(B, S, HEAD_DIM), f32(B, S, HEAD_DIM)],
        compiler_params=_params("parallel"),
        name="in_projection",
    )(h, g.reshape(1, D_MODEL), cos_t, sin_t, wa, wr, wc, wt)


def _chunk_slice(j):
    return pl.ds(pl.multiple_of(j * TQ, TQ), TQ)


EXP_UNDERFLOW = 110.0
SCORE_BOUND_SLACK = 1.05


def _sb_kernel(q_ref, k_ref, vT_ref, tri_ref, o_ref, acc_ref, kmax_ref):
    i = pl.program_id(1)
    row = lax.broadcasted_iota(jnp.int32, (TQ, TQ), 0)
    col = lax.broadcasted_iota(jnp.int32, (TQ, TQ), 1)
    strict = row < col
    tri = tri_ref[...]
    qs = [q_ref[0, h] for h in range(N_HEADS)]
    acc_ref[...] = jnp.zeros(acc_ref.shape, F32)

    def chunks(js, cs, diag=False):
        heads = range(N_HEADS)
        zs = [[_mm_nt(k_ref[0, h, _chunk_slice(j), :], qs[h]) for h in heads] for j in js]
        cums = []
        for zj in zs:
            cums.append([])
            for z in zj:
                sp = jnp.maximum(z, 0.0) + jnp.log(1.0 + jnp.exp(-jnp.abs(z)))
                if diag:
                    sp = jnp.where(strict, sp, 0.0)
                hi = sp.astype(MXU_DTYPE)
                lo = (sp - hi.astype(F32)).astype(MXU_DTYPE)
                cums[-1].append(_mm(tri, hi) + _mm(tri, lo))
        outs = [jnp.zeros((HEAD_DIM, TQ), F32) for _ in heads]
        cs = list(cs)
        for n, j in enumerate(js):
            for h in heads:
                w = jnp.exp(zs[n][h] - cums[n][h] - cs[h])
                if diag:
                    w = jnp.where(strict, w, 0.0)
                outs[h] = outs[h] + _mm(vT_ref[0, h, j], w)
                cs[h] = cs[h] + cums[n][h][0:1, :]
        for h in heads:
            acc_ref[h] += outs[h]
        return tuple(cs)

    ones = jnp.ones((SUBLANES, HEAD_DIM), F32)

    def sq_norms(x):
        xf = x.astype(F32)
        return _mm_nt(ones, xf * xf)[0:1, :]

    @pl.when(i == 0)
    def _():
        for h in range(N_HEADS):
            def body(j, m):
                return jnp.maximum(m, sq_norms(k_ref[0, h, _chunk_slice(j), :]))
            m = lax.fori_loop(0, k_ref.shape[2] // TQ, body, jnp.zeros((1, TQ), F32))
            kmax_ref[h] = jnp.full((SUBLANES, LANES), jnp.max(m), F32)

    zbound = [jnp.sqrt(sq_norms(qs[h]) * kmax_ref[h][0:1, 0:1]) * SCORE_BOUND_SLACK for h in range(N_HEADS)]

    def negligible(cs):
        slack = cs[0] - zbound[0]
        for h in range(1, N_HEADS):
            slack = jnp.minimum(slack, cs[h] - zbound[h])
        return (jnp.min(slack) > EXP_UNDERFLOW).astype(jnp.int32)

    cs = chunks([i], tuple(jnp.zeros((1, TQ), F32) for _ in range(N_HEADS)), diag=True)

    def more(carry):
        j, done, _ = carry
        return (j >= 0) & (done == 0)

    def step(carry):
        j, _, cs = carry
        cs = chunks([j], cs)
        return j - 1, negligible(cs), cs

    lax.while_loop(more, step, (i - 1, negligible(cs), cs))
    o_ref[0] = acc_ref[...].reshape(BRANCH_WIDTH, TQ).T


def _stick_breaking(q, k, vT):
    B, H, S, _ = q.shape
    tri = jnp.asarray(np.triu(np.ones((TQ, TQ), np.float32)), MXU_DTYPE)
    return pl.pallas_call(
        _sb_kernel,
        grid=(B, S // TQ),
        in_specs=[pl.BlockSpec((1, H, TQ, HEAD_DIM), lambda b, i: (b, 0, i, 0)),
                  pl.BlockSpec((1, H, S, HEAD_DIM), lambda b, i: (b, 0, 0, 0)),
                  pl.BlockSpec((1, H, S // TQ, HEAD_DIM, TQ), lambda b, i: (b, 0, 0, 0, 0)),
                  _const_spec((TQ, TQ))],
        out_specs=pl.BlockSpec((1, TQ, BRANCH_WIDTH), lambda b, i: (b, i, 0)),
        out_shape=jax.ShapeDtypeStruct((B, S, BRANCH_WIDTH), F32),
        scratch_shapes=[pltpu.VMEM((H, HEAD_DIM, TQ), F32), pltpu.VMEM((H, SUBLANES, LANES), F32)],
        compiler_params=_params("parallel", "arbitrary"),
        name="stick_breaking",
    )(q, k, vT, tri)


PACKED_ROWS = 2 * SUBLANES


def _softmax_step(ks, qs, biases, vTs, ms, acc_ref):
    heads = range(len(qs))
    ss = [[_mm_nt(k, q) + b for q in qs] for k, b in zip(ks, biases)]
    m_new = []
    for h in heads:
        m = ms[h]
        for st in ss:
            m = jnp.maximum(m, jnp.max(st[h], axis=0, keepdims=True))
        m_new.append(m)
    pvs = []
    for h in heads:
        pv = jnp.zeros((V_ROWS, TQ), F32)
        for st, vT in zip(ss, vTs):
            pv = pv + _mm(vT, jnp.exp2(st[h] - m_new[h]))
        pvs.append(pv)
    for h in heads:
        acc_ref[h] = jnp.exp2(ms[h] - m_new[h]) * acc_ref[h] + pvs[h]
    return tuple(m_new)


def _for_chunks(lo, hi, step, ms):
    odd = (hi - lo) % 2
    ms = lax.cond(odd == 1, lambda ms: step([lo], ms), lambda ms: ms, ms)
    first = lo + odd
    return lax.fori_loop(0, (hi - lo) // 2,
                         lambda t, ms: step([first + 2 * t, first + 2 * t + 1], ms), ms)


def _softmax_result(acc_ref, h):
    return acc_ref[h, 0:HEAD_DIM, :] / acc_ref[h, HEAD_DIM:HEAD_DIM + 1, :]


MAX_FIXED_SHIFT = 45.0


def _sq_norms(x):
    xf = x.astype(F32)
    return _mm_nt(jnp.ones((SUBLANES, HEAD_DIM), F32), xf * xf)[0:1, :]


def _max_sq_norm(k_ref):
    def body(j, m):
        return jnp.maximum(m, _sq_norms(k_ref[0, _chunk_slice(j), :]))
    return jnp.max(lax.fori_loop(0, k_ref.shape[1] // TQ, body, jnp.zeros((1, TQ), F32)))


def _score_bound(qs, kmax2):
    qn2 = _sq_norms(qs[0])
    for q in qs[1:]:
        qn2 = jnp.maximum(qn2, _sq_norms(q))
    return jnp.sqrt(qn2 * kmax2) * SCORE_BOUND_SLACK


def _masked_attention(lo, hi, load, bias_fn, qs, bound, acc_ref):
    acc_ref[...] = jnp.zeros(acc_ref.shape, F32)
    fixed = jnp.max(bound) < MAX_FIXED_SHIFT

    @pl.when(fixed)
    def _():
        def step(js, carry):
            tiles = [load(j) for j in js]
            biases = [bias_fn(j, -bound) for j in js]
            ss = [[_mm_nt(k, q) + b for q in qs] for (k, _), b in zip(tiles, biases)]
            for h in range(len(qs)):
                pv = jnp.zeros((V_ROWS, TQ), F32)
                for st, (_, vT) in zip(ss, tiles):
                    pv = pv + _mm(vT, jnp.exp2(st[h]))
                acc_ref[h] += pv
            return carry

        _for_chunks(lo, hi, step, 0)

    @pl.when(jnp.logical_not(fixed))
    def _():
        def step(js, ms):
            tiles = [load(j) for j in js]
            return _softmax_step([k for k, _ in tiles], qs, [bias_fn(j, 0.0) for j in js],
                                 [vT for _, vT in tiles], ms, acc_ref)

        _for_chunks(lo, hi, step, tuple(jnp.full((1, TQ), NEG, F32) for _ in qs))


def _dsa_kernel(qi_ref, ki_ref, wg_ref, q_ref, k_ref, vT_ref, tril_ref, o_ref, sc_ref, hi_ref, lo_ref,
                acc_ref, kmax_ref, *, topk):
    i = pl.program_id(1)
    nchunks = i + 1
    npairs = (i + 2) // 2
    qpos = i * TQ + lax.broadcasted_iota(jnp.int32, (1, TQ), 1)
    rowi = lax.broadcasted_iota(jnp.int32, (TQ, TQ), 0)
    int_min = jnp.int32(-2 ** 31)

    qis = [qi_ref[0, hh] for hh in range(IDX_HEADS)]
    wT = wg_ref[0]

    def idx_chunks(js, carry, diag=False):
        rs = [[_mm_nt(ki_ref[0, _chunk_slice(j), :], qh) for qh in qis] for j in js]
        for j, rj in zip(js, rs):
            sc = jnp.zeros((TQ, TQ), F32)
            for hh in range(IDX_HEADS):
                sc = sc + wT[hh:hh + 1, :] * jnp.maximum(rj[hh], 0.0)
            if diag:
                sc = jnp.where(rowi <= qpos - i * TQ, sc, -jnp.inf)
            sc_ref[_chunk_slice(j), :] = sc
            hi_ref[_chunk_slice(j), :] = high_half(sc)
            bits = lax.bitcast_convert_type(sc, jnp.int32)
            lo_ref[_chunk_slice(j), :] = (bits ^ jnp.right_shift(bits, 31) ^ jnp.int32(0x8000)).astype(jnp.int16)
        return carry

    def low_half(key):
        return ((key & jnp.int32(0xFFFF)) - 32768).astype(jnp.int16)

    def high_half(x):
        bits = lax.bitcast_convert_type(x, jnp.int32) & jnp.int32(-65536)
        return lax.bitcast_convert_type(bits, F32).astype(jnp.bfloat16)

    _for_chunks(0, i, idx_chunks, 0)
    idx_chunks([i], 0, diag=True)

    @pl.when(nchunks % 2 == 1)
    def _():
        sc_ref[_chunk_slice(nchunks), :] = jnp.full((TQ, TQ), -jnp.inf, F32)
        hi_ref[_chunk_slice(nchunks), :] = jnp.full((TQ, TQ), -jnp.inf, jnp.bfloat16)
        lo_ref[_chunk_slice(nchunks), :] = jnp.full((TQ, TQ), -32768, jnp.int16)

    def count_packed(ref, thr16):
        one, zero = jnp.ones((), jnp.bfloat16), jnp.zeros((), jnp.bfloat16)

        def body(j, acc):
            for r in range(2):
                blk = ref[pl.ds(pl.multiple_of(j * 2 * TQ + r * TQ, TQ), TQ), :]
                ind = jnp.where(blk >= thr16, one, zero)
                parts = [ind[g * PACKED_ROWS:(g + 1) * PACKED_ROWS] for g in range(TQ // PACKED_ROWS)]
                while len(parts) > 1:
                    parts = [a + b for a, b in zip(parts[0::2], parts[1::2])]
                acc = acc + parts[0].astype(F32)
            return acc
        return jnp.sum(lax.fori_loop(0, npairs, body, jnp.zeros((PACKED_ROWS, TQ), F32)),
                       axis=0, keepdims=True)

    take_all = qpos < topk

    def search():
        def key_to_float(u):
            s = u ^ int_min
            return lax.bitcast_convert_type(jnp.where(s < 0, s ^ jnp.int32(0x7FFFFFFF), s), F32)

        def bit_body(t, carry, count_ge):
            res, n_ge, n_gt = carry
            cand = res | jnp.left_shift(jnp.int32(1), 31 - t)
            n = count_ge(cand)
            ok = n >= topk
            return jnp.where(ok, cand, res), jnp.where(ok, n, n_ge), jnp.where(ok, n_gt, n)

        zero = jnp.zeros((1, TQ), F32)
        carry = (jnp.zeros((1, TQ), jnp.int32), zero, zero)
        res, n_ge, n_above = lax.fori_loop(
            0, 16, functools.partial(
                bit_body, count_ge=lambda cand: count_packed(hi_ref, high_half(key_to_float(cand)))), carry)

        bucket = high_half(key_to_float(res))

        def keep_bucket(j, _):
            sl = pl.ds(pl.multiple_of(j * 2 * TQ, 2 * TQ), 2 * TQ)
            lo_ref[sl, :] = jnp.where(hi_ref[sl, :] == bucket, lo_ref[sl, :], jnp.int16(-32768))
            return 0

        lax.fori_loop(0, npairs, keep_bucket, 0)
        res, n_ge, n_gt = lax.fori_loop(
            16, 32, functools.partial(
                bit_body, count_ge=lambda cand: n_above + count_packed(lo_ref, low_half(cand))),
            (res, n_ge, n_above))
        thr = key_to_float(res)

        @pl.when(jnp.max(jnp.where(take_all, 0.0, n_ge)) > topk)
        def _():
            need = jnp.where(take_all, float(2 ** 30), topk - n_gt)
            tril = tril_ref[...]

            def drop(js, before):
                blks = [sc_ref[_chunk_slice(j), :] for j in js]
                counts = [_mm(tril, jnp.where(blk == thr, 1.0, 0.0)) for blk in blks]
                for j, blk, cnt in zip(js, blks, counts):
                    rank = cnt + before
                    sc_ref[_chunk_slice(j), :] = jnp.where((blk == thr) & (rank > need), -jnp.inf, blk)
                    before = rank[TQ - 1:TQ, :]
                return before

            _for_chunks(0, nchunks, drop, zero)

        return thr

    lowest = float(np.finfo(np.float32).min)
    thr = lax.cond(i * TQ + TQ > topk, search, lambda: jnp.full((1, TQ), lowest, F32))
    thr = jnp.where(take_all, lowest, thr)

    @pl.when(i == 0)
    def _():
        kmax_ref[...] = jnp.full(kmax_ref.shape, _max_sq_norm(k_ref), F32)

    qs = [q_ref[0, h] for h in range(N_HEADS)]
    _masked_attention(0, nchunks,
                      lambda j: (k_ref[0, _chunk_slice(j), :], vT_ref[0, j]),
                      lambda j, on: jnp.where(sc_ref[_chunk_slice(j), :] >= thr, on, NEG),
                      qs, _score_bound(qs, kmax_ref[0:1, 0:1]), acc_ref)
    o_ref[0] = jnp.concatenate([_softmax_result(acc_ref, h) for h in range(N_HEADS)], axis=0).T


def _dsa_attention(qidx, kidx, wg, qb, kb, vbT, B, S):
    topk = min(DSA_TOPK, S // 4)
    assert topk <= TQ
    nc = S // TQ
    q_spec = pl.BlockSpec((1, N_HEADS, TQ, HEAD_DIM), lambda b, i: (b, 0, i, 0))
    k_spec = pl.BlockSpec((1, S, HEAD_DIM), lambda b, i: (b, 0, 0))
    tril = jnp.asarray(np.tril(np.ones((TQ, TQ), np.float32)), MXU_DTYPE)
    return pl.pallas_call(
        functools.partial(_dsa_kernel, topk=topk),
        grid=(B, nc),
        in_specs=[q_spec, k_spec, pl.BlockSpec((1, WG_ROWS, TQ), lambda b, i: (b, 0, i)), q_spec, k_spec,
                  pl.BlockSpec((1, nc, V_ROWS, TQ), lambda b, i: (b, 0, 0, 0)), _const_spec((TQ, TQ))],
        out_specs=pl.BlockSpec((1, TQ, BRANCH_WIDTH), lambda b, i: (b, i, 0)),
        out_shape=jax.ShapeDtypeStruct((B, S, BRANCH_WIDTH), F32),
        scratch_shapes=[pltpu.VMEM(((nc + nc % 2) * TQ, TQ), F32),
                        pltpu.VMEM(((nc + nc % 2) * TQ, TQ), jnp.bfloat16),
                        pltpu.VMEM(((nc + nc % 2) * TQ, TQ), jnp.int16),
                        pltpu.VMEM((N_HEADS, V_ROWS, TQ), F32), pltpu.VMEM((SUBLANES, LANES), F32)],
        compiler_params=_params("parallel", "arbitrary"),
        name="dsa_attention",
    )(qidx, kidx, wg, qb, kb, vbT, tril)


def _compress_kernel(c_ref, pos_ref, w1_ref, w2_ref, o_ref, *, transpose_out):
    n = c_ref.shape[1] // CMP_STRIDE
    top = jnp.zeros((n, CMP_HIDDEN), F32)
    bot = jnp.zeros((n, CMP_HIDDEN), F32)
    for p in range(CMP_STRIDE):
        x = c_ref[0, pl.ds(p, n, stride=CMP_STRIDE), :]
        q = p + CMP_STRIDE
        top = top + _mm(x + pos_ref[p:p + 1, :], w1_ref[p * HEAD_DIM:(p + 1) * HEAD_DIM, :])
        bot = bot + _mm(x + pos_ref[q:q + 1, :], w1_ref[q * HEAD_DIM:(q + 1) * HEAD_DIM, :])
    hid = jax.nn.gelu(top + pltpu.roll(bot, n - 1, 0), approximate=True)
    out = _mm(hid, w2_ref[...])
    rowi = lax.broadcasted_iota(jnp.int32, out.shape, 0)
    out = jnp.where(rowi < n - 1, out, 0.0)
    if transpose_out:
        out = jnp.concatenate([out, jnp.zeros_like(out)], axis=1).T[0:HEAD_DIM, :]
    o_ref[0] = out.astype(o_ref.dtype)


def _compress(tok, pos_emb, w1, w2, transpose_out):
    B, S, _ = tok.shape
    assert CMP_LEN == 2 * CMP_STRIDE
    nch = S // CMP_STRIDE
    out_block = (1, HEAD_DIM, nch) if transpose_out else (1, nch, HEAD_DIM)
    return pl.pallas_call(
        functools.partial(_compress_kernel, transpose_out=transpose_out),
        grid=(B,),
        in_specs=[pl.BlockSpec((1, S, HEAD_DIM), lambda b: (b, 0, 0)),
                  _const_spec(pos_emb.shape), _const_spec(w1.shape), _const_spec(w2.shape)],
        out_specs=pl.BlockSpec(out_block, lambda b: (b, 0, 0)),
        out_shape=jax.ShapeDtypeStruct((B,) + out_block[1:], MXU_DTYPE),
        compiler_params=_params("parallel"),
        name="nsa_compress",
    )(tok, pos_emb, w1.astype(MXU_DTYPE), w2.astype(MXU_DTYPE))


def _nsa_kernel(q_ref, wg_ref, kc_ref, vcT_ref, ovT_ref, ks_ref, vsT_ref, kw_ref, vwT_ref,
                o_ref, selb_ref, acc_s_ref, acc_w_ref, kmax_ref, *, n_cmp, n_sel, topn):
    i = pl.program_id(1)
    qpos = i * TQ + lax.broadcasted_iota(jnp.int32, (1, TQ), 1)
    rowi = lax.broadcasted_iota(jnp.int32, (TQ, TQ), 0)
    heads = [q_ref[0, h] for h in range(N_HEADS)]
    gates = wg_ref[0]

    ncp = kc_ref.shape[1]
    cidx = lax.broadcasted_iota(jnp.int32, (ncp, TQ), 0)
    cmask = (cidx * CMP_STRIDE + (CMP_LEN - 1) <= qpos) & (cidx < n_cmp)
    kc, vcT, ovT = kc_ref[0], vcT_ref[0], ovT_ref[...]
    ss = [jnp.where(cmask, _mm_nt(kc, q), NEG) for q in heads]
    ps = []
    for s in ss:
        m = jnp.max(s, axis=0, keepdims=True)
        e = jnp.exp2(s - m)
        ps.append(e * jnp.where(m > 0.5 * NEG, 1.0 / jnp.sum(e, axis=0, keepdims=True), 0.0))
    o_cmp = [_mm(vcT, p) for p in ps]
    imp = jnp.zeros((n_sel, TQ), F32)
    for p in ps:
        p_hi = p.astype(MXU_DTYPE)
        imp = imp + _mm(ovT, p_hi) + _mm(ovT, p - p_hi.astype(F32))

    bidx = lax.broadcasted_iota(jnp.int32, (n_sel, TQ), 0)
    cur = qpos // SEL_LEN
    forced = (bidx == 0) | (bidx == cur) | (bidx == cur - 1)
    work = jnp.where(bidx * SEL_LEN <= qpos, jnp.where(forced, FORCED_SCORE, imp), -jnp.inf)
    sel = jnp.zeros((n_sel, TQ), F32)
    for _ in range(topn):
        m = jnp.max(work, axis=0, keepdims=True)
        first = jnp.min(jnp.where((work == m) & (m > -jnp.inf), bidx, n_sel), axis=0, keepdims=True)
        pick = bidx == first
        sel = jnp.where(pick, 1.0, sel)
        work = jnp.where(pick, -jnp.inf, work)
    selb_ref[...] = jnp.where(sel > 0.0, 0.0, NEG)

    per_chunk = TQ // SEL_LEN

    @pl.when(i == 0)
    def _():
        kmax_ref[...] = jnp.full(kmax_ref.shape, jnp.maximum(_max_sq_norm(ks_ref), _max_sq_norm(kw_ref)),
                                 F32)

    bound = _score_bound(heads, kmax_ref[0:1, 0:1])

    def sel_bias(j, on):
        rows = [jnp.broadcast_to(selb_ref[pl.ds(j * per_chunk + bb, 1), :], (SEL_LEN, TQ))
                for bb in range(per_chunk)]
        return jnp.where(j * TQ + rowi <= qpos, jnp.concatenate(rows, axis=0), NEG) + on

    _masked_attention(0, i + 1, lambda j: (ks_ref[0, _chunk_slice(j), :], vsT_ref[0, j]), sel_bias,
                      heads, bound, acc_s_ref)

    def win_bias(j, on):
        diff = qpos - (j * TQ + rowi)
        return jnp.where((diff >= 0) & (diff < WINDOW), on, NEG)

    _masked_attention(jnp.maximum(i - WINDOW // TQ, 0), i + 1,
                      lambda j: (kw_ref[0, _chunk_slice(j), :], vwT_ref[0, j]), win_bias,
                      heads, bound, acc_w_ref)

    outs = []
    for h in range(N_HEADS):
        g = lambda br: gates[IDX_HEADS + 3 * h + br:IDX_HEADS + 3 * h + br + 1, :]
        outs.append(g(0) * o_cmp[h] + g(1) * _softmax_result(acc_s_ref, h)
                    + g(2) * _softmax_result(acc_w_ref, h))
    o_ref[0] = jnp.concatenate(outs, axis=0).T


def _nsa_attention(qc, wg, kc, vcT, ksel, vselT, kwin, vwinT, B, S):
    n_cmp = (S - CMP_LEN) // CMP_STRIDE + 1
    n_sel = S // SEL_LEN
    topn = min(SEL_TOPN, n_sel)
    ncp = kc.shape[1]
    cs = np.arange(ncp) * CMP_STRIDE
    ss = np.arange(n_sel) * SEL_LEN
    overlapT = ((cs[None, :] < ss[:, None] + SEL_LEN) & (cs[None, :] + CMP_LEN > ss[:, None])
                & (np.arange(ncp)[None, :] < n_cmp)).astype(np.float32)
    k_spec = pl.BlockSpec((1, S, HEAD_DIM), lambda b, i: (b, 0, 0))
    vT_spec = pl.BlockSpec((1, S // TQ, V_ROWS, TQ), lambda b, i: (b, 0, 0, 0))
    return pl.pallas_call(
        functools.partial(_nsa_kernel, n_cmp=n_cmp, n_sel=n_sel, topn=topn),
        grid=(B, S // TQ),
        in_specs=[pl.BlockSpec((1, N_HEADS, TQ, HEAD_DIM), lambda b, i: (b, 0, i, 0)),
                  pl.BlockSpec((1, WG_ROWS, TQ), lambda b, i: (b, 0, i)),
                  pl.BlockSpec((1, ncp, HEAD_DIM), lambda b, i: (b, 0, 0)),
                  pl.BlockSpec((1, HEAD_DIM, ncp), lambda b, i: (b, 0, 0)),
                  _const_spec((n_sel, ncp)), k_spec, vT_spec, k_spec, vT_spec],
        out_specs=pl.BlockSpec((1, TQ, BRANCH_WIDTH), lambda b, i: (b, i, 0)),
        out_shape=jax.ShapeDtypeStruct((B, S, BRANCH_WIDTH), F32),
        scratch_shapes=[pltpu.VMEM((n_sel, TQ), F32), pltpu.VMEM((N_HEADS, V_ROWS, TQ), F32),
                        pltpu.VMEM((N_HEADS, V_ROWS, TQ), F32), pltpu.VMEM((SUBLANES, LANES), F32)],
        compiler_params=_params("parallel", "arbitrary"),
        name="nsa_attention",
    )(qc, wg, kc, vcT, jnp.asarray(overlapT, MXU_DTYPE), ksel, vselT, kwin, vwinT)


def _mem_kv_kernel(m_ref, g_ref, wk_ref, wvT_ref, ks_ref, vT_ref):
    xn = _rmsnorm(m_ref[0], g_ref[...]).astype(MXU_DTYPE)
    M = xn.shape[0]
    k = _mm(xn, wk_ref[...])
    lane_head = lax.broadcasted_iota(jnp.int32, (1, BRANCH_WIDTH), 1) // HEAD_DIM
    for h in range(N_HEADS):
        ks_ref[0, h * M:(h + 1) * M, :] = jnp.where(lane_head == h, k, 0.0).astype(ks_ref.dtype)
    vT_ref[0] = _mm_nt(wvT_ref[...], xn).astype(vT_ref.dtype)


def _mem_kv(mem, g, w_kv):
    B, M, _ = mem.shape
    wk = w_kv[:, :BRANCH_WIDTH].astype(MXU_DTYPE)
    wvT = w_kv[:, BRANCH_WIDTH:].T.astype(MXU_DTYPE)
    return pl.pallas_call(
        _mem_kv_kernel,
        grid=(B,),
        in_specs=[pl.BlockSpec((1, M, D_MODEL), lambda b: (b, 0, 0)), _const_spec((1, D_MODEL)),
                  _const_spec(wk.shape), _const_spec(wvT.shape)],
        out_specs=[pl.BlockSpec((1, N_HEADS * M, BRANCH_WIDTH), lambda b: (b, 0, 0)),
                   pl.BlockSpec((1, BRANCH_WIDTH, M), lambda b: (b, 0, 0))],
        out_shape=[jax.ShapeDtypeStruct((B, N_HEADS * M, BRANCH_WIDTH), MXU_DTYPE),
                   jax.ShapeDtypeStruct((B, BRANCH_WIDTH, M), MXU_DTYPE)],
        compiler_params=_params("parallel"),
        name="memory_kv",
    )(mem, g.reshape(1, D_MODEL), wk, wvT)


FF_CHUNK = 256


def _tail_kernel(h_ref, oa_ref, ob_ref, oc_ref, gmix_ref, wgm_ref, wup_ref, wout_ref,
                 gq_ref, wq_ref, ks_ref, vT_ref, wo_ref, gffn_ref, win_ref, wff_ref, gfin_ref, o_ref,
                 *, final_norm):
    x = h_ref[...]
    xn = _rmsnorm(x, gmix_ref[...]).astype(MXU_DTYPE)
    merged = jnp.zeros((TM, D_MODEL), F32)
    for j, br_ref in enumerate((oa_ref, ob_ref, oc_ref)):
        gate = jax.nn.sigmoid(_mm(xn, wgm_ref[:, j * D_MODEL:(j + 1) * D_MODEL]))
        merged = merged + gate * _mm(br_ref[...], wup_ref[j])
    x = x + _mm(merged, wout_ref[...])

    q = (_mm(_rmsnorm(x, gq_ref[...]), wq_ref[...]) * SOFTMAX_Q_SCALE).astype(MXU_DTYPE)
    M = vT_ref.shape[2]
    s = _mm_nt(ks_ref[0], q)
    outs = []
    for h in range(N_HEADS):
        sh = s[h * M:(h + 1) * M, :]
        p = jnp.exp2(sh - jnp.max(sh, axis=0, keepdims=True))
        o_h = _mm(vT_ref[0, h * HEAD_DIM:(h + 1) * HEAD_DIM, :], p)
        outs.append(o_h / jnp.sum(p, axis=0, keepdims=True))
    x = x + _mm(jnp.concatenate(outs, axis=0).T, wo_ref[...])

    xn = _rmsnorm(x, gffn_ref[...]).astype(MXU_DTYPE)
    for c in range(D_FF // FF_CHUNK):
        lo, hi = c * FF_CHUNK, (c + 1) * FF_CHUNK
        a = _mm(xn, win_ref[:, lo:hi])
        b = _mm(xn, win_ref[:, D_FF + lo:D_FF + hi])
        x = x + _mm(a * jax.nn.sigmoid(a) * b, wff_ref[lo:hi, :])
    if final_norm:
        x = _rmsnorm(x, gfin_ref[...])
    o_ref[...] = x


def _dense_tail(h, o_a, o_b, o_c, g_mix, w_gm, w_up, w_out, g_q, w_q, ks, vT, w_o,
                g_ffn, w_ffn_in, w_ffn_out, g_final, final_norm, S):
    T = h.shape[0]
    nb = S // TM
    row = lambda w: pl.BlockSpec((TM, w), lambda i: (i, 0))
    gain = _const_spec((1, D_MODEL))
    w_gm, w_up, w_out, w_q, w_o, w_ffn_in, w_ffn_out = [
        w.astype(MXU_DTYPE) for w in (w_gm, w_up, w_out, w_q, w_o, w_ffn_in, w_ffn_out)]
    per_batch = lambda a: pl.BlockSpec((1,) + a.shape[1:], lambda i: (i // nb, 0, 0))
    g = lambda v: v.reshape(1, D_MODEL)
    return pl.pallas_call(
        functools.partial(_tail_kernel, final_norm=final_norm),
        grid=(T // TM,),
        in_specs=[row(D_MODEL), row(BRANCH_WIDTH), row(BRANCH_WIDTH), row(BRANCH_WIDTH),
                  gain, _const_spec(w_gm.shape), _const_spec(w_up.shape), _const_spec(w_out.shape),
                  gain, _const_spec(w_q.shape), per_batch(ks), per_batch(vT), _const_spec(w_o.shape),
                  gain, _const_spec(w_ffn_in.shape), _const_spec(w_ffn_out.shape), gain],
        out_specs=row(D_MODEL),
        out_shape=jax.ShapeDtypeStruct((T, D_MODEL), F32),
        compiler_params=_params("parallel"),
        name="dense_tail",
    )(h, o_a, o_b, o_c, g(g_mix), w_gm, w_up, w_out, g(g_q), w_q, ks, vT, w_o,
      g(g_ffn), w_ffn_in, w_ffn_out, g(g_final))


def _rope_tables(S):
    inv = ROPE_THETA ** (-jnp.arange(0, HEAD_DIM, 2, dtype=F32) / HEAD_DIM)
    ang = jnp.arange(S, dtype=F32)[:, None] * inv[None, :]
    cos, sin = jnp.cos(ang), jnp.sin(ang)
    return (jnp.concatenate([cos, cos, cos, cos], axis=1),
            jnp.concatenate([-sin, sin, -sin, sin], axis=1))


def _mixers(h, B, S, g, w_in, cmp_k, cmp_v, cos_t, sin_t):
    (qa, ka, vaT, qb, qidx, qc, kb, kidx, ksel, kwin, vbT, vselT, vwinT, wg, ck, cv) = _in_projection(
        h, g, w_in, cos_t, sin_t, B, S)
    o_a = _stick_breaking(qa, ka, vaT)
    o_b = _dsa_attention(qidx, kidx, wg, qb, kb, vbT, B, S)
    kc = _compress(ck, *cmp_k, transpose_out=False)
    vcT = _compress(cv, *cmp_v, transpose_out=True)
    o_c = _nsa_attention(qc, wg, kc, vcT, ksel, vselT, kwin, vwinT, B, S)
    return [o.reshape(B * S, BRANCH_WIDTH) for o in (o_a, o_b, o_c)]


def kernel(x, mem, norm_mix, w_in, cmp_pos_k, cmp_w1_k, cmp_w2_k, cmp_pos_v, cmp_w1_v, cmp_w2_v,
           w_up, w_out, norm_mem_q, norm_mem_kv, w_mem_q, w_mem_kv, w_mem_o,
           norm_ffn, w_ffn_in, w_ffn_out, norm_final):
    B, S, D = x.shape
    depth = w_in.shape[0]
    assert D == D_MODEL and S % TM == 0 and S % TQ == 0 and WINDOW % TQ == 0
    cos_t, sin_t = _rope_tables(S)
    h = x.reshape(B * S, D)
    for l in range(depth):
        o_a, o_b, o_c = _mixers(h, B, S, norm_mix[l], w_in[l],
                                (cmp_pos_k[l], cmp_w1_k[l], cmp_w2_k[l]),
                                (cmp_pos_v[l], cmp_w1_v[l], cmp_w2_v[l]), cos_t, sin_t)
        ks, vT = _mem_kv(mem, norm_mem_kv[l], w_mem_kv[l])
        h = _dense_tail(h, o_a, o_b, o_c, norm_mix[l], w_in[l][:, _OFF['g_merge'][0]:], w_up[l], w_out[l],
                        norm_mem_q[l], w_mem_q[l], ks, vT, w_mem_o[l],
                        norm_ffn[l], w_ffn_in[l], w_ffn_out[l], norm_final, l == depth - 1, S)
    return h.reshape(B, S, D)
```

```python
import functools

import jax
import jax.numpy as jnp
import numpy as np
from jax import lax
from jax.experimental import pallas as pl
from jax.experimental.pallas import tpu as pltpu

F32 = jnp.float32
MXU_DTYPE = jnp.bfloat16

D_MODEL = 1024
HEAD_DIM = 64
N_HEADS = 4
BRANCH_WIDTH = N_HEADS * HEAD_DIM
IDX_HEADS = 4
DSA_TOPK = 256
CMP_LEN = 32
CMP_STRIDE = 16
CMP_HIDDEN = 128
SEL_LEN = 64
SEL_TOPN = 16
WINDOW = 512
D_FF = 2816
ROPE_THETA = 10000.0
RMS_EPS = 1e-6
FORCED_SCORE = 1e4
Q_SCALE = HEAD_DIM ** -0.5
LOG2E = 1.4426950408889634
SOFTMAX_Q_SCALE = Q_SCALE * LOG2E
W_IDX_SCALE = (IDX_HEADS * HEAD_DIM) ** -0.5
NEG = -1e30

LANES = 128
SUBLANES = 8
VMEM_LIMIT = 56 * 1024 * 1024
TM = 512
TQ = 256

_OFF = {}
_o = 0
for _n, _w in (('q_a', 256), ('k_a', 256), ('v_a', 256), ('q_b', 256), ('k_b', 64), ('v_b', 64),
               ('q_idx', 256), ('k_idx', 64), ('w_idx', 4), ('q_c', 256), ('k_cmp', 64),
               ('v_cmp', 64), ('k_sel', 64), ('v_sel', 64), ('k_win', 64), ('v_win', 64),
               ('g_c', 12), ('g_merge', 3 * D_MODEL)):
    _OFF[_n] = (_o, _o + _w)
    _o += _w
D_IN = _o


def _mm(a, b):
    return jnp.dot(a.astype(MXU_DTYPE), b.astype(MXU_DTYPE), preferred_element_type=F32)


def _mm_nt(a, b):
    return lax.dot_general(a.astype(MXU_DTYPE), b.astype(MXU_DTYPE),
                           (((1,), (1,)), ((), ())), preferred_element_type=F32)


def _params(*sem):
    return pltpu.CompilerParams(dimension_semantics=sem, vmem_limit_bytes=VMEM_LIMIT)


def _const_spec(shape):
    nd = len(shape)
    return pl.BlockSpec(shape, lambda *_: (0,) * nd, pipeline_mode=pl.Buffered(1))


def _rmsnorm(x, g):
    return x * lax.rsqrt(jnp.mean(x * x, axis=-1, keepdims=True) + RMS_EPS) * g


def _rope_lanes(x, cos, sin):
    lane = lax.broadcasted_iota(jnp.int32, (1, LANES), 1)
    first_half = (lane % HEAD_DIM) < (HEAD_DIM // 2)
    out = []
    for g in range(x.shape[1] // LANES):
        xg = x[:, g * LANES:(g + 1) * LANES]
        rot = jnp.where(first_half, pltpu.roll(xg, LANES - HEAD_DIM // 2, 1),
                        pltpu.roll(xg, HEAD_DIM // 2, 1))
        out.append(xg * cos + rot * sin)
    return out


WG_ROW = BRANCH_WIDTH + 3 * HEAD_DIM
WG_ROWS = IDX_HEADS + 3 * N_HEADS
WT_ROWS = 512
V_ROWS = HEAD_DIM + 2 * SUBLANES


def _inproj_kernel(h_ref, g_ref, cos_ref, sin_ref, wa_ref, wr_ref, wc_ref, wt_ref,
                   qa_ref, ka_ref, vaT_ref, qb_ref, qi_ref, qc_ref, kb_ref, ki_ref, ks_ref, kw_ref,
                   vbT_ref, vsT_ref, vwT_ref, wg_ref, ck_ref, cv_ref):
    xn = _rmsnorm(h_ref[...], g_ref[...]).astype(MXU_DTYPE)
    cos, sin = cos_ref[...], sin_ref[...]
    head = lambda y, h: y[:, h * HEAD_DIM:(h + 1) * HEAD_DIM]

    ya = _mm(xn, wa_ref[...])
    for h in range(N_HEADS):
        qa_ref[0, h] = (head(ya, h) * Q_SCALE).astype(qa_ref.dtype)
        ka_ref[0, h] = head(ya, N_HEADS + h).astype(ka_ref.dtype)

    yr = _rope_lanes(_mm(xn, wr_ref[...]), cos, sin)
    for g, (q_ref, scale) in enumerate(((qb_ref, SOFTMAX_Q_SCALE), (qi_ref, 1.0),
                                        (qc_ref, SOFTMAX_Q_SCALE))):
        for half in range(2):
            og = yr[2 * g + half] * scale
            q_ref[0, 2 * half] = head(og, 0).astype(q_ref.dtype)
            q_ref[0, 2 * half + 1] = head(og, 1).astype(q_ref.dtype)
    kb_ref[0] = head(yr[6], 0).astype(kb_ref.dtype)
    ki_ref[0] = head(yr[6], 1).astype(ki_ref.dtype)
    ks_ref[0] = head(yr[7], 0).astype(ks_ref.dtype)
    kw_ref[0] = head(yr[7], 1).astype(kw_ref.dtype)

    yc = _mm(xn, wc_ref[...])
    ck_ref[0] = head(_rope_lanes(yc, cos, sin)[0], 0)
    cv_ref[0] = head(yc, 1)

    yT = _mm_nt(wt_ref[...], xn)
    ones_row = (lax.broadcasted_iota(jnp.int32, (V_ROWS - HEAD_DIM, TQ), 0) == 0).astype(vbT_ref.dtype)
    for c in range(TM // TQ):
        cols = slice(c * TQ, (c + 1) * TQ)
        for h in range(N_HEADS):
            vaT_ref[0, h, c] = yT[h * HEAD_DIM:(h + 1) * HEAD_DIM, cols].astype(vaT_ref.dtype)
        for n, v_ref in enumerate((vbT_ref, vsT_ref, vwT_ref)):
            lo = BRANCH_WIDTH + n * HEAD_DIM
            v_ref[0, c, 0:HEAD_DIM, :] = yT[lo:lo + HEAD_DIM, cols].astype(v_ref.dtype)
            v_ref[0, c, HEAD_DIM:, :] = ones_row
    ys = yT[WG_ROW:WG_ROW + WG_ROWS, :]
    row = lax.broadcasted_iota(jnp.int32, ys.shape, 0)
    wg_ref[0] = jnp.where(row < IDX_HEADS, ys * W_IDX_SCALE, jax.nn.sigmoid(ys))


def _in_projection(h, g, w_in, cos_t, sin_t, B, S):
    cols = lambda *names: jnp.concatenate([w_in[:, _OFF[n][0]:_OFF[n][1]] for n in names], axis=1)
    wa = cols('q_a', 'k_a').astype(MXU_DTYPE)
    wr = cols('q_b', 'q_idx', 'q_c', 'k_b', 'k_idx', 'k_sel', 'k_win').astype(MXU_DTYPE)
    wc = cols('k_cmp', 'v_cmp').astype(MXU_DTYPE)
    wt = cols('v_a', 'v_b', 'v_sel', 'v_win', 'w_idx', 'g_c')
    wt = jnp.pad(wt, ((0, 0), (0, WT_ROWS - wt.shape[1]))).T.astype(MXU_DTYPE)
    nb, nc, cpb = S // TM, S // TQ, TM // TQ
    bf = lambda *shape: jax.ShapeDtypeStruct(shape, MXU_DTYPE)
    f32 = lambda *shape: jax.ShapeDtypeStruct(shape, F32)
    heads_spec = pl.BlockSpec((1, N_HEADS, TM, HEAD_DIM), lambda i: (i // nb, 0, i % nb, 0))
    tok_spec = pl.BlockSpec((1, TM, HEAD_DIM), lambda i: (i // nb, i % nb, 0))
    vT_spec = pl.BlockSpec((1, cpb, V_ROWS, TQ), lambda i: (i // nb, i % nb, 0, 0))
    pos = pl.BlockSpec((TM, LANES), lambda i: (i % nb, 0))
    q4, k3, v4 = bf(B, N_HEADS, S, HEAD_DIM), bf(B, S, HEAD_DIM), bf(B, nc, V_ROWS, TQ)
    return pl.pallas_call(
        _inproj_kernel,
        grid=(B * nb,),
        in_specs=[pl.BlockSpec((TM, D_MODEL), lambda i: (i, 0)), _const_spec((1, D_MODEL)), pos, pos,
                  _const_spec(wa.shape), _const_spec(wr.shape), _const_spec(wc.shape),
                  _const_spec(wt.shape)],
        out_specs=[heads_spec, heads_spec,
                   pl.BlockSpec((1, N_HEADS, cpb, HEAD_DIM, TQ), lambda i: (i // nb, 0, i % nb, 0, 0)),
                   heads_spec, heads_spec, heads_spec, tok_spec, tok_spec, tok_spec, tok_spec,
                   vT_spec, vT_spec, vT_spec,
                   pl.BlockSpec((1, WG_ROWS, TM), lambda i: (i // nb, 0, i % nb)), tok_spec, tok_spec],
        out_shape=[q4, q4, bf(B, N_HEADS, nc, HEAD_DIM, TQ), q4, q4, q4, k3, k3, k3, k3, v4, v4, v4,
                   f32(B, WG_ROWS, S), f32(B, S, HEAD_DIM), f32(B, S, HEAD_DIM)],
        compiler_params=_params("parallel"),
        name="in_projection",
    )(h, g.reshape(1, D_MODEL), cos_t, sin_t, wa, wr, wc, wt)


def _chunk_slice(j):
    return pl.ds(pl.multiple_of(j * TQ, TQ), TQ)


EXP_UNDERFLOW = 110.0
SCORE_BOUND_SLACK = 1.05


def _sb_kernel(q_ref, k_ref, vT_ref, tri_ref, o_ref, acc_ref, kmax_ref):
    i = pl.program_id(1)
    row = lax.broadcasted_iota(jnp.int32, (TQ, TQ), 0)
    col = lax.broadcasted_iota(jnp.int32, (TQ, TQ), 1)
    strict = row < col
    tri = tri_ref[...]
    qs = [q_ref[0, h] for h in range(N_HEADS)]
    acc_ref[...] = jnp.zeros(acc_ref.shape, F32)

    def chunks(js, cs, diag=False):
        heads = range(N_HEADS)
        zs = [[_mm_nt(k_ref[0, h, _chunk_slice(j), :], qs[h]) for h in heads] for j in js]
        cums = []
        for zj in zs:
            cums.append([])
            for z in zj:
                sp = jnp.maximum(z, 0.0) + jnp.log(1.0 + jnp.exp(-jnp.abs(z)))
                if diag:
                    sp = jnp.where(strict, sp, 0.0)
                hi = sp.astype(MXU_DTYPE)
                lo = (sp - hi.astype(F32)).astype(MXU_DTYPE)
                cums[-1].append(_mm(tri, hi) + _mm(tri, lo))
        outs = [jnp.zeros((HEAD_DIM, TQ), F32) for _ in heads]
        cs = list(cs)
        for n, j in enumerate(js):
            for h in heads:
                w = jnp.exp(zs[n][h] - cums[n][h] - cs[h])
                if diag:
                    w = jnp.where(strict, w, 0.0)
                outs[h] = outs[h] + _mm(vT_ref[0, h, j], w)
                cs[h] = cs[h] + cums[n][h][0:1, :]
        for h in heads:
            acc_ref[h] += outs[h]
        return tuple(cs)

    ones = jnp.ones((SUBLANES, HEAD_DIM), F32)

    def sq_norms(x):
        xf = x.astype(F32)
        return _mm_nt(ones, xf * xf)[0:1, :]

    @pl.when(i == 0)
    def _():
        for h in range(N_HEADS):
            def body(j, m):
                return jnp.maximum(m, sq_norms(k_ref[0, h, _chunk_slice(j), :]))
            m = lax.fori_loop(0, k_ref.shape[2] // TQ, body, jnp.zeros((1, TQ), F32))
            kmax_ref[h] = jnp.full((SUBLANES, LANES), jnp.max(m), F32)

    zbound = [jnp.sqrt(sq_norms(qs[h]) * kmax_ref[h][0:1, 0:1]) * SCORE_BOUND_SLACK for h in range(N_HEADS)]

    def negligible(cs):
        slack = cs[0] - zbound[0]
        for h in range(1, N_HEADS):
            slack = jnp.minimum(slack, cs[h] - zbound[h])
        return (jnp.min(slack) > EXP_UNDERFLOW).astype(jnp.int32)

    cs = chunks([i], tuple(jnp.zeros((1, TQ), F32) for _ in range(N_HEADS)), diag=True)

    def more(carry):
        j, done, _ = carry
        return (j >= 0) & (done == 0)

    def step(carry):
        j, _, cs = carry
        cs = chunks([j], cs)
        return j - 1, negligible(cs), cs

    lax.while_loop(more, step, (i - 1, negligible(cs), cs))
    o_ref[0] = acc_ref[...].reshape(BRANCH_WIDTH, TQ).T


def _stick_breaking(q, k, vT):
    B, H, S, _ = q.shape
    tri = jnp.asarray(np.triu(np.ones((TQ, TQ), np.float32)), MXU_DTYPE)
    return pl.pallas_call(
        _sb_kernel,
        grid=(B, S // TQ),
        in_specs=[pl.BlockSpec((1, H, TQ, HEAD_DIM), lambda b, i: (b, 0, i, 0)),
                  pl.BlockSpec((1, H, S, HEAD_DIM), lambda b, i: (b, 0, 0, 0)),
                  pl.BlockSpec((1, H, S // TQ, HEAD_DIM, TQ), lambda b, i: (b, 0, 0, 0, 0)),
                  _const_spec((TQ, TQ))],
        out_specs=pl.BlockSpec((1, TQ, BRANCH_WIDTH), lambda b, i: (b, i, 0)),
        out_shape=jax.ShapeDtypeStruct((B, S, BRANCH_WIDTH), F32),
        scratch_shapes=[pltpu.VMEM((H, HEAD_DIM, TQ), F32), pltpu.VMEM((H, SUBLANES, LANES), F32)],
        compiler_params=_params("parallel", "arbitrary"),
        name="stick_breaking",
    )(q, k, vT, tri)


COUNT_ROWS = 64


def _softmax_step(ks, qs, biases, vTs, ms, acc_ref):
    heads = range(len(qs))
    ss = [[_mm_nt(k, q) + b for q in qs] for k, b in zip(ks, biases)]
    m_new = []
    for h in heads:
        m = ms[h]
        for st in ss:
            m = jnp.maximum(m, jnp.max(st[h], axis=0, keepdims=True))
        m_new.append(m)
    pvs = []
    for h in heads:
        pv = jnp.zeros((V_ROWS, TQ), F32)
        for st, vT in zip(ss, vTs):
            pv = pv + _mm(vT, jnp.exp2(st[h] - m_new[h]))
        pvs.append(pv)
    for h in heads:
        acc_ref[h] = jnp.exp2(ms[h] - m_new[h]) * acc_ref[h] + pvs[h]
    return tuple(m_new)


def _for_chunks(lo, hi, step, ms):
    odd = (hi - lo) % 2
    ms = lax.cond(odd == 1, lambda ms: step([lo], ms), lambda ms: ms, ms)
    first = lo + odd
    return lax.fori_loop(0, (hi - lo) // 2,
                         lambda t, ms: step([first + 2 * t, first + 2 * t + 1], ms), ms)


def _softmax_result(acc_ref, h):
    return acc_ref[h, 0:HEAD_DIM, :] / acc_ref[h, HEAD_DIM:HEAD_DIM + 1, :]


MAX_FIXED_SHIFT = 45.0


def _sq_norms(x):
    xf = x.astype(F32)
    return _mm_nt(jnp.ones((SUBLANES, HEAD_DIM), F32), xf * xf)[0:1, :]


def _max_sq_norm(k_ref):
    def body(j, m):
        return jnp.maximum(m, _sq_norms(k_ref[0, _chunk_slice(j), :]))
    return jnp.max(lax.fori_loop(0, k_ref.shape[1] // TQ, body, jnp.zeros((1, TQ), F32)))


def _score_bound(qs, kmax2):
    qn2 = _sq_norms(qs[0])
    for q in qs[1:]:
        qn2 = jnp.maximum(qn2, _sq_norms(q))
    return jnp.sqrt(qn2 * kmax2) * SCORE_BOUND_SLACK


def _masked_attention(lo, hi, load, bias_fn, qs, bound, acc_ref, triple=False):
    acc_ref[...] = jnp.zeros(acc_ref.shape, F32)
    fixed = jnp.max(bound) < MAX_FIXED_SHIFT

    def run(step, carry):
        if not triple:
            return _for_chunks(lo, hi, step, carry)
        return lax.cond(hi - lo == 3, lambda c: step([lo, lo + 1, lo + 2], c),
                        lambda c: _for_chunks(lo, hi, step, c), carry)

    @pl.when(fixed)
    def _():
        def step(js, carry):
            tiles = [load(j) for j in js]
            biases = [bias_fn(j, -bound) for j in js]
            ss = [[_mm_nt(k, q) + b for q in qs] for (k, _), b in zip(tiles, biases)]
            for h in range(len(qs)):
                pv = jnp.zeros((V_ROWS, TQ), F32)
                for st, (_, vT) in zip(ss, tiles):
                    pv = pv + _mm(vT, jnp.exp2(st[h]))
                acc_ref[h] += pv
            return carry

        run(step, 0)

    @pl.when(jnp.logical_not(fixed))
    def _():
        def step(js, ms):
            tiles = [load(j) for j in js]
            return _softmax_step([k for k, _ in tiles], qs, [bias_fn(j, 0.0) for j in js],
                                 [vT for _, vT in tiles], ms, acc_ref)

        run(step, tuple(jnp.full((1, TQ), NEG, F32) for _ in qs))


def _dsa_kernel(qi_ref, ki_ref, wg_ref, q_ref, k_ref, vT_ref, tril_ref, o_ref, sc_ref, acc_ref, kmax_ref,
                *, topk):
    i = pl.program_id(1)
    nchunks = i + 1
    qpos = i * TQ + lax.broadcasted_iota(jnp.int32, (1, TQ), 1)
    rowi = lax.broadcasted_iota(jnp.int32, (TQ, TQ), 0)
    int_min = jnp.int32(-2 ** 31)

    qis = [qi_ref[0, hh] for hh in range(IDX_HEADS)]
    wT = wg_ref[0]

    def idx_chunks(js, carry, diag=False):
        rs = [[_mm_nt(ki_ref[0, _chunk_slice(j), :], qh) for qh in qis] for j in js]
        for j, rj in zip(js, rs):
            sc = jnp.zeros((TQ, TQ), F32)
            for hh in range(IDX_HEADS):
                sc = sc + wT[hh:hh + 1, :] * jnp.maximum(rj[hh], 0.0)
            if diag:
                sc = jnp.where(rowi <= qpos - i * TQ, sc, -jnp.inf)
            sc_ref[_chunk_slice(j), :] = sc
        return carry

    _for_chunks(0, i, idx_chunks, 0)
    idx_chunks([i], 0, diag=True)

    def walk(odd, step, carry):
        def pair(t, carry):
            return step(2 * t + 1, step(2 * t, carry))
        carry = lax.fori_loop(0, nchunks // 2, pair, carry)
        return step(nchunks - 1, carry) if odd else carry

    def count_ge(thr, odd):
        def step(j, acc):
            for r in range(TQ // COUNT_ROWS):
                base = pl.multiple_of(j * TQ + r * COUNT_ROWS, COUNT_ROWS)
                ind = jnp.where(sc_ref[pl.ds(base, COUNT_ROWS), :] >= thr, 1.0, 0.0)
                acc = acc + ind.reshape(COUNT_ROWS // SUBLANES, SUBLANES, TQ).sum(axis=0)
            return acc
        return jnp.sum(walk(odd, step, jnp.zeros((SUBLANES, TQ), F32)), axis=0, keepdims=True)

    take_all = qpos < topk

    def search(odd):
        def key_to_float(u):
            s = u ^ int_min
            return lax.bitcast_convert_type(jnp.where(s < 0, s ^ jnp.int32(0x7FFFFFFF), s), F32)

        def bit_body(t, carry):
            res, n_ge, n_gt = carry
            cand = res | jnp.left_shift(jnp.int32(1), 31 - t)
            n = count_ge(key_to_float(cand), odd)
            ok = n >= topk
            return jnp.where(ok, cand, res), jnp.where(ok, n, n_ge), jnp.where(ok, n_gt, n)

        zero = jnp.zeros((1, TQ), F32)
        res, n_ge, n_gt = lax.fori_loop(0, 32, bit_body, (jnp.zeros((1, TQ), jnp.int32), zero, zero))
        thr = key_to_float(res)

        @pl.when(jnp.max(jnp.where(take_all, 0.0, n_ge)) > topk)
        def _():
            need = jnp.where(take_all, float(2 ** 30), topk - n_gt)
            tril = tril_ref[...]

            def drop(js, before):
                blks = [sc_ref[_chunk_slice(j), :] for j in js]
                counts = [_mm(tril, jnp.where(blk == thr, 1.0, 0.0)) for blk in blks]
                for j, blk, cnt in zip(js, blks, counts):
                    rank = cnt + before
                    sc_ref[_chunk_slice(j), :] = jnp.where((blk == thr) & (rank > need), -jnp.inf, blk)
                    before = rank[TQ - 1:TQ, :]
                return before

            _for_chunks(0, nchunks, drop, zero)

        return thr

    lowest = float(np.finfo(np.float32).min)
    thr = lax.switch(jnp.where(i * TQ + TQ > topk, 1 + nchunks % 2, 0),
                     [lambda: jnp.full((1, TQ), lowest, F32),
                      functools.partial(search, False), functools.partial(search, True)])
    thr = jnp.where(take_all, lowest, thr)

    @pl.when(i == 0)
    def _():
        kmax_ref[...] = jnp.full(kmax_ref.shape, _max_sq_norm(k_ref), F32)

    qs = [q_ref[0, h] for h in range(N_HEADS)]
    _masked_attention(0, nchunks,
                      lambda j: (k_ref[0, _chunk_slice(j), :], vT_ref[0, j]),
                      lambda j, on: jnp.where(sc_ref[_chunk_slice(j), :] >= thr, on, NEG),
                      qs, _score_bound(qs, kmax_ref[0:1, 0:1]), acc_ref)
    o_ref[0] = jnp.concatenate([_softmax_result(acc_ref, h) for h in range(N_HEADS)], axis=0).T


def _dsa_attention(qidx, kidx, wg, qb, kb, vbT, B, S):
    topk = min(DSA_TOPK, S // 4)
    assert topk <= TQ
    nc = S // TQ
    q_spec = pl.BlockSpec((1, N_HEADS, TQ, HEAD_DIM), lambda b, i: (b, 0, i, 0))
    k_spec = pl.BlockSpec((1, S, HEAD_DIM), lambda b, i: (b, 0, 0))
    tril = jnp.asarray(np.tril(np.ones((TQ, TQ), np.float32)), MXU_DTYPE)
    return pl.pallas_call(
        functools.partial(_dsa_kernel, topk=topk),
        grid=(B, nc),
        in_specs=[q_spec, k_spec, pl.BlockSpec((1, WG_ROWS, TQ), lambda b, i: (b, 0, i)), q_spec, k_spec,
                  pl.BlockSpec((1, nc, V_ROWS, TQ), lambda b, i: (b, 0, 0, 0)), _const_spec((TQ, TQ))],
        out_specs=pl.BlockSpec((1, TQ, BRANCH_WIDTH), lambda b, i: (b, i, 0)),
        out_shape=jax.ShapeDtypeStruct((B, S, BRANCH_WIDTH), F32),
        scratch_shapes=[pltpu.VMEM((S, TQ), F32),
                        pltpu.VMEM((N_HEADS, V_ROWS, TQ), F32), pltpu.VMEM((SUBLANES, LANES), F32)],
        compiler_params=_params("parallel", "arbitrary"),
        name="dsa_attention",
    )(qidx, kidx, wg, qb, kb, vbT, tril)


def _compress_kernel(c_ref, pos_ref, w1_ref, w2_ref, o_ref, *, transpose_out):
    n = c_ref.shape[1] // CMP_STRIDE
    top = jnp.zeros((n, CMP_HIDDEN), F32)
    bot = jnp.zeros((n, CMP_HIDDEN), F32)
    for p in range(CMP_STRIDE):
        x = c_ref[0, pl.ds(p, n, stride=CMP_STRIDE), :]
        q = p + CMP_STRIDE
        top = top + _mm(x + pos_ref[p:p + 1, :], w1_ref[p * HEAD_DIM:(p + 1) * HEAD_DIM, :])
        bot = bot + _mm(x + pos_ref[q:q + 1, :], w1_ref[q * HEAD_DIM:(q + 1) * HEAD_DIM, :])
    hid = jax.nn.gelu(top + pltpu.roll(bot, n - 1, 0), approximate=True)
    out = _mm(hid, w2_ref[...])
    rowi = lax.broadcasted_iota(jnp.int32, out.shape, 0)
    out = jnp.where(rowi < n - 1, out, 0.0)
    if transpose_out:
        out = jnp.concatenate([out, jnp.zeros_like(out)], axis=1).T[0:HEAD_DIM, :]
    o_ref[0] = out.astype(o_ref.dtype)


def _compress(tok, pos_emb, w1, w2, transpose_out):
    B, S, _ = tok.shape
    assert CMP_LEN == 2 * CMP_STRIDE
    nch = S // CMP_STRIDE
    out_block = (1, HEAD_DIM, nch) if transpose_out else (1, nch, HEAD_DIM)
    return pl.pallas_call(
        functools.partial(_compress_kernel, transpose_out=transpose_out),
        grid=(B,),
        in_specs=[pl.BlockSpec((1, S, HEAD_DIM), lambda b: (b, 0, 0)),
                  _const_spec(pos_emb.shape), _const_spec(w1.shape), _const_spec(w2.shape)],
        out_specs=pl.BlockSpec(out_block, lambda b: (b, 0, 0)),
        out_shape=jax.ShapeDtypeStruct((B,) + out_block[1:], MXU_DTYPE),
        compiler_params=_params("parallel"),
        name="nsa_compress",
    )(tok, pos_emb, w1.astype(MXU_DTYPE), w2.astype(MXU_DTYPE))


def _nsa_kernel(q_ref, wg_ref, kc_ref, vcT_ref, ovT_ref, ks_ref, vsT_ref, kw_ref, vwT_ref,
                o_ref, selb_ref, acc_s_ref, acc_w_ref, kmax_ref, *, n_cmp, n_sel, topn):
    i = pl.program_id(1)
    qpos = i * TQ + lax.broadcasted_iota(jnp.int32, (1, TQ), 1)
    rowi = lax.broadcasted_iota(jnp.int32, (TQ, TQ), 0)
    heads = [q_ref[0, h] for h in range(N_HEADS)]
    gates = wg_ref[0]

    ncp = kc_ref.shape[1]
    cidx = lax.broadcasted_iota(jnp.int32, (ncp, TQ), 0)
    cmask = (cidx * CMP_STRIDE + (CMP_LEN - 1) <= qpos) & (cidx < n_cmp)
    kc, vcT, ovT = kc_ref[0], vcT_ref[0], ovT_ref[...]
    ss = [jnp.where(cmask, _mm_nt(kc, q), NEG) for q in heads]
    ps = []
    for s in ss:
        m = jnp.max(s, axis=0, keepdims=True)
        e = jnp.exp2(s - m)
        ps.append(e * jnp.where(m > 0.5 * NEG, 1.0 / jnp.sum(e, axis=0, keepdims=True), 0.0))
    vo = jnp.concatenate([vcT, ovT], axis=0)
    o_cmp = []
    imp = jnp.zeros((n_sel, TQ), F32)
    for p in ps:
        p_hi = p.astype(MXU_DTYPE)
        both = _mm(vo, p_hi)
        o_cmp.append(both[0:HEAD_DIM, :])
        imp = imp + both[HEAD_DIM:, :] + _mm(ovT, p - p_hi.astype(F32))

    bidx = lax.broadcasted_iota(jnp.int32, (n_sel, TQ), 0)
    cur = qpos // SEL_LEN
    forced = (bidx == 0) | (bidx == cur) | (bidx == cur - 1)
    work = jnp.where(bidx * SEL_LEN <= qpos, jnp.where(forced, FORCED_SCORE, imp), -jnp.inf)
    sel = jnp.zeros((n_sel, TQ), F32)
    for _ in range(topn):
        m = jnp.max(work, axis=0, keepdims=True)
        first = jnp.min(jnp.where((work == m) & (m > -jnp.inf), bidx, n_sel), axis=0, keepdims=True)
        pick = bidx == first
        sel = jnp.where(pick, 1.0, sel)
        work = jnp.where(pick, -jnp.inf, work)
    selb_ref[...] = jnp.where(sel > 0.0, 0.0, NEG)

    per_chunk = TQ // SEL_LEN

    @pl.when(i == 0)
    def _():
        kmax_ref[...] = jnp.full(kmax_ref.shape, jnp.maximum(_max_sq_norm(ks_ref), _max_sq_norm(kw_ref)),
                                 F32)

    bound = _score_bound(heads, kmax_ref[0:1, 0:1])

    def sel_bias(j, on):
        rows = [jnp.broadcast_to(selb_ref[pl.ds(j * per_chunk + bb, 1), :], (SEL_LEN, TQ))
                for bb in range(per_chunk)]
        return jnp.where(j * TQ + rowi <= qpos, jnp.concatenate(rows, axis=0), NEG) + on

    _masked_attention(0, i + 1, lambda j: (ks_ref[0, _chunk_slice(j), :], vsT_ref[0, j]), sel_bias,
                      heads, bound, acc_s_ref)

    def win_bias(j, on):
        diff = qpos - (j * TQ + rowi)
        return jnp.where((diff >= 0) & (diff < WINDOW), on, NEG)

    _masked_attention(jnp.maximum(i - WINDOW // TQ, 0), i + 1,
                      lambda j: (kw_ref[0, _chunk_slice(j), :], vwT_ref[0, j]), win_bias,
                      heads, bound, acc_w_ref, triple=WINDOW // TQ == 2)

    outs = []
    for h in range(N_HEADS):
        g = lambda br: gates[IDX_HEADS + 3 * h + br:IDX_HEADS + 3 * h + br + 1, :]
        outs.append(g(0) * o_cmp[h] + g(1) * _softmax_result(acc_s_ref, h)
                    + g(2) * _softmax_result(acc_w_ref, h))
    o_ref[0] = jnp.concatenate(outs, axis=0).T


def _nsa_attention(qc, wg, kc, vcT, ksel, vselT, kwin, vwinT, B, S):
    n_cmp = (S - CMP_LEN) // CMP_STRIDE + 1
    n_sel = S // SEL_LEN
    topn = min(SEL_TOPN, n_sel)
    ncp = kc.shape[1]
    cs = np.arange(ncp) * CMP_STRIDE
    ss = np.arange(n_sel) * SEL_LEN
    overlapT = ((cs[None, :] < ss[:, None] + SEL_LEN) & (cs[None, :] + CMP_LEN > ss[:, None])
                & (np.arange(ncp)[None, :] < n_cmp)).astype(np.float32)
    k_spec = pl.BlockSpec((1, S, HEAD_DIM), lambda b, i: (b, 0, 0))
    vT_spec = pl.BlockSpec((1, S // TQ, V_ROWS, TQ), lambda b, i: (b, 0, 0, 0))
    return pl.pallas_call(
        functools.partial(_nsa_kernel, n_cmp=n_cmp, n_sel=n_sel, topn=topn),
        grid=(B, S // TQ),
        in_specs=[pl.BlockSpec((1, N_HEADS, TQ, HEAD_DIM), lambda b, i: (b, 0, i, 0)),
                  pl.BlockSpec((1, WG_ROWS, TQ), lambda b, i: (b, 0, i)),
                  pl.BlockSpec((1, ncp, HEAD_DIM), lambda b, i: (b, 0, 0)),
                  pl.BlockSpec((1, HEAD_DIM, ncp), lambda b, i: (b, 0, 0)),
                  _const_spec((n_sel, ncp)), k_spec, vT_spec, k_spec, vT_spec],
        out_specs=pl.BlockSpec((1, TQ, BRANCH_WIDTH), lambda b, i: (b, i, 0)),
        out_shape=jax.ShapeDtypeStruct((B, S, BRANCH_WIDTH), F32),
        scratch_shapes=[pltpu.VMEM((n_sel, TQ), F32), pltpu.VMEM((N_HEADS, V_ROWS, TQ), F32),
                        pltpu.VMEM((N_HEADS, V_ROWS, TQ), F32), pltpu.VMEM((SUBLANES, LANES), F32)],
        compiler_params=_params("parallel", "arbitrary"),
        name="nsa_attention",
    )(qc, wg, kc, vcT, jnp.asarray(overlapT, MXU_DTYPE), ksel, vselT, kwin, vwinT)


def _mem_kv_kernel(m_ref, g_ref, wk_ref, wvT_ref, ks_ref, vT_ref):
    xn = _rmsnorm(m_ref[0], g_ref[...]).astype(MXU_DTYPE)
    M = xn.shape[0]
    k = _mm(xn, wk_ref[...])
    lane_head = lax.broadcasted_iota(jnp.int32, (1, BRANCH_WIDTH), 1) // HEAD_DIM
    for h in range(N_HEADS):
        ks_ref[0, h * M:(h + 1) * M, :] = jnp.where(lane_head == h, k, 0.0).astype(ks_ref.dtype)
    vT_ref[0] = _mm_nt(wvT_ref[...], xn).astype(vT_ref.dtype)


def _mem_kv(mem, g, w_kv):
    B, M, _ = mem.shape
    wk = w_kv[:, :BRANCH_WIDTH].astype(MXU_DTYPE)
    wvT = w_kv[:, BRANCH_WIDTH:].T.astype(MXU_DTYPE)
    return pl.pallas_call(
        _mem_kv_kernel,
        grid=(B,),
        in_specs=[pl.BlockSpec((1, M, D_MODEL), lambda b: (b, 0, 0)), _const_spec((1, D_MODEL)),
                  _const_spec(wk.shape), _const_spec(wvT.shape)],
        out_specs=[pl.BlockSpec((1, N_HEADS * M, BRANCH_WIDTH), lambda b: (b, 0, 0)),
                   pl.BlockSpec((1, BRANCH_WIDTH, M), lambda b: (b, 0, 0))],
        out_shape=[jax.ShapeDtypeStruct((B, N_HEADS * M, BRANCH_WIDTH), MXU_DTYPE),
                   jax.ShapeDtypeStruct((B, BRANCH_WIDTH, M), MXU_DTYPE)],
        compiler_params=_params("parallel"),
        name="memory_kv",
    )(mem, g.reshape(1, D_MODEL), wk, wvT)


FF_CHUNK = 256


def _tail_kernel(h_ref, oa_ref, ob_ref, oc_ref, gmix_ref, wgm_ref, wup_ref, wout_ref,
                 gq_ref, wq_ref, ks_ref, vT_ref, wo_ref, gffn_ref, win_ref, wff_ref, gfin_ref, o_ref,
                 *, final_norm):
    x = h_ref[...]
    xn = _rmsnorm(x, gmix_ref[...]).astype(MXU_DTYPE)
    merged = jnp.zeros((TM, D_MODEL), F32)
    for j, br_ref in enumerate((oa_ref, ob_ref, oc_ref)):
        gate = jax.nn.sigmoid(_mm(xn, wgm_ref[:, j * D_MODEL:(j + 1) * D_MODEL]))
        merged = merged + gate * _mm(br_ref[...], wup_ref[j])
    x = x + _mm(merged, wout_ref[...])

    q = (_mm(_rmsnorm(x, gq_ref[...]), wq_ref[...]) * SOFTMAX_Q_SCALE).astype(MXU_DTYPE)
    M = vT_ref.shape[2]
    s = _mm_nt(ks_ref[0], q)
    outs = []
    for h in range(N_HEADS):
        sh = s[h * M:(h + 1) * M, :]
        p = jnp.exp2(sh - jnp.max(sh, axis=0, keepdims=True))
        o_h = _mm(vT_ref[0, h * HEAD_DIM:(h + 1) * HEAD_DIM, :], p)
        outs.append(o_h / jnp.sum(p, axis=0, keepdims=True))
    x = x + _mm(jnp.concatenate(outs, axis=0).T, wo_ref[...])

    xn = _rmsnorm(x, gffn_ref[...]).astype(MXU_DTYPE)
    for c in range(D_FF // FF_CHUNK):
        lo, hi = c * FF_CHUNK, (c + 1) * FF_CHUNK
        a = _mm(xn, win_ref[:, lo:hi])
        b = _mm(xn, win_ref[:, D_FF + lo:D_FF + hi])
        x = x + _mm(a * jax.nn.sigmoid(a) * b, wff_ref[lo:hi, :])
    if final_norm:
        x = _rmsnorm(x, gfin_ref[...])
    o_ref[...] = x


def _dense_tail(h, o_a, o_b, o_c, g_mix, w_gm, w_up, w_out, g_q, w_q, ks, vT, w_o,
                g_ffn, w_ffn_in, w_ffn_out, g_final, final_norm, S):
    T = h.shape[0]
    nb = S // TM
    row = lambda w: pl.BlockSpec((TM, w), lambda i: (i, 0))
    gain = _const_spec((1, D_MODEL))
    w_gm, w_up, w_out, w_q, w_o, w_ffn_in, w_ffn_out = [
        w.astype(MXU_DTYPE) for w in (w_gm, w_up, w_out, w_q, w_o, w_ffn_in, w_ffn_out)]
    per_batch = lambda a: pl.BlockSpec((1,) + a.shape[1:], lambda i: (i // nb, 0, 0))
    g = lambda v: v.reshape(1, D_MODEL)
    return pl.pallas_call(
        functools.partial(_tail_kernel, final_norm=final_norm),
        grid=(T // TM,),
        in_specs=[row(D_MODEL), row(BRANCH_WIDTH), row(BRANCH_WIDTH), row(BRANCH_WIDTH),
                  gain, _const_spec(w_gm.shape), _const_spec(w_up.shape), _const_spec(w_out.shape),
                  gain, _const_spec(w_q.shape), per_batch(ks), per_batch(vT), _const_spec(w_o.shape),
                  gain, _const_spec(w_ffn_in.shape), _const_spec(w_ffn_out.shape), gain],
        out_specs=row(D_MODEL),
        out_shape=jax.ShapeDtypeStruct((T, D_MODEL), F32),
        compiler_params=_params("parallel"),
        name="dense_tail",
    )(h, o_a, o_b, o_c, g(g_mix), w_gm, w_up, w_out, g(g_q), w_q, ks, vT, w_o,
      g(g_ffn), w_ffn_in, w_ffn_out, g(g_final))


def _rope_tables(S):
    inv = ROPE_THETA ** (-jnp.arange(0, HEAD_DIM, 2, dtype=F32) / HEAD_DIM)
    ang = jnp.arange(S, dtype=F32)[:, None] * inv[None, :]
    cos, sin = jnp.cos(ang), jnp.sin(ang)
    return (jnp.concatenate([cos, cos, cos, cos], axis=1),
            jnp.concatenate([-sin, sin, -sin, sin], axis=1))


def _mixers(h, B, S, g, w_in, cmp_k, cmp_v, cos_t, sin_t):
    (qa, ka, vaT, qb, qidx, qc, kb, kidx, ksel, kwin, vbT, vselT, vwinT, wg, ck, cv) = _in_projection(
        h, g, w_in, cos_t, sin_t, B, S)
    o_a = _stick_breaking(qa, ka, vaT)
    o_b = _dsa_attention(qidx, kidx, wg, qb, kb, vbT, B, S)
    kc = _compress(ck, *cmp_k, transpose_out=False)
    vcT = _compress(cv, *cmp_v, transpose_out=True)
    o_c = _nsa_attention(qc, wg, kc, vcT, ksel, vselT, kwin, vwinT, B, S)
    return [o.reshape(B * S, BRANCH_WIDTH) for o in (o_a, o_b, o_c)]


def kernel(x, mem, norm_mix, w_in, cmp_pos_k, cmp_w1_k, cmp_w2_k, cmp_pos_v, cmp_w1_v, cmp_w2_v,
           w_up, w_out, norm_mem_q, norm_mem_kv, w_mem_q, w_mem_kv, w_mem_o,
           norm_ffn, w_ffn_in, w_ffn_out, norm_final):
    B, S, D = x.shape
    depth = w_in.shape[0]
    assert D == D_MODEL and S % TM == 0 and S % TQ == 0 and WINDOW % TQ == 0
    cos_t, sin_t = _rope_tables(S)
    h = x.reshape(B * S, D)
    for l in range(depth):
        o_a, o_b, o_c = _mixers(h, B, S, norm_mix[l], w_in[l],
                                (cmp_pos_k[l], cmp_w1_k[l], cmp_w2_k[l]),
                                (cmp_pos_v[l], cmp_w1_v[l], cmp_w2_v[l]), cos_t, sin_t)
        ks, vT = _mem_kv(mem, norm_mem_kv[l], w_mem_kv[l])
        h = _dense_tail(h, o_a, o_b, o_c, norm_mix[l], w_in[l][:, _OFF['g_merge'][0]:], w_up[l], w_out[l],
                        norm_mem_q[l], w_mem_q[l], ks, vT, w_mem_o[l],
                        norm_ffn[l], w_ffn_in[l], w_ffn_out[l], norm_final, l == depth - 1, S)
    return h.reshape(B, S, D)
```

```python
import functools

import jax
import jax.numpy as jnp
import numpy as np
from jax import lax
from jax.experimental import pallas as pl
from jax.experimental.pallas import tpu as pltpu

F32 = jnp.float32
MXU_DTYPE = jnp.bfloat16

D_MODEL = 1024
HEAD_DIM = 64
N_HEADS = 4
BRANCH_WIDTH = N_HEADS * HEAD_DIM
IDX_HEADS = 4
DSA_TOPK = 256
CMP_LEN = 32
CMP_STRIDE = 16
CMP_HIDDEN = 128
SEL_LEN = 64
SEL_TOPN = 16
WINDOW = 512
D_FF = 2816
ROPE_THETA = 10000.0
RMS_EPS = 1e-6
FORCED_SCORE = 1e4
Q_SCALE = HEAD_DIM ** -0.5
LOG2E = 1.4426950408889634
SOFTMAX_Q_SCALE = Q_SCALE * LOG2E
W_IDX_SCALE = (IDX_HEADS * HEAD_DIM) ** -0.5
NEG = -1e30

LANES = 128
SUBLANES = 8
VMEM_LIMIT = 56 * 1024 * 1024
TM = 512
TQ = 256

_OFF = {}
_o = 0
for _n, _w in (('q_a', 256), ('k_a', 256), ('v_a', 256), ('q_b', 256), ('k_b', 64), ('v_b', 64),
               ('q_idx', 256), ('k_idx', 64), ('w_idx', 4), ('q_c', 256), ('k_cmp', 64),
               ('v_cmp', 64), ('k_sel', 64), ('v_sel', 64), ('k_win', 64), ('v_win', 64),
               ('g_c', 12), ('g_merge', 3 * D_MODEL)):
    _OFF[_n] = (_o, _o + _w)
    _o += _w
D_IN = _o


def _mm(a, b):
    return jnp.dot(a.astype(MXU_DTYPE), b.astype(MXU_DTYPE), preferred_element_type=F32)


def _mm_nt(a, b):
    return lax.dot_general(a.astype(MXU_DTYPE), b.astype(MXU_DTYPE),
                           (((1,), (1,)), ((), ())), preferred_element_type=F32)


def _params(*sem):
    return pltpu.CompilerParams(dimension_semantics=sem, vmem_limit_bytes=VMEM_LIMIT)


def _const_spec(shape):
    nd = len(shape)
    return pl.BlockSpec(shape, lambda *_: (0,) * nd, pipeline_mode=pl.Buffered(1))


def _rmsnorm(x, g):
    return x * lax.rsqrt(jnp.mean(x * x, axis=-1, keepdims=True) + RMS_EPS) * g


def _rope_lanes(x, cos, sin):
    lane = lax.broadcasted_iota(jnp.int32, (1, LANES), 1)
    first_half = (lane % HEAD_DIM) < (HEAD_DIM // 2)
    out = []
    for g in range(x.shape[1] // LANES):
        xg = x[:, g * LANES:(g + 1) * LANES]
        rot = jnp.where(first_half, pltpu.roll(xg, LANES - HEAD_DIM // 2, 1),
                        pltpu.roll(xg, HEAD_DIM // 2, 1))
        out.append(xg * cos + rot * sin)
    return out


WG_ROW = BRANCH_WIDTH + 3 * HEAD_DIM
WG_ROWS = IDX_HEADS + 3 * N_HEADS
WT_ROWS = 512
V_ROWS = HEAD_DIM + 2 * SUBLANES


def _inproj_kernel(h_ref, g_ref, cos_ref, sin_ref, wa_ref, wr_ref, wc_ref, wt_ref,
                   qa_ref, ka_ref, vaT_ref, qb_ref, qi_ref, qc_ref, kb_ref, ki_ref, ks_ref, kw_ref,
                   vbT_ref, vsT_ref, vwT_ref, wg_ref, ck_ref, cv_ref):
    xn = _rmsnorm(h_ref[...], g_ref[...]).astype(MXU_DTYPE)
    cos, sin = cos_ref[...], sin_ref[...]
    head = lambda y, h: y[:, h * HEAD_DIM:(h + 1) * HEAD_DIM]

    ya = _mm(xn, wa_ref[...])
    for h in range(N_HEADS):
        qa_ref[0, h] = (head(ya, h) * Q_SCALE).astype(qa_ref.dtype)
        ka_ref[0, h] = head(ya, N_HEADS + h).astype(ka_ref.dtype)

    yr = _rope_lanes(_mm(xn, wr_ref[...]), cos, sin)
    for g, (q_ref, scale) in enumerate(((qb_ref, SOFTMAX_Q_SCALE), (qi_ref, 1.0),
                                        (qc_ref, SOFTMAX_Q_SCALE))):
        for half in range(2):
            og = yr[2 * g + half] * scale
            q_ref[0, 2 * half] = head(og, 0).astype(q_ref.dtype)
            q_ref[0, 2 * half + 1] = head(og, 1).astype(q_ref.dtype)
    kb_ref[0] = head(yr[6], 0).astype(kb_ref.dtype)
    ki_ref[0] = head(yr[6], 1).astype(ki_ref.dtype)
    ks_ref[0] = head(yr[7], 0).astype(ks_ref.dtype)
    kw_ref[0] = head(yr[7], 1).astype(kw_ref.dtype)

    yc = _mm(xn, wc_ref[...])
    ck_ref[0] = head(_rope_lanes(yc, cos, sin)[0], 0)
    cv_ref[0] = head(yc, 1)

    yT = _mm_nt(wt_ref[...], xn)
    ones_row = (lax.broadcasted_iota(jnp.int32, (V_ROWS - HEAD_DIM, TQ), 0) == 0).astype(vbT_ref.dtype)
    for c in range(TM // TQ):
        cols = slice(c * TQ, (c + 1) * TQ)
        for h in range(N_HEADS):
            vaT_ref[0, h, c] = yT[h * HEAD_DIM:(h + 1) * HEAD_DIM, cols].astype(vaT_ref.dtype)
        for n, v_ref in enumerate((vbT_ref, vsT_ref, vwT_ref)):
            lo = BRANCH_WIDTH + n * HEAD_DIM
            v_ref[0, c, 0:HEAD_DIM, :] = yT[lo:lo + HEAD_DIM, cols].astype(v_ref.dtype)
            v_ref[0, c, HEAD_DIM:, :] = ones_row
    ys = yT[WG_ROW:WG_ROW + WG_ROWS, :]
    row = lax.broadcasted_iota(jnp.int32, ys.shape, 0)
    wg_ref[0] = jnp.where(row < IDX_HEADS, ys * W_IDX_SCALE, jax.nn.sigmoid(ys))


def _in_projection(h, g, w_in, cos_t, sin_t, B, S):
    cols = lambda *names: jnp.concatenate([w_in[:, _OFF[n][0]:_OFF[n][1]] for n in names], axis=1)
    wa = cols('q_a', 'k_a').astype(MXU_DTYPE)
    wr = cols('q_b', 'q_idx', 'q_c', 'k_b', 'k_idx', 'k_sel', 'k_win').astype(MXU_DTYPE)
    wc = cols('k_cmp', 'v_cmp').astype(MXU_DTYPE)
    wt = cols('v_a', 'v_b', 'v_sel', 'v_win', 'w_idx', 'g_c')
    wt = jnp.pad(wt, ((0, 0), (0, WT_ROWS - wt.shape[1]))).T.astype(MXU_DTYPE)
    nb, nc, cpb = S // TM, S // TQ, TM // TQ
    bf = lambda *shape: jax.ShapeDtypeStruct(shape, MXU_DTYPE)
    f32 = lambda *shape: jax.ShapeDtypeStruct(shape, F32)
    heads_spec = pl.BlockSpec((1, N_HEADS, TM, HEAD_DIM), lambda i: (i // nb, 0, i % nb, 0))
    tok_spec = pl.BlockSpec((1, TM, HEAD_DIM), lambda i: (i // nb, i % nb, 0))
    vT_spec = pl.BlockSpec((1, cpb, V_ROWS, TQ), lambda i: (i // nb, i % nb, 0, 0))
    pos = pl.BlockSpec((TM, LANES), lambda i: (i % nb, 0))
    q4, k3, v4 = bf(B, N_HEADS, S, HEAD_DIM), bf(B, S, HEAD_DIM), bf(B, nc, V_ROWS, TQ)
    return pl.pallas_call(
        _inproj_kernel,
        grid=(B * nb,),
        in_specs=[pl.BlockSpec((TM, D_MODEL), lambda i: (i, 0)), _const_spec((1, D_MODEL)), pos, pos,
                  _const_spec(wa.shape), _const_spec(wr.shape), _const_spec(wc.shape),
                  _const_spec(wt.shape)],
        out_specs=[heads_spec, heads_spec,
                   pl.BlockSpec((1, N_HEADS, cpb, HEAD_DIM, TQ), lambda i: (i // nb, 0, i % nb, 0, 0)),
                   heads_spec, heads_spec, heads_spec, tok_spec, tok_spec, tok_spec, tok_spec,
                   vT_spec, vT_spec, vT_spec,
                   pl.BlockSpec((1, WG_ROWS, TM), lambda i: (i // nb, 0, i % nb)), tok_spec, tok_spec],
        out_shape=[q4, q4, bf(B, N_HEADS, nc, HEAD_DIM, TQ), q4, q4, q4, k3, k3, k3, k3, v4, v4, v4,
                   f32(B, WG_ROWS, S), f32(B, S, HEAD_DIM), f32(B, S, HEAD_DIM)],
        compiler_params=_params("parallel"),
        name="in_projection",
    )(h, g.reshape(1, D_MODEL), cos_t, sin_t, wa, wr, wc, wt)


def _chunk_slice(j):
    return pl.ds(pl.multiple_of(j * TQ, TQ), TQ)


EXP_UNDERFLOW = 110.0
SCORE_BOUND_SLACK = 1.05


def _sb_kernel(q_ref, k_ref, vT_ref, tri_ref, o_ref, acc_ref, kmax_ref):
    i = pl.program_id(1)
    row = lax.broadcasted_iota(jnp.int32, (TQ, TQ), 0)
    col = lax.broadcasted_iota(jnp.int32, (TQ, TQ), 1)
    strict = row < col
    tri = tri_ref[...]
    qs = [q_ref[0, h] for h in range(N_HEADS)]
    acc_ref[...] = jnp.zeros(acc_ref.shape, F32)

    def chunks(js, cs, diag=False):
        heads = range(N_HEADS)
        zs = [[_mm_nt(k_ref[0, h, _chunk_slice(j), :], qs[h]) for h in heads] for j in js]
        cums = []
        for zj in zs:
            cums.append([])
            for z in zj:
                sp = jnp.maximum(z, 0.0) + jnp.log(1.0 + jnp.exp(-jnp.abs(z)))
                if diag:
                    sp = jnp.where(strict, sp, 0.0)
                hi = sp.astype(MXU_DTYPE)
                lo = (sp - hi.astype(F32)).astype(MXU_DTYPE)
                cums[-1].append(_mm(tri, hi) + _mm(tri, lo))
        outs = [jnp.zeros((HEAD_DIM, TQ), F32) for _ in heads]
        cs = list(cs)
        for n, j in enumerate(js):
            for h in heads:
                w = jnp.exp(zs[n][h] - cums[n][h] - cs[h])
                if diag:
                    w = jnp.where(strict, w, 0.0)
                outs[h] = outs[h] + _mm(vT_ref[0, h, j], w)
                cs[h] = cs[h] + cums[n][h][0:1, :]
        for h in heads:
            acc_ref[h] += outs[h]
        return tuple(cs)

    ones = jnp.ones((SUBLANES, HEAD_DIM), F32)

    def sq_norms(x):
        xf = x.astype(F32)
        return _mm_nt(ones, xf * xf)[0:1, :]

    @pl.when(i == 0)
    def _():
        for h in range(N_HEADS):
            def body(j, m):
                return jnp.maximum(m, sq_norms(k_ref[0, h, _chunk_slice(j), :]))
            m = lax.fori_loop(0, k_ref.shape[2] // TQ, body, jnp.zeros((1, TQ), F32))
            kmax_ref[h] = jnp.full((SUBLANES, LANES), jnp.max(m), F32)

    zbound = [jnp.sqrt(sq_norms(qs[h]) * kmax_ref[h][0:1, 0:1]) * SCORE_BOUND_SLACK for h in range(N_HEADS)]

    def negligible(cs):
        slack = cs[0] - zbound[0]
        for h in range(1, N_HEADS):
            slack = jnp.minimum(slack, cs[h] - zbound[h])
        return (jnp.min(slack) > EXP_UNDERFLOW).astype(jnp.int32)

    cs = chunks([i], tuple(jnp.zeros((1, TQ), F32) for _ in range(N_HEADS)), diag=True)

    def more(carry):
        j, done, _ = carry
        return (j >= 0) & (done == 0)

    def step(carry):
        j, _, cs = carry
        cs = chunks([j], cs)
        return j - 1, negligible(cs), cs

    lax.while_loop(more, step, (i - 1, negligible(cs), cs))
    o_ref[0] = acc_ref[...].reshape(BRANCH_WIDTH, TQ).T


def _stick_breaking(q, k, vT):
    B, H, S, _ = q.shape
    tri = jnp.asarray(np.triu(np.ones((TQ, TQ), np.float32)), MXU_DTYPE)
    return pl.pallas_call(
        _sb_kernel,
        grid=(B, S // TQ),
        in_specs=[pl.BlockSpec((1, H, TQ, HEAD_DIM), lambda b, i: (b, 0, i, 0)),
                  pl.BlockSpec((1, H, S, HEAD_DIM), lambda b, i: (b, 0, 0, 0)),
                  pl.BlockSpec((1, H, S // TQ, HEAD_DIM, TQ), lambda b, i: (b, 0, 0, 0, 0)),
                  _const_spec((TQ, TQ))],
        out_specs=pl.BlockSpec((1, TQ, BRANCH_WIDTH), lambda b, i: (b, i, 0)),
        out_shape=jax.ShapeDtypeStruct((B, S, BRANCH_WIDTH), F32),
        scratch_shapes=[pltpu.VMEM((H, HEAD_DIM, TQ), F32), pltpu.VMEM((H, SUBLANES, LANES), F32)],
        compiler_params=_params("parallel", "arbitrary"),
        name="stick_breaking",
    )(q, k, vT, tri)


COUNT_ROWS = 64
PACKED_ROWS = 2 * SUBLANES
FINE_STEPS = 18


def _softmax_step(ks, qs, biases, vTs, ms, acc_ref):
    heads = range(len(qs))
    ss = [[_mm_nt(k, q) + b for q in qs] for k, b in zip(ks, biases)]
    m_new = []
    for h in heads:
        m = ms[h]
        for st in ss:
            m = jnp.maximum(m, jnp.max(st[h], axis=0, keepdims=True))
        m_new.append(m)
    pvs = []
    for h in heads:
        pv = jnp.zeros((V_ROWS, TQ), F32)
        for st, vT in zip(ss, vTs):
            pv = pv + _mm(vT, jnp.exp2(st[h] - m_new[h]))
        pvs.append(pv)
    for h in heads:
        acc_ref[h] = jnp.exp2(ms[h] - m_new[h]) * acc_ref[h] + pvs[h]
    return tuple(m_new)


def _for_chunks(lo, hi, step, ms):
    odd = (hi - lo) % 2
    ms = lax.cond(odd == 1, lambda ms: step([lo], ms), lambda ms: ms, ms)
    first = lo + odd
    return lax.fori_loop(0, (hi - lo) // 2,
                         lambda t, ms: step([first + 2 * t, first + 2 * t + 1], ms), ms)


def _softmax_result(acc_ref, h):
    return acc_ref[h, 0:HEAD_DIM, :] / acc_ref[h, HEAD_DIM:HEAD_DIM + 1, :]


MAX_FIXED_SHIFT = 45.0


def _sq_norms(x):
    xf = x.astype(F32)
    return _mm_nt(jnp.ones((SUBLANES, HEAD_DIM), F32), xf * xf)[0:1, :]


def _max_sq_norm(k_ref):
    def body(j, m):
        return jnp.maximum(m, _sq_norms(k_ref[0, _chunk_slice(j), :]))
    return jnp.max(lax.fori_loop(0, k_ref.shape[1] // TQ, body, jnp.zeros((1, TQ), F32)))


def _score_bound(qs, kmax2):
    qn2 = _sq_norms(qs[0])
    for q in qs[1:]:
        qn2 = jnp.maximum(qn2, _sq_norms(q))
    return jnp.sqrt(qn2 * kmax2) * SCORE_BOUND_SLACK


def _masked_attention(lo, hi, load, bias_fn, qs, bound, acc_ref, triple=False):
    acc_ref[...] = jnp.zeros(acc_ref.shape, F32)
    fixed = jnp.max(bound) < MAX_FIXED_SHIFT

    def run(step, carry):
        if not triple:
            return _for_chunks(lo, hi, step, carry)
        return lax.cond(hi - lo == 3, lambda c: step([lo, lo + 1, lo + 2], c),
                        lambda c: _for_chunks(lo, hi, step, c), carry)

    @pl.when(fixed)
    def _():
        def step(js, carry):
            tiles = [load(j) for j in js]
            biases = [bias_fn(j, -bound) for j in js]
            ss = [[_mm_nt(k, q) + b for q in qs] for (k, _), b in zip(tiles, biases)]
            for h in range(len(qs)):
                pv = jnp.zeros((V_ROWS, TQ), F32)
                for st, (_, vT) in zip(ss, tiles):
                    pv = pv + _mm(vT, jnp.exp2(st[h]))
                acc_ref[h] += pv
            return carry

        run(step, 0)

    @pl.when(jnp.logical_not(fixed))
    def _():
        def step(js, ms):
            tiles = [load(j) for j in js]
            return _softmax_step([k for k, _ in tiles], qs, [bias_fn(j, 0.0) for j in js],
                                 [vT for _, vT in tiles], ms, acc_ref)

        run(step, tuple(jnp.full((1, TQ), NEG, F32) for _ in qs))


def _dsa_kernel(qi_ref, ki_ref, wg_ref, q_ref, k_ref, vT_ref, tril_ref, o_ref, sc_ref, rd_ref, acc_ref,
                kmax_ref, *, topk):
    i = pl.program_id(1)
    nchunks = i + 1
    qpos = i * TQ + lax.broadcasted_iota(jnp.int32, (1, TQ), 1)
    rowi = lax.broadcasted_iota(jnp.int32, (TQ, TQ), 0)
    int_min = jnp.int32(-2 ** 31)

    qis = [qi_ref[0, hh] for hh in range(IDX_HEADS)]
    wT = wg_ref[0]

    def idx_chunks(js, carry, diag=False):
        rs = [[_mm_nt(ki_ref[0, _chunk_slice(j), :], qh) for qh in qis] for j in js]
        for j, rj in zip(js, rs):
            sc = jnp.zeros((TQ, TQ), F32)
            for hh in range(IDX_HEADS):
                sc = sc + wT[hh:hh + 1, :] * jnp.maximum(rj[hh], 0.0)
            if diag:
                sc = jnp.where(rowi <= qpos - i * TQ, sc, -jnp.inf)
            sc_ref[_chunk_slice(j), :] = sc
            rd_ref[_chunk_slice(j), :] = sc.astype(jnp.bfloat16)
        return carry

    _for_chunks(0, i, idx_chunks, 0)
    idx_chunks([i], 0, diag=True)

    def walk(odd, step, carry):
        def pair(t, carry):
            return step(2 * t + 1, step(2 * t, carry))
        carry = lax.fori_loop(0, nchunks // 2, pair, carry)
        return step(nchunks - 1, carry) if odd else carry

    def count_ge(thr, odd):
        def step(j, acc):
            for r in range(TQ // COUNT_ROWS):
                base = pl.multiple_of(j * TQ + r * COUNT_ROWS, COUNT_ROWS)
                ind = jnp.where(sc_ref[pl.ds(base, COUNT_ROWS), :] >= thr, 1.0, 0.0)
                acc = acc + ind.reshape(COUNT_ROWS // SUBLANES, SUBLANES, TQ).sum(axis=0)
            return acc
        return jnp.sum(walk(odd, step, jnp.zeros((SUBLANES, TQ), F32)), axis=0, keepdims=True)

    def count_rounded_ge(thr16, odd):
        one, zero = jnp.ones((), jnp.bfloat16), jnp.zeros((), jnp.bfloat16)

        def step(j, acc):
            ind = jnp.where(rd_ref[_chunk_slice(j), :] >= thr16, one, zero)
            parts = [ind[g * PACKED_ROWS:(g + 1) * PACKED_ROWS] for g in range(TQ // PACKED_ROWS)]
            while len(parts) > 1:
                parts = [a + b for a, b in zip(parts[0::2], parts[1::2])]
            return acc + parts[0].astype(F32)
        return jnp.sum(walk(odd, step, jnp.zeros((PACKED_ROWS, TQ), F32)), axis=0, keepdims=True)

    take_all = qpos < topk

    def search(odd):
        def key_to_float(s):
            return lax.bitcast_convert_type(jnp.where(s < 0, s ^ jnp.int32(0x7FFFFFFF), s), F32)

        def coarse(t, res):
            cand = res | jnp.left_shift(jnp.int32(1), 31 - t)
            bits = lax.bitcast_convert_type(key_to_float(cand ^ int_min), jnp.int32) & jnp.int32(-65536)
            thr16 = lax.bitcast_convert_type(bits, F32).astype(jnp.bfloat16)
            return jnp.where(count_rounded_ge(thr16, odd) >= topk, cand, res)

        s16 = lax.fori_loop(0, 16, coarse, jnp.zeros((1, TQ), jnp.int32)) ^ int_min
        lo, hi = s16 - 65536, s16 + 131072

        def fine(_, carry):
            lo, hi, n_ge, n_gt = carry
            mid = lo + jnp.right_shift(hi - lo, 1)
            n = count_ge(key_to_float(mid), odd)
            ok = n >= topk
            return jnp.where(ok, mid, lo), jnp.where(ok, hi, mid), jnp.where(ok, n, n_ge), jnp.where(ok, n_gt, n)

        lo, hi, n_ge, n_gt = lax.fori_loop(
            0, FINE_STEPS, fine,
            (lo, hi, jnp.full((1, TQ), float(topk), F32), count_ge(key_to_float(hi), odd)))
        thr = key_to_float(lo)
        zero = jnp.zeros((1, TQ), F32)

        @pl.when(jnp.max(jnp.where(take_all, 0.0, n_ge)) > topk)
        def _():
            need = jnp.where(take_all, float(2 ** 30), topk - n_gt)
            tril = tril_ref[...]

            def drop(js, before):
                blks = [sc_ref[_chunk_slice(j), :] for j in js]
                counts = [_mm(tril, jnp.where(blk == thr, 1.0, 0.0)) for blk in blks]
                for j, blk, cnt in zip(js, blks, counts):
                    rank = cnt + before
                    sc_ref[_chunk_slice(j), :] = jnp.where((blk == thr) & (rank > need), -jnp.inf, blk)
                    before = rank[TQ - 1:TQ, :]
                return before

            _for_chunks(0, nchunks, drop, zero)

        return thr

    lowest = float(np.finfo(np.float32).min)
    thr = lax.switch(jnp.where(i * TQ + TQ > topk, 1 + nchunks % 2, 0),
                     [lambda: jnp.full((1, TQ), lowest, F32),
                      functools.partial(search, False), functools.partial(search, True)])
    thr = jnp.where(take_all, lowest, thr)

    @pl.when(i == 0)
    def _():
        kmax_ref[...] = jnp.full(kmax_ref.shape, _max_sq_norm(k_ref), F32)

    qs = [q_ref[0, h] for h in range(N_HEADS)]
    _masked_attention(0, nchunks,
                      lambda j: (k_ref[0, _chunk_slice(j), :], vT_ref[0, j]),
                      lambda j, on: jnp.where(sc_ref[_chunk_slice(j), :] >= thr, on, NEG),
                      qs, _score_bound(qs, kmax_ref[0:1, 0:1]), acc_ref)
    o_ref[0] = jnp.concatenate([_softmax_result(acc_ref, h) for h in range(N_HEADS)], axis=0).T


def _dsa_attention(qidx, kidx, wg, qb, kb, vbT, B, S):
    topk = min(DSA_TOPK, S // 4)
    assert topk <= TQ
    nc = S // TQ
    q_spec = pl.BlockSpec((1, N_HEADS, TQ, HEAD_DIM), lambda b, i: (b, 0, i, 0))
    k_spec = pl.BlockSpec((1, S, HEAD_DIM), lambda b, i: (b, 0, 0))
    tril = jnp.asarray(np.tril(np.ones((TQ, TQ), np.float32)), MXU_DTYPE)
    return pl.pallas_call(
        functools.partial(_dsa_kernel, topk=topk),
        grid=(B, nc),
        in_specs=[q_spec, k_spec, pl.BlockSpec((1, WG_ROWS, TQ), lambda b, i: (b, 0, i)), q_spec, k_spec,
                  pl.BlockSpec((1, nc, V_ROWS, TQ), lambda b, i: (b, 0, 0, 0)), _const_spec((TQ, TQ))],
        out_specs=pl.BlockSpec((1, TQ, BRANCH_WIDTH), lambda b, i: (b, i, 0)),
        out_shape=jax.ShapeDtypeStruct((B, S, BRANCH_WIDTH), F32),
        scratch_shapes=[pltpu.VMEM((S, TQ), F32), pltpu.VMEM((S, TQ), jnp.bfloat16),
                        pltpu.VMEM((N_HEADS, V_ROWS, TQ), F32), pltpu.VMEM((SUBLANES, LANES), F32)],
        compiler_params=_params("parallel", "arbitrary"),
        name="dsa_attention",
    )(qidx, kidx, wg, qb, kb, vbT, tril)


def _compress_kernel(c_ref, pos_ref, w1_ref, w2_ref, o_ref, *, transpose_out):
    n = c_ref.shape[1] // CMP_STRIDE
    top = jnp.zeros((n, CMP_HIDDEN), F32)
    bot = jnp.zeros((n, CMP_HIDDEN), F32)
    for p in range(CMP_STRIDE):
        x = c_ref[0, pl.ds(p, n, stride=CMP_STRIDE), :]
        q = p + CMP_STRIDE
        top = top + _mm(x + pos_ref[p:p + 1, :], w1_ref[p * HEAD_DIM:(p + 1) * HEAD_DIM, :])
        bot = bot + _mm(x + pos_ref[q:q + 1, :], w1_ref[q * HEAD_DIM:(q + 1) * HEAD_DIM, :])
    hid = jax.nn.gelu(top + pltpu.roll(bot, n - 1, 0), approximate=True)
    out = _mm(hid, w2_ref[...])
    rowi = lax.broadcasted_iota(jnp.int32, out.shape, 0)
    out = jnp.where(rowi < n - 1, out, 0.0)
    if transpose_out:
        out = jnp.concatenate([out, jnp.zeros_like(out)], axis=1).T[0:HEAD_DIM, :]
    o_ref[0] = out.astype(o_ref.dtype)


def _compress(tok, pos_emb, w1, w2, transpose_out):
    B, S, _ = tok.shape
    assert CMP_LEN == 2 * CMP_STRIDE
    nch = S // CMP_STRIDE
    out_block = (1, HEAD_DIM, nch) if transpose_out else (1, nch, HEAD_DIM)
    return pl.pallas_call(
        functools.partial(_compress_kernel, transpose_out=transpose_out),
        grid=(B,),
        in_specs=[pl.BlockSpec((1, S, HEAD_DIM), lambda b: (b, 0, 0)),
                  _const_spec(pos_emb.shape), _const_spec(w1.shape), _const_spec(w2.shape)],
        out_specs=pl.BlockSpec(out_block, lambda b: (b, 0, 0)),
        out_shape=jax.ShapeDtypeStruct((B,) + out_block[1:], MXU_DTYPE),
        compiler_params=_params("parallel"),
        name="nsa_compress",
    )(tok, pos_emb, w1.astype(MXU_DTYPE), w2.astype(MXU_DTYPE))


def _nsa_kernel(q_ref, wg_ref, kc_ref, vcT_ref, ovT_ref, ks_ref, vsT_ref, kw_ref, vwT_ref,
                o_ref, selb_ref, acc_s_ref, acc_w_ref, kmax_ref, *, n_cmp, n_sel, topn):
    i = pl.program_id(1)
    qpos = i * TQ + lax.broadcasted_iota(jnp.int32, (1, TQ), 1)
    rowi = lax.broadcasted_iota(jnp.int32, (TQ, TQ), 0)
    heads = [q_ref[0, h] for h in range(N_HEADS)]
    gates = wg_ref[0]

    ncp = kc_ref.shape[1]
    cidx = lax.broadcasted_iota(jnp.int32, (ncp, TQ), 0)
    cmask = (cidx * CMP_STRIDE + (CMP_LEN - 1) <= qpos) & (cidx < n_cmp)
    kc, vcT, ovT = kc_ref[0], vcT_ref[0], ovT_ref[...]
    ss = [jnp.where(cmask, _mm_nt(kc, q), NEG) for q in heads]
    ps = []
    for s in ss:
        m = jnp.max(s, axis=0, keepdims=True)
        e = jnp.exp2(s - m)
        ps.append(e * jnp.where(m > 0.5 * NEG, 1.0 / jnp.sum(e, axis=0, keepdims=True), 0.0))
    vo = jnp.concatenate([vcT, ovT], axis=0)
    o_cmp = []
    imp = jnp.zeros((n_sel, TQ), F32)
    for p in ps:
        p_hi = p.astype(MXU_DTYPE)
        both = _mm(vo, p_hi)
        o_cmp.append(both[0:HEAD_DIM, :])
        imp = imp + both[HEAD_DIM:, :] + _mm(ovT, p - p_hi.astype(F32))

    bidx = lax.broadcasted_iota(jnp.int32, (n_sel, TQ), 0)
    cur = qpos // SEL_LEN
    forced = (bidx == 0) | (bidx == cur) | (bidx == cur - 1)
    work = jnp.where(bidx * SEL_LEN <= qpos, jnp.where(forced, FORCED_SCORE, imp), -jnp.inf)
    sel = jnp.zeros((n_sel, TQ), F32)
    for _ in range(topn):
        m = jnp.max(work, axis=0, keepdims=True)
        first = jnp.min(jnp.where((work == m) & (m > -jnp.inf), bidx, n_sel), axis=0, keepdims=True)
        pick = bidx == first
        sel = jnp.where(pick, 1.0, sel)
        work = jnp.where(pick, -jnp.inf, work)
    selb_ref[...] = jnp.where(sel > 0.0, 0.0, NEG)

    per_chunk = TQ // SEL_LEN

    @pl.when(i == 0)
    def _():
        kmax_ref[...] = jnp.full(kmax_ref.shape, jnp.maximum(_max_sq_norm(ks_ref), _max_sq_norm(kw_ref)),
                                 F32)

    bound = _score_bound(heads, kmax_ref[0:1, 0:1])

    def sel_bias(j, on):
        rows = [jnp.broadcast_to(selb_ref[pl.ds(j * per_chunk + bb, 1), :], (SEL_LEN, TQ))
                for bb in range(per_chunk)]
        return jnp.where(j * TQ + rowi <= qpos, jnp.concatenate(rows, axis=0), NEG) + on

    _masked_attention(0, i + 1, lambda j: (ks_ref[0, _chunk_slice(j), :], vsT_ref[0, j]), sel_bias,
                      heads, bound, acc_s_ref)

    def win_bias(j, on):
        diff = qpos - (j * TQ + rowi)
        return jnp.where((diff >= 0) & (diff < WINDOW), on, NEG)

    _masked_attention(jnp.maximum(i - WINDOW // TQ, 0), i + 1,
                      lambda j: (kw_ref[0, _chunk_slice(j), :], vwT_ref[0, j]), win_bias,
                      heads, bound, acc_w_ref, triple=WINDOW // TQ == 2)

    outs = []
    for h in range(N_HEADS):
        g = lambda br: gates[IDX_HEADS + 3 * h + br:IDX_HEADS + 3 * h + br + 1, :]
        outs.append(g(0) * o_cmp[h] + g(1) * _softmax_result(acc_s_ref, h)
                    + g(2) * _softmax_result(acc_w_ref, h))
    o_ref[0] = jnp.concatenate(outs, axis=0).T


def _nsa_attention(qc, wg, kc, vcT, ksel, vselT, kwin, vwinT, B, S):
    n_cmp = (S - CMP_LEN) // CMP_STRIDE + 1
    n_sel = S // SEL_LEN
    topn = min(SEL_TOPN, n_sel)
    ncp = kc.shape[1]
    cs = np.arange(ncp) * CMP_STRIDE
    ss = np.arange(n_sel) * SEL_LEN
    overlapT = ((cs[None, :] < ss[:, None] + SEL_LEN) & (cs[None, :] + CMP_LEN > ss[:, None])
                & (np.arange(ncp)[None, :] < n_cmp)).astype(np.float32)
    k_spec = pl.BlockSpec((1, S, HEAD_DIM), lambda b, i: (b, 0, 0))
    vT_spec = pl.BlockSpec((1, S // TQ, V_ROWS, TQ), lambda b, i: (b, 0, 0, 0))
    return pl.pallas_call(
        functools.partial(_nsa_kernel, n_cmp=n_cmp, n_sel=n_sel, topn=topn),
        grid=(B, S // TQ),
        in_specs=[pl.BlockSpec((1, N_HEADS, TQ, HEAD_DIM), lambda b, i: (b, 0, i, 0)),
                  pl.BlockSpec((1, WG_ROWS, TQ), lambda b, i: (b, 0, i)),
                  pl.BlockSpec((1, ncp, HEAD_DIM), lambda b, i: (b, 0, 0)),
                  pl.BlockSpec((1, HEAD_DIM, ncp), lambda b, i: (b, 0, 0)),
                  _const_spec((n_sel, ncp)), k_spec, vT_spec, k_spec, vT_spec],
        out_specs=pl.BlockSpec((1, TQ, BRANCH_WIDTH), lambda b, i: (b, i, 0)),
        out_shape=jax.ShapeDtypeStruct((B, S, BRANCH_WIDTH), F32),
        scratch_shapes=[pltpu.VMEM((n_sel, TQ), F32), pltpu.VMEM((N_HEADS, V_ROWS, TQ), F32),
                        pltpu.VMEM((N_HEADS, V_ROWS, TQ), F32), pltpu.VMEM((SUBLANES, LANES), F32)],
        compiler_params=_params("parallel", "arbitrary"),
        name="nsa_attention",
    )(qc, wg, kc, vcT, jnp.asarray(overlapT, MXU_DTYPE), ksel, vselT, kwin, vwinT)


def _mem_kv_kernel(m_ref, g_ref, wk_ref, wvT_ref, ks_ref, vT_ref):
    xn = _rmsnorm(m_ref[0], g_ref[...]).astype(MXU_DTYPE)
    M = xn.shape[0]
    k = _mm(xn, wk_ref[...])
    lane_head = lax.broadcasted_iota(jnp.int32, (1, BRANCH_WIDTH), 1) // HEAD_DIM
    for h in range(N_HEADS):
        ks_ref[0, h * M:(h + 1) * M, :] = jnp.where(lane_head == h, k, 0.0).astype(ks_ref.dtype)
    vT_ref[0] = _mm_nt(wvT_ref[...], xn).astype(vT_ref.dtype)


def _mem_kv(mem, g, w_kv):
    B, M, _ = mem.shape
    wk = w_kv[:, :BRANCH_WIDTH].astype(MXU_DTYPE)
    wvT = w_kv[:, BRANCH_WIDTH:].T.astype(MXU_DTYPE)
    return pl.pallas_call(
        _mem_kv_kernel,
        grid=(B,),
        in_specs=[pl.BlockSpec((1, M, D_MODEL), lambda b: (b, 0, 0)), _const_spec((1, D_MODEL)),
                  _const_spec(wk.shape), _const_spec(wvT.shape)],
        out_specs=[pl.BlockSpec((1, N_HEADS * M, BRANCH_WIDTH), lambda b: (b, 0, 0)),
                   pl.BlockSpec((1, BRANCH_WIDTH, M), lambda b: (b, 0, 0))],
        out_shape=[jax.ShapeDtypeStruct((B, N_HEADS * M, BRANCH_WIDTH), MXU_DTYPE),
                   jax.ShapeDtypeStruct((B, BRANCH_WIDTH, M), MXU_DTYPE)],
        compiler_params=_params("parallel"),
        name="memory_kv",
    )(mem, g.reshape(1, D_MODEL), wk, wvT)


FF_CHUNK = 256


def _tail_kernel(h_ref, oa_ref, ob_ref, oc_ref, gmix_ref, wgm_ref, wup_ref, wout_ref,
                 gq_ref, wq_ref, ks_ref, vT_ref, wo_ref, gffn_ref, win_ref, wff_ref, gfin_ref, o_ref,
                 *, final_norm):
    x = h_ref[...]
    xn = _rmsnorm(x, gmix_ref[...]).astype(MXU_DTYPE)
    merged = jnp.zeros((TM, D_MODEL), F32)
    for j, br_ref in enumerate((oa_ref, ob_ref, oc_ref)):
        gate = jax.nn.sigmoid(_mm(xn, wgm_ref[:, j * D_MODEL:(j + 1) * D_MODEL]))
        merged = merged + gate * _mm(br_ref[...], wup_ref[j])
    x = x + _mm(merged, wout_ref[...])

    q = (_mm(_rmsnorm(x, gq_ref[...]), wq_ref[...]) * SOFTMAX_Q_SCALE).astype(MXU_DTYPE)
    M = vT_ref.shape[2]
    s = _mm_nt(ks_ref[0], q)
    outs = []
    for h in range(N_HEADS):
        sh = s[h * M:(h + 1) * M, :]
        p = jnp.exp2(sh - jnp.max(sh, axis=0, keepdims=True))
        o_h = _mm(vT_ref[0, h * HEAD_DIM:(h + 1) * HEAD_DIM, :], p)
        outs.append(o_h / jnp.sum(p, axis=0, keepdims=True))
    x = x + _mm(jnp.concatenate(outs, axis=0).T, wo_ref[...])

    xn = _rmsnorm(x, gffn_ref[...]).astype(MXU_DTYPE)
    for c in range(D_FF // FF_CHUNK):
        lo, hi = c * FF_CHUNK, (c + 1) * FF_CHUNK
        a = _mm(xn, win_ref[:, lo:hi])
        b = _mm(xn, win_ref[:, D_FF + lo:D_FF + hi])
        x = x + _mm(a * jax.nn.sigmoid(a) * b, wff_ref[lo:hi, :])
    if final_norm:
        x = _rmsnorm(x, gfin_ref[...])
    o_ref[...] = x


def _dense_tail(h, o_a, o_b, o_c, g_mix, w_gm, w_up, w_out, g_q, w_q, ks, vT, w_o,
                g_ffn, w_ffn_in, w_ffn_out, g_final, final_norm, S):
    T = h.shape[0]
    nb = S // TM
    row = lambda w: pl.BlockSpec((TM, w), lambda i: (i, 0))
    gain = _const_spec((1, D_MODEL))
    w_gm, w_up, w_out, w_q, w_o, w_ffn_in, w_ffn_out = [
        w.astype(MXU_DTYPE) for w in (w_gm, w_up, w_out, w_q, w_o, w_ffn_in, w_ffn_out)]
    per_batch = lambda a: pl.BlockSpec((1,) + a.shape[1:], lambda i: (i // nb, 0, 0))
    g = lambda v: v.reshape(1, D_MODEL)
    return pl.pallas_call(
        functools.partial(_tail_kernel, final_norm=final_norm),
        grid=(T // TM,),
        in_specs=[row(D_MODEL), row(BRANCH_WIDTH), row(BRANCH_WIDTH), row(BRANCH_WIDTH),
                  gain, _const_spec(w_gm.shape), _const_spec(w_up.shape), _const_spec(w_out.shape),
                  gain, _const_spec(w_q.shape), per_batch(ks), per_batch(vT), _const_spec(w_o.shape),
                  gain, _const_spec(w_ffn_in.shape), _const_spec(w_ffn_out.shape), gain],
        out_specs=row(D_MODEL),
        out_shape=jax.ShapeDtypeStruct((T, D_MODEL), F32),
        compiler_params=_params("parallel"),
        name="dense_tail",
    )(h, o_a, o_b, o_c, g(g_mix), w_gm, w_up, w_out, g(g_q), w_q, ks, vT, w_o,
      g(g_ffn), w_ffn_in, w_ffn_out, g(g_final))


def _rope_tables(S):
    inv = ROPE_THETA ** (-jnp.arange(0, HEAD_DIM, 2, dtype=F32) / HEAD_DIM)
    ang = jnp.arange(S, dtype=F32)[:, None] * inv[None, :]
    cos, sin = jnp.cos(ang), jnp.sin(ang)
    return (jnp.concatenate([cos, cos, cos, cos], axis=1),
            jnp.concatenate([-sin, sin, -sin, sin], axis=1))


def _mixers(h, B, S, g, w_in, cmp_k, cmp_v, cos_t, sin_t):
    (qa, ka, vaT, qb, qidx, qc, kb, kidx, ksel, kwin, vbT, vselT, vwinT, wg, ck, cv) = _in_projection(
        h, g, w_in, cos_t, sin_t, B, S)
    o_a = _stick_breaking(qa, ka, vaT)
    o_b = _dsa_attention(qidx, kidx, wg, qb, kb, vbT, B, S)
    kc = _compress(ck, *cmp_k, transpose_out=False)
    vcT = _compress(cv, *cmp_v, transpose_out=True)
    o_c = _nsa_attention(qc, wg, kc, vcT, ksel, vselT, kwin, vwinT, B, S)
    return [o.reshape(B * S, BRANCH_WIDTH) for o in (o_a, o_b, o_c)]


def kernel(x, mem, norm_mix, w_in, cmp_pos_k, cmp_w1_k, cmp_w2_k, cmp_pos_v, cmp_w1_v, cmp_w2_v,
           w_up, w_out, norm_mem_q, norm_mem_kv, w_mem_q, w_mem_kv, w_mem_o,
           norm_ffn, w_ffn_in, w_ffn_out, norm_final):
    B, S, D = x.shape
    depth = w_in.shape[0]
    assert D == D_MODEL and S % TM == 0 and S % TQ == 0 and WINDOW % TQ == 0
    cos_t, sin_t = _rope_tables(S)
    h = x.reshape(B * S, D)
    for l in range(depth):
        o_a, o_b, o_c = _mixers(h, B, S, norm_mix[l], w_in[l],
                                (cmp_pos_k[l], cmp_w1_k[l], cmp_w2_k[l]),
                                (cmp_pos_v[l], cmp_w1_v[l], cmp_w2_v[l]), cos_t, sin_t)
        ks, vT = _mem_kv(mem, norm_mem_kv[l], w_mem_kv[l])
        h = _dense_tail(h, o_a, o_b, o_c, norm_mix[l], w_in[l][:, _OFF['g_merge'][0]:], w_up[l], w_out[l],
                        norm_mem_q[l], w_mem_q[l], ks, vT, w_mem_o[l],
                        norm_ffn[l], w_ffn_in[l], w_ffn_out[l], norm_final, l == depth - 1, S)
    return h.reshape(B, S, D)
```

```python
import functools

import jax
import jax.numpy as jnp
import numpy as np
from jax import lax
from jax.experimental import pallas as pl
from jax.experimental.pallas import tpu as pltpu

F32 = jnp.float32
MXU_DTYPE = jnp.bfloat16

D_MODEL = 1024
HEAD_DIM = 64
N_HEADS = 4
BRANCH_WIDTH = N_HEADS * HEAD_DIM
IDX_HEADS = 4
DSA_TOPK = 256
CMP_LEN = 32
CMP_STRIDE = 16
CMP_HIDDEN = 128
SEL_LEN = 64
SEL_TOPN = 16
WINDOW = 512
D_FF = 2816
ROPE_THETA = 10000.0
RMS_EPS = 1e-6
FORCED_SCORE = 1e4
Q_SCALE = HEAD_DIM ** -0.5
LOG2E = 1.4426950408889634
SOFTMAX_Q_SCALE = Q_SCALE * LOG2E
W_IDX_SCALE = (IDX_HEADS * HEAD_DIM) ** -0.5
NEG = -1e30

LANES = 128
SUBLANES = 8
VMEM_LIMIT = 56 * 1024 * 1024
TM = 512
TQ = 256

_OFF = {}
_o = 0
for _n, _w in (('q_a', 256), ('k_a', 256), ('v_a', 256), ('q_b', 256), ('k_b', 64), ('v_b', 64),
               ('q_idx', 256), ('k_idx', 64), ('w_idx', 4), ('q_c', 256), ('k_cmp', 64),
               ('v_cmp', 64), ('k_sel', 64), ('v_sel', 64), ('k_win', 64), ('v_win', 64),
               ('g_c', 12), ('g_merge', 3 * D_MODEL)):
    _OFF[_n] = (_o, _o + _w)
    _o += _w
D_IN = _o


def _mm(a, b):
    return jnp.dot(a.astype(MXU_DTYPE), b.astype(MXU_DTYPE), preferred_element_type=F32)


def _mm_nt(a, b):
    return lax.dot_general(a.astype(MXU_DTYPE), b.astype(MXU_DTYPE),
                           (((1,), (1,)), ((), ())), preferred_element_type=F32)


def _params(*sem):
    return pltpu.CompilerParams(dimension_semantics=sem, vmem_limit_bytes=VMEM_LIMIT)


def _const_spec(shape):
    nd = len(shape)
    return pl.BlockSpec(shape, lambda *_: (0,) * nd, pipeline_mode=pl.Buffered(1))


def _rmsnorm(x, g):
    return x * lax.rsqrt(jnp.mean(x * x, axis=-1, keepdims=True) + RMS_EPS) * g


def _rope_lanes(x, cos, sin):
    lane = lax.broadcasted_iota(jnp.int32, (1, LANES), 1)
    first_half = (lane % HEAD_DIM) < (HEAD_DIM // 2)
    out = []
    for g in range(x.shape[1] // LANES):
        xg = x[:, g * LANES:(g + 1) * LANES]
        rot = jnp.where(first_half, pltpu.roll(xg, LANES - HEAD_DIM // 2, 1),
                        pltpu.roll(xg, HEAD_DIM // 2, 1))
        out.append(xg * cos + rot * sin)
    return out


WG_ROW = BRANCH_WIDTH + 3 * HEAD_DIM
WG_ROWS = IDX_HEADS + 3 * N_HEADS
WT_ROWS = 512
V_ROWS = HEAD_DIM + 2 * SUBLANES


def _inproj_kernel(h_ref, g_ref, cos_ref, sin_ref, wa_ref, wr_ref, wc_ref, wt_ref,
                   qa_ref, ka_ref, vaT_ref, qb_ref, qi_ref, qc_ref, kb_ref, ki_ref, ks_ref, kw_ref,
                   vbT_ref, vsT_ref, vwT_ref, wg_ref, ck_ref, cv_ref):
    xn = _rmsnorm(h_ref[...], g_ref[...]).astype(MXU_DTYPE)
    cos, sin = cos_ref[...], sin_ref[...]
    head = lambda y, h: y[:, h * HEAD_DIM:(h + 1) * HEAD_DIM]

    ya = _mm(xn, wa_ref[...])
    for h in range(N_HEADS):
        qa_ref[0, h] = (head(ya, h) * Q_SCALE).astype(qa_ref.dtype)
        ka_ref[0, h] = head(ya, N_HEADS + h).astype(ka_ref.dtype)

    yr = _rope_lanes(_mm(xn, wr_ref[...]), cos, sin)
    for g, (q_ref, scale) in enumerate(((qb_ref, SOFTMAX_Q_SCALE), (qi_ref, 1.0),
                                        (qc_ref, SOFTMAX_Q_SCALE))):
        for half in range(2):
            og = yr[2 * g + half] * scale
            q_ref[0, 2 * half] = head(og, 0).astype(q_ref.dtype)
            q_ref[0, 2 * half + 1] = head(og, 1).astype(q_ref.dtype)
    kb_ref[0] = head(yr[6], 0).astype(kb_ref.dtype)
    ki_ref[0] = head(yr[6], 1).astype(ki_ref.dtype)
    ks_ref[0] = head(yr[7], 0).astype(ks_ref.dtype)
    kw_ref[0] = head(yr[7], 1).astype(kw_ref.dtype)

    yc = _mm(xn, wc_ref[...])
    ck_ref[0] = head(_rope_lanes(yc, cos, sin)[0], 0)
    cv_ref[0] = head(yc, 1)

    yT = _mm_nt(wt_ref[...], xn)
    ones_row = (lax.broadcasted_iota(jnp.int32, (V_ROWS - HEAD_DIM, TQ), 0) == 0).astype(vbT_ref.dtype)
    for c in range(TM // TQ):
        cols = slice(c * TQ, (c + 1) * TQ)
        for h in range(N_HEADS):
            vaT_ref[0, h, c] = yT[h * HEAD_DIM:(h + 1) * HEAD_DIM, cols].astype(vaT_ref.dtype)
        for n, v_ref in enumerate((vbT_ref, vsT_ref, vwT_ref)):
            lo = BRANCH_WIDTH + n * HEAD_DIM
            v_ref[0, c, 0:HEAD_DIM, :] = yT[lo:lo + HEAD_DIM, cols].astype(v_ref.dtype)
            v_ref[0, c, HEAD_DIM:, :] = ones_row
    ys = yT[WG_ROW:WG_ROW + WG_ROWS, :]
    row = lax.broadcasted_iota(jnp.int32, ys.shape, 0)
    wg_ref[0] = jnp.where(row < IDX_HEADS, ys * W_IDX_SCALE, jax.nn.sigmoid(ys))


def _in_projection(h, g, w_in, cos_t, sin_t, B, S):
    cols = lambda *names: jnp.concatenate([w_in[:, _OFF[n][0]:_OFF[n][1]] for n in names], axis=1)
    wa = cols('q_a', 'k_a').astype(MXU_DTYPE)
    wr = cols('q_b', 'q_idx', 'q_c', 'k_b', 'k_idx', 'k_sel', 'k_win').astype(MXU_DTYPE)
    wc = cols('k_cmp', 'v_cmp').astype(MXU_DTYPE)
    wt = cols('v_a', 'v_b', 'v_sel', 'v_win', 'w_idx', 'g_c')
    wt = jnp.pad(wt, ((0, 0), (0, WT_ROWS - wt.shape[1]))).T.astype(MXU_DTYPE)
    nb, nc, cpb = S // TM, S // TQ, TM // TQ
    bf = lambda *shape: jax.ShapeDtypeStruct(shape, MXU_DTYPE)
    f32 = lambda *shape: jax.ShapeDtypeStruct(shape, F32)
    heads_spec = pl.BlockSpec((1, N_HEADS, TM, HEAD_DIM), lambda i: (i // nb, 0, i % nb, 0))
    tok_spec = pl.BlockSpec((1, TM, HEAD_DIM), lambda i: (i // nb, i % nb, 0))
    vT_spec = pl.BlockSpec((1, cpb, V_ROWS, TQ), lambda i: (i // nb, i % nb, 0, 0))
    pos = pl.BlockSpec((TM, LANES), lambda i: (i % nb, 0))
    q4, k3, v4 = bf(B, N_HEADS, S, HEAD_DIM), bf(B, S, HEAD_DIM), bf(B, nc, V_ROWS, TQ)
    return pl.pallas_call(
        _inproj_kernel,
        grid=(B * nb,),
        in_specs=[pl.BlockSpec((TM, D_MODEL), lambda i: (i, 0)), _const_spec((1, D_MODEL)), pos, pos,
                  _const_spec(wa.shape), _const_spec(wr.shape), _const_spec(wc.shape),
                  _const_spec(wt.shape)],
        out_specs=[heads_spec, heads_spec,
                   pl.BlockSpec((1, N_HEADS, cpb, HEAD_DIM, TQ), lambda i: (i // nb, 0, i % nb, 0, 0)),
                   heads_spec, heads_spec, heads_spec, tok_spec, tok_spec, tok_spec, tok_spec,
                   vT_spec, vT_spec, vT_spec,
                   pl.BlockSpec((1, WG_ROWS, TM), lambda i: (i // nb, 0, i % nb)), tok_spec, tok_spec],
        out_shape=[q4, q4, bf(B, N_HEADS, nc, HEAD_DIM, TQ), q4, q4, q4, k3, k3, k3, k3, v4, v4, v4,
                   f32(B, WG_ROWS, S), f32(B, S, HEAD_DIM), f32(B, S, HEAD_DIM)],
        compiler_params=_params("parallel"),
        name="in_projection",
    )(h, g.reshape(1, D_MODEL), cos_t, sin_t, wa, wr, wc, wt)


def _chunk_slice(j):
    return pl.ds(pl.multiple_of(j * TQ, TQ), TQ)


EXP_UNDERFLOW = 110.0
SCORE_BOUND_SLACK = 1.05


def _sb_kernel(q_ref, k_ref, vT_ref, tri_ref, o_ref, acc_ref, kmax_ref):
    i = pl.program_id(1)
    row = lax.broadcasted_iota(jnp.int32, (TQ, TQ), 0)
    col = lax.broadcasted_iota(jnp.int32, (TQ, TQ), 1)
    strict = row < col
    tri = tri_ref[...]
    qs = [q_ref[0, h] for h in range(N_HEADS)]
    acc_ref[...] = jnp.zeros(acc_ref.shape, F32)

    def chunks(js, cs, diag=False):
        heads = range(N_HEADS)
        zs = [[_mm_nt(k_ref[0, h, _chunk_slice(j), :], qs[h]) for h in heads] for j in js]
        cums = []
        for zj in zs:
            cums.append([])
            for z in zj:
                sp = jnp.maximum(z, 0.0) + jnp.log(1.0 + jnp.exp(-jnp.abs(z)))
                if diag:
                    sp = jnp.where(strict, sp, 0.0)
                hi = sp.astype(MXU_DTYPE)
                lo = (sp - hi.astype(F32)).astype(MXU_DTYPE)
                cums[-1].append(_mm(tri, hi) + _mm(tri, lo))
        outs = [jnp.zeros((HEAD_DIM, TQ), F32) for _ in heads]
        cs = list(cs)
        for n, j in enumerate(js):
            for h in heads:
                w = jnp.exp(zs[n][h] - cums[n][h] - cs[h])
                if diag:
                    w = jnp.where(strict, w, 0.0)
                outs[h] = outs[h] + _mm(vT_ref[0, h, j], w)
                cs[h] = cs[h] + cums[n][h][0:1, :]
        for h in heads:
            acc_ref[h] += outs[h]
        return tuple(cs)

    ones = jnp.ones((SUBLANES, HEAD_DIM), F32)

    def sq_norms(x):
        xf = x.astype(F32)
        return _mm_nt(ones, xf * xf)[0:1, :]

    @pl.when(i == 0)
    def _():
        for h in range(N_HEADS):
            def body(j, m):
                return jnp.maximum(m, sq_norms(k_ref[0, h, _chunk_slice(j), :]))
            m = lax.fori_loop(0, k_ref.shape[2] // TQ, body, jnp.zeros((1, TQ), F32))
            kmax_ref[h] = jnp.full((SUBLANES, LANES), jnp.max(m), F32)

    zbound = [jnp.sqrt(sq_norms(qs[h]) * kmax_ref[h][0:1, 0:1]) * SCORE_BOUND_SLACK for h in range(N_HEADS)]

    def negligible(cs):
        slack = cs[0] - zbound[0]
        for h in range(1, N_HEADS):
            slack = jnp.minimum(slack, cs[h] - zbound[h])
        return (jnp.min(slack) > EXP_UNDERFLOW).astype(jnp.int32)

    cs = chunks([i], tuple(jnp.zeros((1, TQ), F32) for _ in range(N_HEADS)), diag=True)

    def more(carry):
        j, done, _ = carry
        return (j >= 0) & (done == 0)

    def step(carry):
        j, _, cs = carry
        cs = chunks([j], cs)
        return j - 1, negligible(cs), cs

    lax.while_loop(more, step, (i - 1, negligible(cs), cs))
    o_ref[0] = acc_ref[...].reshape(BRANCH_WIDTH, TQ).T


def _stick_breaking(q, k, vT):
    B, H, S, _ = q.shape
    tri = jnp.asarray(np.triu(np.ones((TQ, TQ), np.float32)), MXU_DTYPE)
    return pl.pallas_call(
        _sb_kernel,
        grid=(B, S // TQ),
        in_specs=[pl.BlockSpec((1, H, TQ, HEAD_DIM), lambda b, i: (b, 0, i, 0)),
                  pl.BlockSpec((1, H, S, HEAD_DIM), lambda b, i: (b, 0, 0, 0)),
                  pl.BlockSpec((1, H, S // TQ, HEAD_DIM, TQ), lambda b, i: (b, 0, 0, 0, 0)),
                  _const_spec((TQ, TQ))],
        out_specs=pl.BlockSpec((1, TQ, BRANCH_WIDTH), lambda b, i: (b, i, 0)),
        out_shape=jax.ShapeDtypeStruct((B, S, BRANCH_WIDTH), F32),
        scratch_shapes=[pltpu.VMEM((H, HEAD_DIM, TQ), F32), pltpu.VMEM((H, SUBLANES, LANES), F32)],
        compiler_params=_params("parallel", "arbitrary"),
        name="stick_breaking",
    )(q, k, vT, tri)


COUNT_ROWS = 64
PACKED_ROWS = 2 * SUBLANES
FINE_STEPS = 17


def _softmax_step(ks, qs, biases, vTs, ms, acc_ref):
    heads = range(len(qs))
    ss = [[_mm_nt(k, q) + b for q in qs] for k, b in zip(ks, biases)]
    m_new = []
    for h in heads:
        m = ms[h]
        for st in ss:
            m = jnp.maximum(m, jnp.max(st[h], axis=0, keepdims=True))
        m_new.append(m)
    pvs = []
    for h in heads:
        pv = jnp.zeros((V_ROWS, TQ), F32)
        for st, vT in zip(ss, vTs):
            pv = pv + _mm(vT, jnp.exp2(st[h] - m_new[h]))
        pvs.append(pv)
    for h in heads:
        acc_ref[h] = jnp.exp2(ms[h] - m_new[h]) * acc_ref[h] + pvs[h]
    return tuple(m_new)


def _for_chunks(lo, hi, step, ms):
    odd = (hi - lo) % 2
    ms = lax.cond(odd == 1, lambda ms: step([lo], ms), lambda ms: ms, ms)
    first = lo + odd
    return lax.fori_loop(0, (hi - lo) // 2,
                         lambda t, ms: step([first + 2 * t, first + 2 * t + 1], ms), ms)


def _softmax_result(acc_ref, h):
    return acc_ref[h, 0:HEAD_DIM, :] / acc_ref[h, HEAD_DIM:HEAD_DIM + 1, :]


MAX_FIXED_SHIFT = 45.0


def _sq_norms(x):
    xf = x.astype(F32)
    return _mm_nt(jnp.ones((SUBLANES, HEAD_DIM), F32), xf * xf)[0:1, :]


def _max_sq_norm(k_ref):
    def body(j, m):
        return jnp.maximum(m, _sq_norms(k_ref[0, _chunk_slice(j), :]))
    return jnp.max(lax.fori_loop(0, k_ref.shape[1] // TQ, body, jnp.zeros((1, TQ), F32)))


def _score_bound(qs, kmax2):
    qn2 = _sq_norms(qs[0])
    for q in qs[1:]:
        qn2 = jnp.maximum(qn2, _sq_norms(q))
    return jnp.sqrt(qn2 * kmax2) * SCORE_BOUND_SLACK


def _masked_attention(lo, hi, load, bias_fn, qs, bound, acc_ref, triple=False):
    acc_ref[...] = jnp.zeros(acc_ref.shape, F32)
    fixed = jnp.max(bound) < MAX_FIXED_SHIFT

    def run(step, carry):
        if not triple:
            return _for_chunks(lo, hi, step, carry)
        return lax.cond(hi - lo == 3, lambda c: step([lo, lo + 1, lo + 2], c),
                        lambda c: _for_chunks(lo, hi, step, c), carry)

    @pl.when(fixed)
    def _():
        def step(js, carry):
            tiles = [load(j) for j in js]
            biases = [bias_fn(j, -bound) for j in js]
            ss = [[_mm_nt(k, q) + b for q in qs] for (k, _), b in zip(tiles, biases)]
            for h in range(len(qs)):
                pv = jnp.zeros((V_ROWS, TQ), F32)
                for st, (_, vT) in zip(ss, tiles):
                    pv = pv + _mm(vT, jnp.exp2(st[h]))
                acc_ref[h] += pv
            return carry

        run(step, 0)

    @pl.when(jnp.logical_not(fixed))
    def _():
        def step(js, ms):
            tiles = [load(j) for j in js]
            return _softmax_step([k for k, _ in tiles], qs, [bias_fn(j, 0.0) for j in js],
                                 [vT for _, vT in tiles], ms, acc_ref)

        run(step, tuple(jnp.full((1, TQ), NEG, F32) for _ in qs))


def _dsa_kernel(qi_ref, ki_ref, wg_ref, q_ref, k_ref, vT_ref, tril_ref, o_ref, sc_ref, rd_ref, acc_ref,
                kmax_ref, *, topk):
    i = pl.program_id(1)
    nchunks = i + 1
    qpos = i * TQ + lax.broadcasted_iota(jnp.int32, (1, TQ), 1)
    rowi = lax.broadcasted_iota(jnp.int32, (TQ, TQ), 0)
    int_min = jnp.int32(-2 ** 31)

    qis = [qi_ref[0, hh] for hh in range(IDX_HEADS)]
    wT = wg_ref[0]

    def idx_chunks(js, carry, diag=False):
        rs = [[_mm_nt(ki_ref[0, _chunk_slice(j), :], qh) for qh in qis] for j in js]
        for j, rj in zip(js, rs):
            sc = jnp.zeros((TQ, TQ), F32)
            for hh in range(IDX_HEADS):
                sc = sc + wT[hh:hh + 1, :] * jnp.maximum(rj[hh], 0.0)
            if diag:
                sc = jnp.where(rowi <= qpos - i * TQ, sc, -jnp.inf)
            sc_ref[_chunk_slice(j), :] = sc
            rd_ref[_chunk_slice(j), :] = sc.astype(jnp.bfloat16)
        return carry

    _for_chunks(0, i, idx_chunks, 0)
    idx_chunks([i], 0, diag=True)

    def walk(odd, step, carry):
        def pair(t, carry):
            return step(2 * t + 1, step(2 * t, carry))
        carry = lax.fori_loop(0, nchunks // 2, pair, carry)
        return step(nchunks - 1, carry) if odd else carry

    def count_ge(thr, odd):
        def step(j, acc):
            for r in range(TQ // COUNT_ROWS):
                base = pl.multiple_of(j * TQ + r * COUNT_ROWS, COUNT_ROWS)
                ind = jnp.where(sc_ref[pl.ds(base, COUNT_ROWS), :] >= thr, 1.0, 0.0)
                acc = acc + ind.reshape(COUNT_ROWS // SUBLANES, SUBLANES, TQ).sum(axis=0)
            return acc
        return jnp.sum(walk(odd, step, jnp.zeros((SUBLANES, TQ), F32)), axis=0, keepdims=True)

    def count_rounded_ge(thr16, odd):
        one, zero = jnp.ones((), jnp.bfloat16), jnp.zeros((), jnp.bfloat16)

        def step(j, acc):
            ind = jnp.where(rd_ref[_chunk_slice(j), :] >= thr16, one, zero)
            parts = [ind[g * PACKED_ROWS:(g + 1) * PACKED_ROWS] for g in range(TQ // PACKED_ROWS)]
            while len(parts) > 1:
                parts = [a + b for a, b in zip(parts[0::2], parts[1::2])]
            return acc + parts[0].astype(F32)
        return jnp.sum(walk(odd, step, jnp.zeros((PACKED_ROWS, TQ), F32)), axis=0, keepdims=True)

    take_all = qpos < topk

    def search(odd):
        def key_to_float(s):
            return lax.bitcast_convert_type(jnp.where(s < 0, s ^ jnp.int32(0x7FFFFFFF), s), F32)

        def coarse(t, res):
            cand = res | jnp.left_shift(jnp.int32(1), 31 - t)
            bits = lax.bitcast_convert_type(key_to_float(cand ^ int_min), jnp.int32) & jnp.int32(-65536)
            thr16 = lax.bitcast_convert_type(bits, F32).astype(jnp.bfloat16)
            return jnp.where(count_rounded_ge(thr16, odd) >= topk, cand, res)

        s16 = lax.fori_loop(0, 16, coarse, jnp.zeros((1, TQ), jnp.int32)) ^ int_min
        lo = jnp.where(s16 < 0, s16 - 1, s16 - 65536)
        hi = lo + 131072

        def fine(_, carry):
            lo, hi, n_ge, n_gt = carry
            mid = lo + jnp.right_shift(hi - lo, 1)
            n = count_ge(key_to_float(mid), odd)
            ok = n >= topk
            return jnp.where(ok, mid, lo), jnp.where(ok, hi, mid), jnp.where(ok, n, n_ge), jnp.where(ok, n_gt, n)

        lo, hi, n_ge, n_gt = lax.fori_loop(
            0, FINE_STEPS, fine,
            (lo, hi, jnp.full((1, TQ), float(topk), F32), count_ge(key_to_float(hi), odd)))
        thr = key_to_float(lo)
        zero = jnp.zeros((1, TQ), F32)

        @pl.when(jnp.max(jnp.where(take_all, 0.0, n_ge)) > topk)
        def _():
            need = jnp.where(take_all, float(2 ** 30), topk - n_gt)
            tril = tril_ref[...]

            def drop(js, before):
                blks = [sc_ref[_chunk_slice(j), :] for j in js]
                counts = [_mm(tril, jnp.where(blk == thr, 1.0, 0.0)) for blk in blks]
                for j, blk, cnt in zip(js, blks, counts):
                    rank = cnt + before
                    sc_ref[_chunk_slice(j), :] = jnp.where((blk == thr) & (rank > need), -jnp.inf, blk)
                    before = rank[TQ - 1:TQ, :]
                return before

            _for_chunks(0, nchunks, drop, zero)

        return thr

    lowest = float(np.finfo(np.float32).min)
    thr = lax.switch(jnp.where(i * TQ + TQ > topk, 1 + nchunks % 2, 0),
                     [lambda: jnp.full((1, TQ), lowest, F32),
                      functools.partial(search, False), functools.partial(search, True)])
    thr = jnp.where(take_all, lowest, thr)

    @pl.when(i == 0)
    def _():
        kmax_ref[...] = jnp.full(kmax_ref.shape, _max_sq_norm(k_ref), F32)

    qs = [q_ref[0, h] for h in range(N_HEADS)]
    _masked_attention(0, nchunks,
                      lambda j: (k_ref[0, _chunk_slice(j), :], vT_ref[0, j]),
                      lambda j, on: jnp.where(sc_ref[_chunk_slice(j), :] >= thr, on, NEG),
                      qs, _score_bound(qs, kmax_ref[0:1, 0:1]), acc_ref)
    o_ref[0] = jnp.concatenate([_softmax_result(acc_ref, h) for h in range(N_HEADS)], axis=0).T


def _dsa_attention(qidx, kidx, wg, qb, kb, vbT, B, S):
    topk = min(DSA_TOPK, S // 4)
    assert topk <= TQ
    nc = S // TQ
    q_spec = pl.BlockSpec((1, N_HEADS, TQ, HEAD_DIM), lambda b, i: (b, 0, i, 0))
    k_spec = pl.BlockSpec((1, S, HEAD_DIM), lambda b, i: (b, 0, 0))
    tril = jnp.asarray(np.tril(np.ones((TQ, TQ), np.float32)), MXU_DTYPE)
    return pl.pallas_call(
        functools.partial(_dsa_kernel, topk=topk),
        grid=(B, nc),
        in_specs=[q_spec, k_spec, pl.BlockSpec((1, WG_ROWS, TQ), lambda b, i: (b, 0, i)), q_spec, k_spec,
                  pl.BlockSpec((1, nc, V_ROWS, TQ), lambda b, i: (b, 0, 0, 0)), _const_spec((TQ, TQ))],
        out_specs=pl.BlockSpec((1, TQ, BRANCH_WIDTH), lambda b, i: (b, i, 0)),
        out_shape=jax.ShapeDtypeStruct((B, S, BRANCH_WIDTH), F32),
        scratch_shapes=[pltpu.VMEM((S, TQ), F32), pltpu.VMEM((S, TQ), jnp.bfloat16),
                        pltpu.VMEM((N_HEADS, V_ROWS, TQ), F32), pltpu.VMEM((SUBLANES, LANES), F32)],
        compiler_params=_params("parallel", "arbitrary"),
        name="dsa_attention",
    )(qidx, kidx, wg, qb, kb, vbT, tril)


def _compress_kernel(c_ref, pos_ref, w1_ref, w2_ref, o_ref, *, transpose_out):
    n = c_ref.shape[1] // CMP_STRIDE
    top = jnp.zeros((n, CMP_HIDDEN), F32)
    bot = jnp.zeros((n, CMP_HIDDEN), F32)
    for p in range(CMP_STRIDE):
        x = c_ref[0, pl.ds(p, n, stride=CMP_STRIDE), :]
        q = p + CMP_STRIDE
        top = top + _mm(x + pos_ref[p:p + 1, :], w1_ref[p * HEAD_DIM:(p + 1) * HEAD_DIM, :])
        bot = bot + _mm(x + pos_ref[q:q + 1, :], w1_ref[q * HEAD_DIM:(q + 1) * HEAD_DIM, :])
    hid = jax.nn.gelu(top + pltpu.roll(bot, n - 1, 0), approximate=True)
    out = _mm(hid, w2_ref[...])
    rowi = lax.broadcasted_iota(jnp.int32, out.shape, 0)
    out = jnp.where(rowi < n - 1, out, 0.0)
    if transpose_out:
        out = jnp.concatenate([out, jnp.zeros_like(out)], axis=1).T[0:HEAD_DIM, :]
    o_ref[0] = out.astype(o_ref.dtype)


def _compress(tok, pos_emb, w1, w2, transpose_out):
    B, S, _ = tok.shape
    assert CMP_LEN == 2 * CMP_STRIDE
    nch = S // CMP_STRIDE
    out_block = (1, HEAD_DIM, nch) if transpose_out else (1, nch, HEAD_DIM)
    return pl.pallas_call(
        functools.partial(_compress_kernel, transpose_out=transpose_out),
        grid=(B,),
        in_specs=[pl.BlockSpec((1, S, HEAD_DIM), lambda b: (b, 0, 0)),
                  _const_spec(pos_emb.shape), _const_spec(w1.shape), _const_spec(w2.shape)],
        out_specs=pl.BlockSpec(out_block, lambda b: (b, 0, 0)),
        out_shape=jax.ShapeDtypeStruct((B,) + out_block[1:], MXU_DTYPE),
        compiler_params=_params("parallel"),
        name="nsa_compress",
    )(tok, pos_emb, w1.astype(MXU_DTYPE), w2.astype(MXU_DTYPE))


def _nsa_kernel(q_ref, wg_ref, kc_ref, vcT_ref, ovT_ref, ks_ref, vsT_ref, kw_ref, vwT_ref,
                o_ref, selb_ref, acc_s_ref, acc_w_ref, kmax_ref, *, n_cmp, n_sel, topn):
    i = pl.program_id(1)
    qpos = i * TQ + lax.broadcasted_iota(jnp.int32, (1, TQ), 1)
    rowi = lax.broadcasted_iota(jnp.int32, (TQ, TQ), 0)
    heads = [q_ref[0, h] for h in range(N_HEADS)]
    gates = wg_ref[0]

    ncp = kc_ref.shape[1]
    cidx = lax.broadcasted_iota(jnp.int32, (ncp, TQ), 0)
    cmask = (cidx * CMP_STRIDE + (CMP_LEN - 1) <= qpos) & (cidx < n_cmp)
    kc, vcT, ovT = kc_ref[0], vcT_ref[0], ovT_ref[...]
    ss = [jnp.where(cmask, _mm_nt(kc, q), NEG) for q in heads]
    ps = []
    for s in ss:
        m = jnp.max(s, axis=0, keepdims=True)
        e = jnp.exp2(s - m)
        ps.append(e * jnp.where(m > 0.5 * NEG, 1.0 / jnp.sum(e, axis=0, keepdims=True), 0.0))
    vo = jnp.concatenate([vcT, ovT], axis=0)
    o_cmp = []
    imp = jnp.zeros((n_sel, TQ), F32)
    for p in ps:
        p_hi = p.astype(MXU_DTYPE)
        both = _mm(vo, p_hi)
        o_cmp.append(both[0:HEAD_DIM, :])
        imp = imp + both[HEAD_DIM:, :] + _mm(ovT, p - p_hi.astype(F32))

    bidx = lax.broadcasted_iota(jnp.int32, (n_sel, TQ), 0)
    cur = qpos // SEL_LEN
    forced = (bidx == 0) | (bidx == cur) | (bidx == cur - 1)
    work = jnp.where(bidx * SEL_LEN <= qpos, jnp.where(forced, FORCED_SCORE, imp), -jnp.inf)
    sel = jnp.zeros((n_sel, TQ), F32)
    for _ in range(topn):
        m = jnp.max(work, axis=0, keepdims=True)
        first = jnp.min(jnp.where((work == m) & (m > -jnp.inf), bidx, n_sel), axis=0, keepdims=True)
        pick = bidx == first
        sel = jnp.where(pick, 1.0, sel)
        work = jnp.where(pick, -jnp.inf, work)
    selb_ref[...] = jnp.where(sel > 0.0, 0.0, NEG)

    per_chunk = TQ // SEL_LEN

    @pl.when(i == 0)
    def _():
        kmax_ref[...] = jnp.full(kmax_ref.shape, jnp.maximum(_max_sq_norm(ks_ref), _max_sq_norm(kw_ref)),
                                 F32)

    bound = _score_bound(heads, kmax_ref[0:1, 0:1])

    def sel_bias(j, on):
        rows = [jnp.broadcast_to(selb_ref[pl.ds(j * per_chunk + bb, 1), :], (SEL_LEN, TQ))
                for bb in range(per_chunk)]
        return jnp.where(j * TQ + rowi <= qpos, jnp.concatenate(rows, axis=0), NEG) + on

    _masked_attention(0, i + 1, lambda j: (ks_ref[0, _chunk_slice(j), :], vsT_ref[0, j]), sel_bias,
                      heads, bound, acc_s_ref)

    def win_bias(j, on):
        diff = qpos - (j * TQ + rowi)
        return jnp.where((diff >= 0) & (diff < WINDOW), on, NEG)

    _masked_attention(jnp.maximum(i - WINDOW // TQ, 0), i + 1,
                      lambda j: (kw_ref[0, _chunk_slice(j), :], vwT_ref[0, j]), win_bias,
                      heads, bound, acc_w_ref, triple=WINDOW // TQ == 2)

    outs = []
    for h in range(N_HEADS):
        g = lambda br: gates[IDX_HEADS + 3 * h + br:IDX_HEADS + 3 * h + br + 1, :]
        outs.append(g(0) * o_cmp[h] + g(1) * _softmax_result(acc_s_ref, h)
                    + g(2) * _softmax_result(acc_w_ref, h))
    o_ref[0] = jnp.concatenate(outs, axis=0).T


def _nsa_attention(qc, wg, kc, vcT, ksel, vselT, kwin, vwinT, B, S):
    n_cmp = (S - CMP_LEN) // CMP_STRIDE + 1
    n_sel = S // SEL_LEN
    topn = min(SEL_TOPN, n_sel)
    ncp = kc.shape[1]
    cs = np.arange(ncp) * CMP_STRIDE
    ss = np.arange(n_sel) * SEL_LEN
    overlapT = ((cs[None, :] < ss[:, None] + SEL_LEN) & (cs[None, :] + CMP_LEN > ss[:, None])
                & (np.arange(ncp)[None, :] < n_cmp)).astype(np.float32)
    k_spec = pl.BlockSpec((1, S, HEAD_DIM), lambda b, i: (b, 0, 0))
    vT_spec = pl.BlockSpec((1, S // TQ, V_ROWS, TQ), lambda b, i: (b, 0, 0, 0))
    return pl.pallas_call(
        functools.partial(_nsa_kernel, n_cmp=n_cmp, n_sel=n_sel, topn=topn),
        grid=(B, S // TQ),
        in_specs=[pl.BlockSpec((1, N_HEADS, TQ, HEAD_DIM), lambda b, i: (b, 0, i, 0)),
                  pl.BlockSpec((1, WG_ROWS, TQ), lambda b, i: (b, 0, i)),
                  pl.BlockSpec((1, ncp, HEAD_DIM), lambda b, i: (b, 0, 0)),
                  pl.BlockSpec((1, HEAD_DIM, ncp), lambda b, i: (b, 0, 0)),
                  _const_spec((n_sel, ncp)), k_spec, vT_spec, k_spec, vT_spec],
        out_specs=pl.BlockSpec((1, TQ, BRANCH_WIDTH), lambda b, i: (b, i, 0)),
        out_shape=jax.ShapeDtypeStruct((B, S, BRANCH_WIDTH), F32),
        scratch_shapes=[pltpu.VMEM((n_sel, TQ), F32), pltpu.VMEM((N_HEADS, V_ROWS, TQ), F32),
                        pltpu.VMEM((N_HEADS, V_ROWS, TQ), F32), pltpu.VMEM((SUBLANES, LANES), F32)],
        compiler_params=_params("parallel", "arbitrary"),
        name="nsa_attention",
    )(qc, wg, kc, vcT, jnp.asarray(overlapT, MXU_DTYPE), ksel, vselT, kwin, vwinT)


def _mem_kv_kernel(m_ref, g_ref, wk_ref, wvT_ref, ks_ref, vT_ref):
    xn = _rmsnorm(m_ref[0], g_ref[...]).astype(MXU_DTYPE)
    M = xn.shape[0]
    k = _mm(xn, wk_ref[...])
    lane_head = lax.broadcasted_iota(jnp.int32, (1, BRANCH_WIDTH), 1) // HEAD_DIM
    for h in range(N_HEADS):
        ks_ref[0, h * M:(h + 1) * M, :] = jnp.where(lane_head == h, k, 0.0).astype(ks_ref.dtype)
    vT_ref[0] = _mm_nt(wvT_ref[...], xn).astype(vT_ref.dtype)


def _mem_kv(mem, g, w_kv):
    B, M, _ = mem.shape
    wk = w_kv[:, :BRANCH_WIDTH].astype(MXU_DTYPE)
    wvT = w_kv[:, BRANCH_WIDTH:].T.astype(MXU_DTYPE)
    return pl.pallas_call(
        _mem_kv_kernel,
        grid=(B,),
        in_specs=[pl.BlockSpec((1, M, D_MODEL), lambda b: (b, 0, 0)), _const_spec((1, D_MODEL)),
                  _const_spec(wk.shape), _const_spec(wvT.shape)],
        out_specs=[pl.BlockSpec((1, N_HEADS * M, BRANCH_WIDTH), lambda b: (b, 0, 0)),
                   pl.BlockSpec((1, BRANCH_WIDTH, M), lambda b: (b, 0, 0))],
        out_shape=[jax.ShapeDtypeStruct((B, N_HEADS * M, BRANCH_WIDTH), MXU_DTYPE),
                   jax.ShapeDtypeStruct((B, BRANCH_WIDTH, M), MXU_DTYPE)],
        compiler_params=_params("parallel"),
        name="memory_kv",
    )(mem, g.reshape(1, D_MODEL), wk, wvT)


FF_CHUNK = 256


def _tail_kernel(h_ref, oa_ref, ob_ref, oc_ref, gmix_ref, wgm_ref, wup_ref, wout_ref,
                 gq_ref, wq_ref, ks_ref, vT_ref, wo_ref, gffn_ref, win_ref, wff_ref, gfin_ref, o_ref,
                 *, final_norm):
    x = h_ref[...]
    xn = _rmsnorm(x, gmix_ref[...]).astype(MXU_DTYPE)
    merged = jnp.zeros((TM, D_MODEL), F32)
    for j, br_ref in enumerate((oa_ref, ob_ref, oc_ref)):
        gate = jax.nn.sigmoid(_mm(xn, wgm_ref[:, j * D_MODEL:(j + 1) * D_MODEL]))
        merged = merged + gate * _mm(br_ref[...], wup_ref[j])
    x = x + _mm(merged, wout_ref[...])

    q = (_mm(_rmsnorm(x, gq_ref[...]), wq_ref[...]) * SOFTMAX_Q_SCALE).astype(MXU_DTYPE)
    M = vT_ref.shape[2]
    s = _mm_nt(ks_ref[0], q)
    outs = []
    for h in range(N_HEADS):
        sh = s[h * M:(h + 1) * M, :]
        p = jnp.exp2(sh - jnp.max(sh, axis=0, keepdims=True))
        o_h = _mm(vT_ref[0, h * HEAD_DIM:(h + 1) * HEAD_DIM, :], p)
        outs.append(o_h / jnp.sum(p, axis=0, keepdims=True))
    x = x + _mm(jnp.concatenate(outs, axis=0).T, wo_ref[...])

    xn = _rmsnorm(x, gffn_ref[...]).astype(MXU_DTYPE)
    for c in range(D_FF // FF_CHUNK):
        lo, hi = c * FF_CHUNK, (c + 1) * FF_CHUNK
        a = _mm(xn, win_ref[:, lo:hi])
        b = _mm(xn, win_ref[:, D_FF + lo:D_FF + hi])
        x = x + _mm(a * jax.nn.sigmoid(a) * b, wff_ref[lo:hi, :])
    if final_norm:
        x = _rmsnorm(x, gfin_ref[...])
    o_ref[...] = x


def _dense_tail(h, o_a, o_b, o_c, g_mix, w_gm, w_up, w_out, g_q, w_q, ks, vT, w_o,
                g_ffn, w_ffn_in, w_ffn_out, g_final, final_norm, S):
    T = h.shape[0]
    nb = S // TM
    row = lambda w: pl.BlockSpec((TM, w), lambda i: (i, 0))
    gain = _const_spec((1, D_MODEL))
    w_gm, w_up, w_out, w_q, w_o, w_ffn_in, w_ffn_out = [
        w.astype(MXU_DTYPE) for w in (w_gm, w_up, w_out, w_q, w_o, w_ffn_in, w_ffn_out)]
    per_batch = lambda a: pl.BlockSpec((1,) + a.shape[1:], lambda i: (i // nb, 0, 0))
    g = lambda v: v.reshape(1, D_MODEL)
    return pl.pallas_call(
        functools.partial(_tail_kernel, final_norm=final_norm),
        grid=(T // TM,),
        in_specs=[row(D_MODEL), row(BRANCH_WIDTH), row(BRANCH_WIDTH), row(BRANCH_WIDTH),
                  gain, _const_spec(w_gm.shape), _const_spec(w_up.shape), _const_spec(w_out.shape),
                  gain, _const_spec(w_q.shape), per_batch(ks), per_batch(vT), _const_spec(w_o.shape),
                  gain, _const_spec(w_ffn_in.shape), _const_spec(w_ffn_out.shape), gain],
        out_specs=row(D_MODEL),
        out_shape=jax.ShapeDtypeStruct((T, D_MODEL), F32),
        compiler_params=_params("parallel"),
        name="dense_tail",
    )(h, o_a, o_b, o_c, g(g_mix), w_gm, w_up, w_out, g(g_q), w_q, ks, vT, w_o,
      g(g_ffn), w_ffn_in, w_ffn_out, g(g_final))


def _rope_tables(S):
    inv = ROPE_THETA ** (-jnp.arange(0, HEAD_DIM, 2, dtype=F32) / HEAD_DIM)
    ang = jnp.arange(S, dtype=F32)[:, None] * inv[None, :]
    cos, sin = jnp.cos(ang), jnp.sin(ang)
    return (jnp.concatenate([cos, cos, cos, cos], axis=1),
            jnp.concatenate([-sin, sin, -sin, sin], axis=1))


def _mixers(h, B, S, g, w_in, cmp_k, cmp_v, cos_t, sin_t):
    (qa, ka, vaT, qb, qidx, qc, kb, kidx, ksel, kwin, vbT, vselT, vwinT, wg, ck, cv) = _in_projection(
        h, g, w_in, cos_t, sin_t, B, S)
    o_a = _stick_breaking(qa, ka, vaT)
    o_b = _dsa_attention(qidx, kidx, wg, qb, kb, vbT, B, S)
    kc = _compress(ck, *cmp_k, transpose_out=False)
    vcT = _compress(cv, *cmp_v, transpose_out=True)
    o_c = _nsa_attention(qc, wg, kc, vcT, ksel, vselT, kwin, vwinT, B, S)
    return [o.reshape(B * S, BRANCH_WIDTH) for o in (o_a, o_b, o_c)]


def kernel(x, mem, norm_mix, w_in, cmp_pos_k, cmp_w1_k, cmp_w2_k, cmp_pos_v, cmp_w1_v, cmp_w2_v,
           w_up, w_out, norm_mem_q, norm_mem_kv, w_mem_q, w_mem_kv, w_mem_o,
           norm_ffn, w_ffn_in, w_ffn_out, norm_final):
    B, S, D = x.shape
    depth = w_in.shape[0]
    assert D == D_MODEL and S % TM == 0 and S % TQ == 0 and WINDOW % TQ == 0
    cos_t, sin_t = _rope_tables(S)
    h = x.reshape(B * S, D)
    for l in range(depth):
        o_a, o_b, o_c = _mixers(h, B, S, norm_mix[l], w_in[l],
                                (cmp_pos_k[l], cmp_w1_k[l], cmp_w2_k[l]),
                                (cmp_pos_v[l], cmp_w1_v[l], cmp_w2_v[l]), cos_t, sin_t)
        ks, vT = _mem_kv(mem, norm_mem_kv[l], w_mem_kv[l])
        h = _dense_tail(h, o_a, o_b, o_c, norm_mix[l], w_in[l][:, _OFF['g_merge'][0]:], w_up[l], w_out[l],
                        norm_mem_q[l], w_mem_q[l], ks, vT, w_mem_o[l],
                        norm_ffn[l], w_ffn_in[l], w_ffn_out[l], norm_final, l == depth - 1, S)
    return h.reshape(B, S, D)
```
